```python
import math
import jax
import jax.numpy as jnp
from jax import lax
import numpy as np

D_MODEL = 2048
BATCH = 4
SEQ = 4096
DEPTH = 2

GRID_W = 64
CTX_LEN = 256
EPS = 1e-6
ROPE_BASE = 10000.0
NEG_INF = -1e30

A_HEADS = 4
A_KV_HEADS = 2
A_HEAD_DIM = 128
A_WINDOW = 128
A_BLOCK = 128

R_HEADS = 4
R_QK_DIM = 64
R_V_DIM = 128
R_CHUNK = 128

S_HEADS = 8
S_HEAD_DIM = 64
S_GROUPS = 2
S_STATE = 128
S_CONV = 5
S_CHUNK = 128

M_HEADS = 4
M_Q_RANK = 512
M_KV_RANK = 128
M_NOPE = 128
M_ROPE = 64
M_V = 128
M_BLOCK = 128

D_FF = 4 * D_MODEL
N_BRANCH = 4
N_MOD = 6

A_Q_W = A_HEADS * A_HEAD_DIM
A_KV_W = A_KV_HEADS * A_HEAD_DIM
R_QK_W = R_HEADS * R_QK_DIM
R_V_W = R_HEADS * R_V_DIM
S_INNER = S_HEADS * S_HEAD_DIM
S_BC_W = S_GROUPS * S_STATE
S_XBC_W = S_INNER + 2 * S_BC_W
M_QK_DIM = M_NOPE + M_ROPE
M_OUT_W = M_HEADS * M_V
BRANCH_W = A_Q_W
IN_SPLITS = (A_Q_W, A_KV_W, A_KV_W,
             R_QK_W, R_QK_W, R_V_W, R_V_W,
             S_INNER, S_XBC_W, 2 * S_HEADS,
             M_Q_RANK, M_KV_RANK, M_ROPE,
             N_BRANCH * D_MODEL)
IN_W = sum(IN_SPLITS)

kernel_name = 'hybrid_parallel_mixer_dit_block'


def rms_norm(x, w):
    xf = x.astype(jnp.float32)
    y = xf * lax.rsqrt(jnp.mean(xf * xf, axis=-1, keepdims=True) + EPS)
    return (y * w.astype(jnp.float32)).astype(x.dtype)


def modulate(h, shift, scale):
    return h * (1 + scale) + shift


def rope_1d(x, pos):
    half = x.shape[-1] // 2
    freqs = ROPE_BASE ** (-jnp.arange(half, dtype=jnp.float32) / half)
    ang = pos.astype(jnp.float32)[:, None] * freqs[None, :]
    cos = jnp.cos(ang)[:, None, :].astype(x.dtype)
    sin = jnp.sin(ang)[:, None, :].astype(x.dtype)
    x1, x2 = x[..., :half], x[..., half:]
    return jnp.concatenate([x1 * cos - x2 * sin, x1 * sin + x2 * cos], axis=-1)


def rope_2d(x, row, col):
    half = x.shape[-1] // 2
    return jnp.concatenate([rope_1d(x[..., :half], row), rope_1d(x[..., half:], col)], axis=-1)


def split_cols(p):
    return jnp.split(p, np.cumsum(IN_SPLITS)[:-1].tolist(), axis=-1)


def window_gqa(q_l, k_l, v_l, q_c, k_c, v_c, q_norm_w, k_norm_w, sink, row, col, need_ctx):
    Bsz, S, _ = q_l.shape
    Lc = k_c.shape[1]
    G = A_HEADS // A_KV_HEADS
    nb = S // A_BLOCK
    scale = A_HEAD_DIM ** -0.5
    ql = rope_2d(rms_norm(q_l.reshape(Bsz, S, A_HEADS, A_HEAD_DIM), q_norm_w), row, col)
    kl = rope_2d(rms_norm(k_l.reshape(Bsz, S, A_KV_HEADS, A_HEAD_DIM), k_norm_w), row, col)
    vl = v_l.reshape(Bsz, S, A_KV_HEADS, A_HEAD_DIM)
    kc = rms_norm(k_c.reshape(Bsz, Lc, A_KV_HEADS, A_HEAD_DIM), k_norm_w)
    vc = v_c.reshape(Bsz, Lc, A_KV_HEADS, A_HEAD_DIM)
    sink_g = sink.astype(jnp.float32).reshape(A_KV_HEADS, G)

    def band(t):
        tp = jnp.pad(t, ((0, 0), (A_BLOCK, A_BLOCK), (0, 0), (0, 0)))
        tp = tp.reshape(Bsz, nb + 2, A_BLOCK, A_KV_HEADS, A_HEAD_DIM)
        return jnp.concatenate([tp[:, :-2], tp[:, 1:-1], tp[:, 2:]], axis=2)

    kb, vb = band(kl), band(vl)
    qb = ql.reshape(Bsz, nb, A_BLOCK, A_KV_HEADS, G, A_HEAD_DIM)
    qi = jnp.arange(A_BLOCK)[:, None]
    kj = jnp.arange(3 * A_BLOCK)[None, :]
    in_window = jnp.abs(kj - A_BLOCK - qi) <= A_WINDOW
    kpos = (jnp.arange(nb)[:, None] - 1) * A_BLOCK + jnp.arange(3 * A_BLOCK)[None, :]
    in_seq = (kpos >= 0) & (kpos < S)
    mask = in_window[None] & in_seq[:, None, :]

    s_band = jnp.einsum('bnqhgd,bnkhd->bnhgqk', qb, kb).astype(jnp.float32) * scale
    s_band = jnp.where(mask[None, :, None, None], s_band, NEG_INF)
    s_ctx = jnp.einsum('bnqhgd,bkhd->bnhgqk', qb, kc).astype(jnp.float32) * scale
    sink_b = sink_g[None, None, :, :, None]
    m = jnp.maximum(jnp.maximum(s_band.max(-1), s_ctx.max(-1)), sink_b)
    p_band = jnp.exp(s_band - m[..., None])
    p_ctx = jnp.exp(s_ctx - m[..., None])
    inv = (1.0 / (p_band.sum(-1) + p_ctx.sum(-1) + jnp.exp(sink_b - m)))[..., None]
    o = (jnp.einsum('bnhgqk,bnkhd->bnqhgd', (p_band * inv).astype(vl.dtype), vb)
         + jnp.einsum('bnhgqk,bkhd->bnqhgd', (p_ctx * inv).astype(vc.dtype), vc))
    out_l = o.reshape(Bsz, S, A_Q_W)
    out_c = None
    if need_ctx:
        qc = rms_norm(q_c.reshape(Bsz, Lc, A_KV_HEADS, G, A_HEAD_DIM), q_norm_w)
        s = jnp.einsum('bqhgd,bkhd->bhgqk', qc, kc).astype(jnp.float32) * scale
        sink_c = sink_g[None, :, :, None]
        mc = jnp.maximum(s.max(-1), sink_c)
        p = jnp.exp(s - mc[..., None])
        p = p / (p.sum(-1) + jnp.exp(sink_c - mc))[..., None]
        out_c = jnp.einsum('bhgqk,bkhd->bqhgd', p.astype(vc.dtype), vc).reshape(Bsz, Lc, A_Q_W)
    return out_l, out_c


def retention_scan(q, k, v, log_gamma, state0):
    Bsz, L, H, dk = q.shape
    dv = v.shape[-1]
    n = L // R_CHUNK
    qc = q.reshape(Bsz, n, R_CHUNK, H, dk)
    kc = k.reshape(Bsz, n, R_CHUNK, H, dk)
    vc = v.reshape(Bsz, n, R_CHUNK, H, dv)
    pos = jnp.arange(R_CHUNK, dtype=jnp.float32)
    diff = pos[:, None] - pos[None, :]
    decay_mask = jnp.where(diff >= 0, jnp.exp(log_gamma[:, None, None] * jnp.maximum(diff, 0.0)), 0.0)
    scores = jnp.einsum('bnihd,bnjhd->bnhij', qc, kc) * decay_mask
    intra = jnp.einsum('bnhij,bnjhe->bnihe', scores, vc)
    k_dec = kc * jnp.exp(log_gamma[None, :] * (R_CHUNK - 1 - pos)[:, None])[:, :, None]
    kv = jnp.einsum('bnjhd,bnjhe->bnhde', k_dec, vc)
    chunk_decay = jnp.exp(log_gamma * R_CHUNK)[None, :, None, None]

    def step(state, kv_n):
        return chunk_decay * state + kv_n, state

    final, prev = lax.scan(step, state0, jnp.moveaxis(kv, 1, 0))
    prev = jnp.moveaxis(prev, 0, 1)
    q_dec = qc * jnp.exp(log_gamma[None, :] * (pos + 1.0)[:, None])[:, :, None]
    inter = jnp.einsum('bnihd,bnhde->bnihe', q_dec, prev)
    return (intra + inter).reshape(Bsz, L, H, dv), final


def retention(q_l, k_l, v_l, g_l, q_c, k_c, v_c, g_c, decay, norm_w, t_pos, need_ctx):
    def heads(q, k, v):
        b, L, _ = q.shape
        return (q.reshape(b, L, R_HEADS, R_QK_DIM),
                k.reshape(b, L, R_HEADS, R_QK_DIM) * (R_QK_DIM ** -0.5),
                v.reshape(b, L, R_HEADS, R_V_DIM))

    ql, kl, vl = heads(q_l, k_l, v_l)
    ql, kl = rope_1d(ql, t_pos), rope_1d(kl, t_pos)
    qc, kc, vc = heads(q_c, k_c, v_c)
    log_gamma = jnp.log1p(-jnp.exp(decay.astype(jnp.float32)))
    zero = jnp.zeros((q_l.shape[0], R_HEADS, R_QK_DIM, R_V_DIM), jnp.float32)
    flip = lambda t: t[:, ::-1]
    oc_f, st_f = retention_scan(qc, kc, vc, log_gamma[0], zero)
    oc_b, st_b = retention_scan(flip(qc), flip(kc), flip(vc), log_gamma[1], zero)
    ol = (retention_scan(ql, kl, vl, log_gamma[0], st_f)[0]
          + flip(retention_scan(flip(ql), flip(kl), flip(vl), log_gamma[1], st_b)[0]))

    def finish(o, g):
        b, L = g.shape[:2]
        return rms_norm(o, norm_w).astype(g.dtype).reshape(b, L, R_V_W) * jax.nn.silu(g)

    out_c = finish(oc_f + flip(oc_b), g_c) if need_ctx else None
    return finish(ol, g_l), out_c


def dwconv_silu(u, w, b):
    ch = u.shape[-1]
    pad = (S_CONV - 1) // 2
    y = lax.conv_general_dilated(u, w[:, None, :].astype(u.dtype), window_strides=(1,),
                                 padding=[(pad, pad)], dimension_numbers=('NWC', 'WIO', 'NWC'),
                                 feature_group_count=ch)
    return jax.nn.silu(y + b.astype(u.dtype))


def ssd_scan(x, dt, a, bm, cm, h0):
    Bsz, L, H, P = x.shape
    G, N = bm.shape[2], bm.shape[3]
    Hg = H // G
    Q = S_CHUNK
    n = L // Q
    xr = x.reshape(Bsz, n, Q, G, Hg, P)
    br = bm.reshape(Bsz, n, Q, G, N)
    cr = cm.reshape(Bsz, n, Q, G, N)
    dtT = jnp.moveaxis(dt.reshape(Bsz, n, Q, G, Hg), 2, -1)
    cum = jnp.cumsum(dtT * a.reshape(G, Hg)[..., None], axis=-1)
    tri = jnp.tril(jnp.ones((Q, Q), bool))
    decay_in = jnp.exp(jnp.where(tri, cum[..., :, None] - cum[..., None, :], NEG_INF))
    cb = jnp.einsum('bnigs,bnjgs->bngij', cr, br).astype(jnp.float32)
    w = cb[:, :, :, None] * decay_in * dtT[..., None, :]
    y_diag = jnp.einsum('bnghij,bnjghp->bnighp', w, xr)
    decay_out = jnp.exp(cum[..., -1:] - cum) * dtT
    states = jnp.einsum('bnghj,bnjgs,bnjghp->bnghps', decay_out, br, xr)
    chunk_decay = jnp.exp(cum[..., -1])

    def step(h, inp):
        dec, st = inp
        return dec[..., None, None] * h + st, h

    h_final, h_prev = lax.scan(step, h0.reshape(Bsz, G, Hg, P, N),
                               (jnp.moveaxis(chunk_decay, 1, 0), jnp.moveaxis(states, 1, 0)))
    h_prev = jnp.moveaxis(h_prev, 0, 1)
    y_off = jnp.einsum('bnigs,bnghps,bnghi->bnighp', cr, h_prev, jnp.exp(cum))
    return (y_diag + y_off).reshape(Bsz, L, H, P), h_final.reshape(Bsz, H, P, N)


def ssd_mixer(z_l, xbc_l, dt_l, z_c, xbc_c, dt_c, conv_w, conv_b, a_log, dt_bias, d_skip, norm_w, need_ctx):
    A = -jnp.exp(a_log.astype(jnp.float32))

    def prep(xbc, dt_raw):
        b, L, _ = xbc.shape
        u = dwconv_silu(xbc, conv_w, conv_b)
        xs = u[..., :S_INNER].reshape(b, L, S_HEADS, S_HEAD_DIM)
        bm = u[..., S_INNER:S_INNER + S_BC_W].reshape(b, L, S_GROUPS, S_STATE)
        cm = u[..., S_INNER + S_BC_W:].reshape(b, L, S_GROUPS, S_STATE)
        dt = jax.nn.softplus(dt_raw.astype(jnp.float32).reshape(b, L, 2, S_HEADS) + dt_bias.astype(jnp.float32))
        return xs, bm, cm, dt

    xl, bl, cl, dtl = prep(xbc_l, dt_l)
    xc, bc, cc, dtc = prep(xbc_c, dt_c)
    zero = jnp.zeros((z_l.shape[0], S_HEADS, S_HEAD_DIM, S_STATE), jnp.float32)
    flip = lambda t: t[:, ::-1]
    yc_f, hc_f = ssd_scan(xc, dtc[:, :, 0], A[0], bc, cc, zero)
    yc_b, hc_b = ssd_scan(flip(xc), flip(dtc[:, :, 1]), A[1], flip(bc), flip(cc), zero)
    yl = (ssd_scan(xl, dtl[:, :, 0], A[0], bl, cl, hc_f)[0]
          + flip(ssd_scan(flip(xl), flip(dtl[:, :, 1]), A[1], flip(bl), flip(cl), hc_b)[0]))

    def finish(y, xs, z):
        b, L = z.shape[:2]
        y = (y + d_skip.astype(jnp.float32)[:, None] * xs.astype(jnp.float32)).reshape(b, L, S_INNER)
        return rms_norm(y * jax.nn.silu(z.astype(jnp.float32)), norm_w).astype(z.dtype)

    out_c = finish(yc_f + flip(yc_b), xc, z_c) if need_ctx else None
    return finish(yl, xl, z_l), out_c


def mla(cq_l, ckv_l, kr_l, cq_c, ckv_c, kr_c, cq_norm_w, ckv_norm_w, w_uq, w_ukv, q_norm_w, k_norm_w,
        row, col, need_ctx):
    Bsz, S, _ = cq_l.shape
    scale = M_QK_DIM ** -0.5

    def queries(cq):
        q = (rms_norm(cq, cq_norm_w) @ w_uq).reshape(cq.shape[0], cq.shape[1], M_HEADS, M_QK_DIM)
        return rms_norm(q, q_norm_w)

    def keys_values(ckv, kr):
        b, L, _ = ckv.shape
        kv = (rms_norm(ckv, ckv_norm_w) @ w_ukv).reshape(b, L, M_HEADS, M_NOPE + M_V)
        k_rope = jnp.broadcast_to(kr[:, :, None, :], (b, L, M_HEADS, M_ROPE))
        k = rms_norm(jnp.concatenate([kv[..., :M_NOPE], k_rope], axis=-1), k_norm_w)
        return k, kv[..., M_NOPE:]

    def rotate(t):
        return jnp.concatenate([t[..., :M_NOPE], rope_2d(t[..., M_NOPE:], row, col)], axis=-1)

    def attend(qb, k, v):
        s = jnp.einsum('bqhd,bkhd->bhqk', qb, k).astype(jnp.float32) * scale
        p = jax.nn.softmax(s, axis=-1).astype(v.dtype)
        return jnp.einsum('bhqk,bkhe->bqhe', p, v)

    ql = rotate(queries(cq_l))
    kl, vl = keys_values(ckv_l, kr_l)
    kl = rotate(kl)
    kc, vc = keys_values(ckv_c, kr_c)
    k_all = jnp.concatenate([kc, kl], axis=1)
    v_all = jnp.concatenate([vc, vl], axis=1)
    nb = S // M_BLOCK
    q_blocks = jnp.moveaxis(ql.reshape(Bsz, nb, M_BLOCK, M_HEADS, M_QK_DIM), 1, 0)
    o = lax.map(lambda qb: attend(qb, k_all, v_all), q_blocks)
    out_l = jnp.moveaxis(o, 0, 1).reshape(Bsz, S, M_OUT_W)
    out_c = None
    if need_ctx:
        out_c = attend(queries(cq_c), kc, vc).reshape(Bsz, kc.shape[1], M_OUT_W)
    return out_l, out_c


def merge_branches(ys, gates, w_br, w_out):
    g = jax.nn.sigmoid(gates)
    acc = g[..., :D_MODEL] * (ys[0] @ w_br[0])
    for k in range(1, N_BRANCH):
        acc = acc + g[..., k * D_MODEL:(k + 1) * D_MODEL] * (ys[k] @ w_br[k])
    return acc @ w_out


def sq_relu_mlp(h, w1, w2):
    return jnp.square(jax.nn.relu(h @ w1)) @ w2


def setup_inputs(seed: int = 0) -> dict:
    key = jax.random.key(seed)
    keys = iter(jax.random.split(key, 48))

    def normal(shape, scale):
        return scale * jax.random.normal(next(keys), shape, jnp.float32)

    def gain(shape):
        return 1.0 + 0.05 * jax.random.normal(next(keys), shape, jnp.float32)

    L, D = DEPTH, D_MODEL
    r_base = -(5.0 + jnp.arange(R_HEADS, dtype=jnp.float32)) * math.log(2.0)
    r_decay = r_base + 0.05 * jax.random.normal(next(keys), (L, 2, R_HEADS), jnp.float32)
    s_a_log = jnp.log(jax.random.uniform(next(keys), (L, 2, S_HEADS), jnp.float32, 1.0, 16.0))
    dt0 = jnp.exp(jax.random.uniform(next(keys), (L, 2, S_HEADS), jnp.float32,
                                     math.log(1e-3), math.log(1e-1)))
    s_dt_bias = dt0 + jnp.log(-jnp.expm1(-dt0))
    return {
        'x': normal((BATCH, SEQ, D), 1.0),
        'c': normal((BATCH, D), 1.0),
        'ctx': normal((BATCH, CTX_LEN, D), 1.0),
        'c_ctx': normal((D,), 1.0),
        'w_ada': normal((L, D, N_MOD * D), D ** -0.5),
        'b_ada': normal((L, N_MOD * D), 0.02),
        'norm1_w': gain((L, D)),
        'norm2_w': gain((L, D)),
        'w_in': normal((L, D, IN_W), D ** -0.5),
        'a_q_norm': gain((L, A_HEAD_DIM)),
        'a_k_norm': gain((L, A_HEAD_DIM)),
        'a_sink': normal((L, A_HEADS), 0.5),
        'r_decay': r_decay,
        'r_norm': gain((L, R_HEADS, R_V_DIM)),
        's_conv_w': normal((L, S_CONV, S_XBC_W), S_CONV ** -0.5),
        's_conv_b': normal((L, S_XBC_W), 0.02),
        's_a_log': s_a_log,
        's_dt_bias': s_dt_bias,
        's_d': gain((L, S_HEADS)),
        's_norm': gain((L, S_INNER)),
        'm_cq_norm': gain((L, M_Q_RANK)),
        'm_ckv_norm': gain((L, M_KV_RANK)),
        'm_w_uq': normal((L, M_Q_RANK, M_HEADS * M_QK_DIM), M_Q_RANK ** -0.5),
        'm_w_ukv': normal((L, M_KV_RANK, M_HEADS * (M_NOPE + M_V)), M_KV_RANK ** -0.5),
        'm_q_norm': gain((L, M_QK_DIM)),
        'm_k_norm': gain((L, M_QK_DIM)),
        'w_branch': normal((L, N_BRANCH, BRANCH_W, D), BRANCH_W ** -0.5),
        'w_o': normal((L, D, D), D ** -0.5),
        'w_ff1': normal((L, D, D_FF), D ** -0.5),
        'w_ff2': normal((L, D_FF, D), D_FF ** -0.5),
    }


def reference(x, c, ctx, c_ctx, w_ada, b_ada, norm1_w, norm2_w, w_in, a_q_norm, a_k_norm, a_sink,
              r_decay, r_norm, s_conv_w, s_conv_b, s_a_log, s_dt_bias, s_d, s_norm,
              m_cq_norm, m_ckv_norm, m_w_uq, m_w_ukv, m_q_norm, m_k_norm,
              w_branch, w_o, w_ff1, w_ff2):
    Bsz, S, _ = x.shape
    rows = S // GRID_W
    row = jnp.repeat(jnp.arange(rows), GRID_W)
    col = jnp.tile(jnp.arange(GRID_W), rows)
    t_pos = jnp.arange(S)
    cx = ctx
    for i in range(DEPTH):
        need_ctx = i < DEPTH - 1
        mod_l = (jax.nn.silu(c) @ w_ada[i] + b_ada[i]).reshape(Bsz, N_MOD, 1, D_MODEL)
        mod_c = (jax.nn.silu(c_ctx) @ w_ada[i] + b_ada[i]).reshape(N_MOD, D_MODEL)
        h_l = modulate(rms_norm(x, norm1_w[i]), mod_l[:, 0], mod_l[:, 1])
        h_c = modulate(rms_norm(cx, norm1_w[i]), mod_c[0], mod_c[1])
        (aq_l, ak_l, av_l, rq_l, rk_l, rv_l, rg_l, sz_l, sxbc_l, sdt_l,
         mcq_l, mckv_l, mkr_l, gate_l) = split_cols(h_l @ w_in[i])
        (aq_c, ak_c, av_c, rq_c, rk_c, rv_c, rg_c, sz_c, sxbc_c, sdt_c,
         mcq_c, mckv_c, mkr_c, gate_c) = split_cols(h_c @ w_in[i])
        ya_l, ya_c = window_gqa(aq_l, ak_l, av_l, aq_c, ak_c, av_c, a_q_norm[i], a_k_norm[i], a_sink[i],
                                row, col, need_ctx)
        yb_l, yb_c = retention(rq_l, rk_l, rv_l, rg_l, rq_c, rk_c, rv_c, rg_c, r_decay[i], r_norm[i],
                               t_pos, need_ctx)
        yc_l, yc_c = ssd_mixer(sz_l, sxbc_l, sdt_l, sz_c, sxbc_c, sdt_c, s_conv_w[i], s_conv_b[i],
                               s_a_log[i], s_dt_bias[i], s_d[i], s_norm[i], need_ctx)
        yd_l, yd_c = mla(mcq_l, mckv_l, mkr_l, mcq_c, mckv_c, mkr_c, m_cq_norm[i], m_ckv_norm[i],
                         m_w_uq[i], m_w_ukv[i], m_q_norm[i], m_k_norm[i], row, col, need_ctx)
        x = x + mod_l[:, 2] * merge_branches((ya_l, yb_l, yc_l, yd_l), gate_l, w_branch[i], w_o[i])
        h2 = modulate(rms_norm(x, norm2_w[i]), mod_l[:, 3], mod_l[:, 4])
        x = x + mod_l[:, 5] * sq_relu_mlp(h2, w_ff1[i], w_ff2[i])
        if need_ctx:
            cx = cx + mod_c[2] * merge_branches((ya_c, yb_c, yc_c, yd_c), gate_c, w_branch[i], w_o[i])
            h2c = modulate(rms_norm(cx, norm2_w[i]), mod_c[3], mod_c[4])
            cx = cx + mod_c[5] * sq_relu_mlp(h2c, w_ff1[i], w_ff2[i])
    return x
```

```python
import functools
import math

import jax
import jax.numpy as jnp
import numpy as np
from jax import lax
from jax.experimental import pallas as pl
from jax.experimental.pallas import tpu as pltpu

GRID_W = 64
EPS = 1e-6
ROPE_BASE = 10000.0
NEG_INF = -1e30

A_HEADS, A_KV_HEADS, A_HEAD_DIM, A_BLOCK = 4, 2, 128, 128
R_HEADS, R_QK_DIM, R_V_DIM, R_CHUNK = 4, 64, 128, 128
S_HEADS, S_HEAD_DIM, S_GROUPS, S_STATE, S_CONV, S_CHUNK = 8, 64, 2, 128, 5, 128
M_HEADS, M_Q_RANK, M_KV_RANK, M_NOPE, M_ROPE, M_V = 4, 512, 128, 128, 64, 128
M_QK_DIM = M_NOPE + M_ROPE
N_BRANCH, N_MOD = 4, 6
BRANCH_W = 512

OFF_M, W_M = 0, 768
OFF_DT, W_DT = 768, 256
OFF_A, W_A = 1024, 1024
OFF_R, W_R = 2048, 1536
OFF_S, W_S = 3584, 1536
OFF_G = 5120

VMEM_LIMIT = 56 * 1024 * 1024

BF = jnp.bfloat16
F32 = jnp.float32


def _cparams(sem):
    return pltpu.CompilerParams(dimension_semantics=sem, vmem_limit_bytes=VMEM_LIMIT)


def _dot(a, b):
    return jnp.dot(a.astype(BF), b.astype(BF), preferred_element_type=F32)


def _dot_nt(a, b):
    return lax.dot_general(a.astype(BF), b.astype(BF), (((1,), (1,)), ((), ())),
                           preferred_element_type=F32)


def _silu(x):
    return x * (1.0 / (1.0 + jnp.exp(-x)))


def _rms(x, n):
    return x * lax.rsqrt(jnp.sum(x * x, axis=-1, keepdims=True) * (1.0 / n) + EPS)


def _rope(y, cos, s_lo, s_hi, shift):
    w = y.shape[-1]
    return y * cos + pltpu.roll(y, w - shift, 1) * s_lo + pltpu.roll(y, shift, 1) * s_hi


def _ada_kernel(c_ref, w_ref, b_ref, o_ref):
    o_ref[...] = _dot(_silu(c_ref[...]), w_ref[...]) + b_ref[...]


def _ada(c8, w_ada, b_ada):
    depth, d, n = w_ada.shape
    tn = 512
    return pl.pallas_call(
        _ada_kernel,
        grid=(depth, n // tn),
        in_specs=[pl.BlockSpec((8, d), lambda l, j: (0, 0)),
                  pl.BlockSpec((None, d, tn), lambda l, j: (l, 0, j)),
                  pl.BlockSpec((None, 1, tn), lambda l, j: (l, 0, j))],
        out_specs=pl.BlockSpec((None, 8, tn), lambda l, j: (l, 0, j)),
        out_shape=jax.ShapeDtypeStruct((depth, 8, n), F32),
        compiler_params=_cparams(("parallel", "parallel")),
        name="ada",
    )(c8, w_ada, b_ada.reshape(depth, 1, n))


def _inproj_kernel(x_ref, mod_ref, nw_ref, w_ref, o_ref, h_sc):
    @pl.when(pl.program_id(1) == 0)
    def _():
        x = x_ref[...]
        y = _rms(x, x.shape[-1]) * nw_ref[...]
        h_sc[...] = (y * (1.0 + mod_ref[1:2, :]) + mod_ref[0:1, :]).astype(BF)

    o_ref[...] = jnp.dot(h_sc[...], w_ref[...], preferred_element_type=F32).astype(o_ref.dtype)


def _mod_index(i, tm, nl, s, b):
    return jnp.where(i < nl // tm, i // (s // tm), b)


def _inproj(xs, mod, nw, w, dims, tm, tn):
    b, s, lc, nl, t = dims
    d = xs.shape[1]
    n = w.shape[1]
    return pl.pallas_call(
        _inproj_kernel,
        grid=(t // tm, n // tn),
        in_specs=[pl.BlockSpec((tm, d), lambda i, j: (i, 0)),
                  pl.BlockSpec((None, N_MOD, d), lambda i, j: (_mod_index(i, tm, nl, s, b), 0, 0)),
                  pl.BlockSpec((1, d), lambda i, j: (0, 0)),
                  pl.BlockSpec((d, tn), lambda i, j: (0, j))],
        out_specs=pl.BlockSpec((tm, tn), lambda i, j: (i, j)),
        out_shape=jax.ShapeDtypeStruct((t, n), F32),
        scratch_shapes=[pltpu.VMEM((tm, d), BF)],
        compiler_params=_cparams(("parallel", "arbitrary")),
        name="inproj",
    )(xs, mod, nw, w)


def _aprep_kernel(p_ref, cos_ref, slo_ref, shi_ref, qw_ref, kw_ref, q_ref, k_ref, v_ref):
    cos, slo, shi = cos_ref[...], slo_ref[...], shi_ref[...]
    scale = A_HEAD_DIM ** -0.5
    for h in range(A_HEADS):
        y = _rms(p_ref[:, h * 128:(h + 1) * 128], A_HEAD_DIM) * qw_ref[...]
        q_ref[:, h * 128:(h + 1) * 128] = (_rope(y, cos, slo, shi, 32) * scale).astype(BF)
    for h in range(A_KV_HEADS):
        y = _rms(p_ref[:, 512 + h * 128:512 + (h + 1) * 128], A_HEAD_DIM) * kw_ref[...]
        k_ref[:, h * 128:(h + 1) * 128] = _rope(y, cos, slo, shi, 32).astype(BF)
    v_ref[...] = p_ref[:, 768:1024].astype(BF)


def _table_index(i, tm, nl, s):
    return jnp.where(i < nl // tm, i % (s // tm), s // tm)


def _aprep(p, tabs, qw, kw, dims, tm):
    b, s, lc, nl, t = dims
    tab_spec = pl.BlockSpec((tm, 128), lambda i: (_table_index(i, tm, nl, s), 0))
    return pl.pallas_call(
        _aprep_kernel,
        grid=(t // tm,),
        in_specs=[pl.BlockSpec((tm, W_A), lambda i: (i, OFF_A // W_A)),
                  tab_spec, tab_spec, tab_spec,
                  pl.BlockSpec((1, 128), lambda i: (0, 0)),
                  pl.BlockSpec((1, 128), lambda i: (0, 0))],
        out_specs=[pl.BlockSpec((tm, 512), lambda i: (i, 0)),
                   pl.BlockSpec((tm, 256), lambda i: (i, 0)),
                   pl.BlockSpec((tm, 256), lambda i: (i, 0))],
        out_shape=[jax.ShapeDtypeStruct((t, 512), BF),
                   jax.ShapeDtypeStruct((t, 256), BF),
                   jax.ShapeDtypeStruct((t, 256), BF)],
        compiler_params=_cparams(("parallel",)),
        name="a_prep",
    )(p, *tabs, qw, kw)


def _aattn_kernel(sink_ref, q_ref, kp_ref, kc_ref, kn_ref, vp_ref, vc_ref, vn_ref,
                  kx_ref, vx_ref, o_ref, *, nb):
    n = pl.program_id(1)
    blk = A_BLOCK
    lc = kx_ref.shape[0]
    g = A_HEADS // A_KV_HEADS
    nk = 3 * blk + lc
    qi = lax.broadcasted_iota(jnp.int32, (g * blk, nk), 0) % blk
    kj = lax.broadcasted_iota(jnp.int32, (g * blk, nk), 1)
    lo_seq = jnp.where(n == 0, blk, 0)
    hi_seq = jnp.where(n < nb, jnp.where(n == nb - 1, 2 * blk - 1, 3 * blk - 1), -1)
    mask = ((kj >= jnp.maximum(qi, lo_seq)) & (kj <= jnp.minimum(qi + 2 * blk, hi_seq))) | (kj >= 3 * blk)
    row = lax.broadcasted_iota(jnp.int32, (g * blk, 1), 0)
    for hk in range(A_KV_HEADS):
        sl = slice(hk * 128, (hk + 1) * 128)
        q2 = jnp.concatenate([q_ref[:, (hk * g + gi) * 128:(hk * g + gi + 1) * 128]
                              for gi in range(g)], axis=0)
        kk = jnp.concatenate([kp_ref[:, sl], kc_ref[:, sl], kn_ref[:, sl], kx_ref[:, sl]], axis=0)
        vv = jnp.concatenate([vp_ref[:, sl], vc_ref[:, sl], vn_ref[:, sl], vx_ref[:, sl]], axis=0)
        s = _dot_nt(q2, kk)
        s = jnp.where(mask, s, NEG_INF)
        sink = jnp.where(row < blk, sink_ref[hk * g], sink_ref[hk * g + 1])
        m = jnp.maximum(jnp.max(s, axis=-1, keepdims=True), sink)
        p = jnp.exp(s - m)
        inv = 1.0 / (jnp.sum(p, axis=-1, keepdims=True) + jnp.exp(sink - m))
        o = _dot(p, vv) * inv
        for gi in range(g):
            o_ref[:, (hk * g + gi) * 128:(hk * g + gi + 1) * 128] = (
                o[gi * blk:(gi + 1) * blk]).astype(o_ref.dtype)


def _aattn(sink, qa, ka, va, dims, need_ctx):
    b, s, lc, nl, t = dims
    blk = A_BLOCK
    nb = s // blk
    ncb = lc // blk
    steps = nb + (ncb if need_ctx else 0)
    rows = nl + (b * lc if need_ctx else 0)

    def qrow(bi, n):
        return jnp.where(n < nb, bi * nb + n, nl // blk + bi * ncb + (n - nb))

    def krow(off):
        def f(bi, n):
            return (bi * nb + jnp.clip(n + off, 0, nb - 1), 0)
        return f

    kspec = [pl.BlockSpec((blk, 256), krow(o)) for o in (-1, 0, 1)]
    xspec = pl.BlockSpec((lc, 256), lambda bi, n: (nl // lc + bi, 0))
    return pl.pallas_call(
        functools.partial(_aattn_kernel, nb=nb),
        grid=(b, steps),
        in_specs=[pl.BlockSpec(memory_space=pltpu.SMEM),
                  pl.BlockSpec((blk, 512), lambda bi, n: (qrow(bi, n), 0)),
                  *kspec, *kspec, xspec, xspec],
        out_specs=pl.BlockSpec((blk, 512), lambda bi, n: (qrow(bi, n), 0)),
        out_shape=jax.ShapeDtypeStruct((rows, 512), BF),
        compiler_params=_cparams(("parallel", "arbitrary")),
        name="a_attn",
    )(sink, qa, ka, ka, ka, va, va, va, ka, va)


def _scan_slot(st, ncc, nlc):
    nc = ncc + nlc
    t = st % nc
    bwd = st >= nc
    slot_b = jnp.where(t < ncc, ncc - 1 - t, ncc + (nlc - 1 - (t - ncc)))
    return jnp.where(bwd, slot_b, t)


def _scan_row(bi, st, ncc, nlc, nl, chunk):
    slot = _scan_slot(st, ncc, nlc)
    return jnp.where(slot < ncc, nl // chunk + bi * ncc + slot, bi * nlc + (slot - ncc))


def _ret_kernel(lg_ref, q_ref, k_ref, v_ref, g_ref, cos_ref, slo_ref, shi_ref, nw_ref,
                o_ref, st_sc, of_sc, *, ncc, nlc):
    c = R_CHUNK
    nc = ncc + nlc
    st = pl.program_id(1)
    bwd = st >= nc
    slot = _scan_slot(st, ncc, nlc)

    @pl.when(st % nc == 0)
    def _():
        st_sc[...] = jnp.zeros_like(st_sc)

    cos, slo, shi = cos_ref[...], slo_ref[...], shi_ref[...]
    q = _rope(q_ref[...], cos, slo, shi, 32)
    k = _rope(k_ref[...], cos, slo, shi, 32) * (R_QK_DIM ** -0.5)
    kt = k.T
    v = v_ref[...]
    ii = lax.broadcasted_iota(jnp.int32, (c, c), 0)
    jj = lax.broadcasted_iota(jnp.int32, (c, c), 1)
    dd = jnp.where(bwd, jj - ii, ii - jj).astype(F32)
    pcol = lax.broadcasted_iota(jnp.int32, (c, 1), 0)
    pcol = jnp.where(bwd, c - 1 - pcol, pcol).astype(F32)
    prow = lax.broadcasted_iota(jnp.int32, (1, c), 1)
    prow = jnp.where(bwd, c - 1 - prow, prow).astype(F32)
    outs = []
    for h in range(R_HEADS):
        lg = jnp.where(bwd, lg_ref[1, h], lg_ref[0, h])
        qh = q[:, h * 64:(h + 1) * 64]
        kh = k[:, h * 64:(h + 1) * 64]
        vh = v[:, h * 128:(h + 1) * 128]
        dm = jnp.where(dd >= 0, jnp.exp(lg * jnp.maximum(dd, 0.0)), 0.0)
        intra = _dot(_dot_nt(qh, kh) * dm, vh)
        state = st_sc[h]
        inter = _dot(qh * jnp.exp(lg * (pcol + 1.0)), state)
        kdt = kt[h * 64:(h + 1) * 64, :] * jnp.exp(lg * (c - 1.0 - prow))
        st_sc[h] = jnp.exp(lg * jnp.full((1, R_V_DIM), float(c), F32)) * state + _dot(kdt, vh)
        outs.append(intra + inter)
    o = jnp.concatenate(outs, axis=1)

    @pl.when(jnp.logical_not(bwd))
    def _():
        of_sc[slot] = o

    @pl.when(bwd)
    def _():
        tot = o + of_sc[slot]
        g = g_ref[...]
        ys = [_rms(tot[:, h * 128:(h + 1) * 128], R_V_DIM) for h in range(R_HEADS)]
        y = jnp.concatenate(ys, axis=1) * nw_ref[...]
        o_ref[...] = (y * _silu(g)).astype(o_ref.dtype)


def _retention(lg, p, tabs, nw, dims):
    b, s, lc, nl, t = dims
    c = R_CHUNK
    ncc, nlc = lc // c, s // c
    nc = ncc + nlc

    def row(bi, st):
        return _scan_row(bi, st, ncc, nlc, nl, c)

    def pspec(width, col_off):
        return pl.BlockSpec((c, width), lambda bi, st: (row(bi, st), col_off // width))

    def trow(bi, st):
        slot = _scan_slot(st, ncc, nlc)
        return jnp.where(slot < ncc, nlc, slot - ncc)

    def grow(bi, st):
        return row(bi, jnp.maximum(st, nc))

    tab_spec = pl.BlockSpec((c, 256), lambda bi, st: (trow(bi, st), 0))
    return pl.pallas_call(
        functools.partial(_ret_kernel, ncc=ncc, nlc=nlc),
        grid=(b, 2 * nc),
        in_specs=[pl.BlockSpec(memory_space=pltpu.SMEM),
                  pspec(256, OFF_R), pspec(256, OFF_R + 256), pspec(512, OFF_R + 512),
                  pl.BlockSpec((c, 512), lambda bi, st: (grow(bi, st), (OFF_R + 1024) // 512)),
                  tab_spec, tab_spec, tab_spec,
                  pl.BlockSpec((1, 512), lambda bi, st: (0, 0))],
        out_specs=pl.BlockSpec((c, 512), lambda bi, st: (grow(bi, st), 0)),
        out_shape=jax.ShapeDtypeStruct((t, 512), BF),
        scratch_shapes=[pltpu.VMEM((R_HEADS, R_QK_DIM, R_V_DIM), F32),
                        pltpu.VMEM((nc, c, 512), F32)],
        compiler_params=_cparams(("parallel", "arbitrary")),
        name="retention",
    )(lg, p, p, p, p, *tabs, nw)


def _conv_kernel(x_ref, xp_ref, xn_ref, w_ref, b_ref, o_ref, e_sc, *, tm, blocks_per_seg, n_lat_blocks):
    i = pl.program_id(0)
    is_lat = i < n_lat_blocks
    first = jnp.logical_or(jnp.logical_not(is_lat), i % blocks_per_seg == 0)
    last = jnp.logical_or(jnp.logical_not(is_lat), i % blocks_per_seg == blocks_per_seg - 1)
    e_sc[0:8, :] = jnp.where(first, 0.0, xp_ref[...])
    e_sc[8:8 + tm, :] = x_ref[...]
    e_sc[8 + tm:16 + tm, :] = jnp.where(last, 0.0, xn_ref[...])
    pad = (S_CONV - 1) // 2
    acc = b_ref[...] + w_ref[0:1, :] * e_sc[8 - pad:8 - pad + tm, :]
    for kk in range(1, S_CONV):
        acc = acc + w_ref[kk:kk + 1, :] * e_sc[8 - pad + kk:8 - pad + kk + tm, :]
    o_ref[...] = _silu(acc)


def _conv(p, w, bias, dims, tm):
    b, s, lc, nl, t = dims
    ch = w.shape[1]
    r8 = tm // 8
    nblk = t // tm
    xoff = (OFF_S + 512) // ch
    assert (OFF_S + 512) % ch == 0
    assert lc == tm
    return pl.pallas_call(
        functools.partial(_conv_kernel, tm=tm, blocks_per_seg=s // tm, n_lat_blocks=nl // tm),
        grid=(nblk,),
        in_specs=[pl.BlockSpec((tm, ch), lambda i: (i, xoff)),
                  pl.BlockSpec((8, ch), lambda i: (jnp.maximum(i * r8 - 1, 0), xoff)),
                  pl.BlockSpec((8, ch), lambda i: (jnp.minimum((i + 1) * r8, nblk * r8 - 1), xoff)),
                  pl.BlockSpec((S_CONV, ch), lambda i: (0, 0)),
                  pl.BlockSpec((1, ch), lambda i: (0, 0))],
        out_specs=pl.BlockSpec((tm, ch), lambda i: (i, 0)),
        out_shape=jax.ShapeDtypeStruct((t, ch), F32),
        scratch_shapes=[pltpu.VMEM((tm + 16, ch), F32)],
        compiler_params=_cparams(("parallel",)),
        name="s_conv",
    )(p, p, p, w, bias)


def _ssd_kernel(a_ref, dtb_ref, dsk_ref, nw_ref, u_ref, dt_ref, z_ref, o_ref, h_sc, yf_sc,
                *, ncc, nlc):
    qn = S_CHUNK
    nc = ncc + nlc
    st = pl.program_id(1)
    bwd = st >= nc
    slot = _scan_slot(st, ncc, nlc)

    @pl.when(st % nc == 0)
    def _():
        h_sc[...] = jnp.zeros_like(h_sc)

    u = u_ref[...]
    xs = u[:, :512]
    bm = u[:, 512:768]
    cm = u[:, 768:1024]
    raw = dt_ref[...]
    raw = jnp.where(bwd, pltpu.roll(raw, 128 - S_HEADS, 1), raw)
    a = jnp.where(bwd, a_ref[1:2, :], a_ref[0:1, :])
    dtb = jnp.where(bwd, dtb_ref[1:2, :], dtb_ref[0:1, :])
    xr = raw + dtb
    dt = jnp.maximum(xr, 0.0) + jnp.log1p(jnp.exp(-jnp.abs(xr)))
    dta = dt * a
    ii = lax.broadcasted_iota(jnp.int32, (qn, qn), 0)
    jj = lax.broadcasted_iota(jnp.int32, (qn, qn), 1)
    tri = jnp.where(jj <= ii, 1.0, 0.0).astype(BF)
    hi = dta.astype(BF)
    lo = (dta - hi.astype(F32)).astype(BF)
    lo2 = (dta - hi.astype(F32) - lo.astype(F32)).astype(BF)
    pfx = (jnp.dot(tri, hi, preferred_element_type=F32)
           + jnp.dot(tri, lo, preferred_element_type=F32)
           + jnp.dot(tri, lo2, preferred_element_type=F32))
    total = pfx[qn - 1:qn, :]
    cum = jnp.where(bwd, total - pfx + dta, pfx)
    cum_t = cum.T
    dt_t = dt.T
    mask = jnp.where(bwd, jj - ii, ii - jj) >= 0
    ys = []
    for g in range(S_GROUPS):
        bg = bm[:, g * 128:(g + 1) * 128]
        cg = cm[:, g * 128:(g + 1) * 128]
        cb = _dot_nt(cg, bg)
        bgt = bg.T
        hg = S_HEADS // S_GROUPS
        yoff = _dot(cg, h_sc[:, g * hg * 64:(g + 1) * hg * 64])
        for hh in range(hg):
            h = g * hg + hh
            ccol = cum[:, h:h + 1]
            crow = cum_t[h:h + 1, :]
            dtrow = dt_t[h:h + 1, :]
            dtcol = dt[:, h:h + 1]
            dec = jnp.exp(jnp.where(mask, ccol - crow, NEG_INF))
            w = cb * dec * dtrow
            xh = xs[:, h * 64:(h + 1) * 64]
            y = _dot(w, xh) + yoff[:, hh * 64:(hh + 1) * 64] * jnp.exp(ccol)
            tot = total[:, h:h + 1]
            xdec = xh * (jnp.exp(tot - ccol) * dtcol)
            h_sc[:, h * 64:(h + 1) * 64] = (jnp.exp(tot) * h_sc[:, h * 64:(h + 1) * 64]
                                            + _dot(bgt, xdec))
            ys.append(y)
    y = jnp.concatenate(ys, axis=1)

    @pl.when(jnp.logical_not(bwd))
    def _():
        yf_sc[slot] = y

    @pl.when(bwd)
    def _():
        z = z_ref[...]
        yt = (y + yf_sc[slot] + dsk_ref[...] * xs) * _silu(z)
        o_ref[...] = (_rms(yt, yt.shape[-1]) * nw_ref[...]).astype(o_ref.dtype)


def _ssd(a2, dtb2, dsk, nw, u, p, dims):
    b, s, lc, nl, t = dims
    c = S_CHUNK
    ncc, nlc = lc // c, s // c
    nc = ncc + nlc

    def row(bi, st):
        return _scan_row(bi, st, ncc, nlc, nl, c)

    def grow(bi, st):
        return row(bi, jnp.maximum(st, nc))

    const = lambda shape: pl.BlockSpec(shape, lambda bi, st: (0, 0))
    return pl.pallas_call(
        functools.partial(_ssd_kernel, ncc=ncc, nlc=nlc),
        grid=(b, 2 * nc),
        in_specs=[const((2, 128)), const((2, 128)), const((1, 512)), const((1, 512)),
                  pl.BlockSpec((c, 1024), lambda bi, st: (row(bi, st), 0)),
                  pl.BlockSpec((c, 128), lambda bi, st: (row(bi, st), OFF_DT // 128)),
                  pl.BlockSpec((c, 512), lambda bi, st: (grow(bi, st), OFF_S // 512))],
        out_specs=pl.BlockSpec((c, 512), lambda bi, st: (grow(bi, st), 0)),
        out_shape=jax.ShapeDtypeStruct((t, 512), BF),
        scratch_shapes=[pltpu.VMEM((S_STATE, S_HEADS * S_HEAD_DIM), F32),
                        pltpu.VMEM((nc, c, 512), F32)],
        compiler_params=_cparams(("parallel", "arbitrary")),
        name="ssd",
    )(a2, dtb2, dsk, nw, u, p, p)


def _mprep_kernel(p_ref, cos_ref, slo_ref, shi_ref, cqw_ref, ckvw_ref, wq_ref, wkv_ref,
                  qw_ref, kw_ref, q_ref, k_ref, v_ref):
    cos, slo, shi = cos_ref[...], slo_ref[...], shi_ref[...]
    scale = M_QK_DIM ** -0.5
    cq = _rms(p_ref[:, 0:512], M_Q_RANK) * cqw_ref[...]
    q = _dot(cq, wq_ref[...])
    ckv = _rms(p_ref[:, 512:640], M_KV_RANK) * ckvw_ref[...]
    kv = _dot(ckv, wkv_ref[...])
    kr = p_ref[:, 640:768]
    kr_ss = jnp.sum(kr * kr, axis=-1, keepdims=True)
    for h in range(M_HEADS):
        c0 = q[:, h * 256:h * 256 + 128]
        c1 = q[:, h * 256 + 128:h * 256 + 256]
        r = lax.rsqrt((jnp.sum(c0 * c0, axis=-1, keepdims=True)
                       + jnp.sum(c1 * c1, axis=-1, keepdims=True)) * (1.0 / M_QK_DIM) + EPS)
        q_ref[:, h * 256:h * 256 + 128] = (c0 * r * qw_ref[:, 0:128] * scale).astype(BF)
        q_ref[:, h * 256 + 128:h * 256 + 256] = (
            _rope(c1 * r * qw_ref[:, 128:256], cos, slo, shi, 16) * scale).astype(BF)
        k0 = kv[:, h * 128:(h + 1) * 128]
        r = lax.rsqrt((jnp.sum(k0 * k0, axis=-1, keepdims=True) + kr_ss) * (1.0 / M_QK_DIM) + EPS)
        k_ref[:, h * 256:h * 256 + 128] = (k0 * r * kw_ref[:, 0:128]).astype(BF)
        k_ref[:, h * 256 + 128:h * 256 + 256] = (
            _rope(kr * r * kw_ref[:, 128:256], cos, slo, shi, 16)).astype(BF)
    v_ref[...] = kv[:, 512:1024].astype(BF)


def _mprep(p, tabs, cqw, ckvw, wq, wkv, qw, kw, dims, tm):
    b, s, lc, nl, t = dims
    tab_spec = pl.BlockSpec((tm, 128), lambda i: (_table_index(i, tm, nl, s), 0))
    const = lambda shape: pl.BlockSpec(shape, lambda i: (0, 0))
    return pl.pallas_call(
        _mprep_kernel,
        grid=(t // tm,),
        in_specs=[pl.BlockSpec((tm, W_M), lambda i: (i, OFF_M // W_M)),
                  tab_spec, tab_spec, tab_spec,
                  const((1, 512)), const((1, 128)), const((512, 1024)), const((128, 1024)),
                  const((1, 256)), const((1, 256))],
        out_specs=[pl.BlockSpec((tm, 1024), lambda i: (i, 0)),
                   pl.BlockSpec((tm, 1024), lambda i: (i, 0)),
                   pl.BlockSpec((tm, 512), lambda i: (i, 0))],
        out_shape=[jax.ShapeDtypeStruct((t, 1024), BF),
                   jax.ShapeDtypeStruct((t, 1024), BF),
                   jax.ShapeDtypeStruct((t, 512), BF)],
        compiler_params=_cparams(("parallel",)),
        name="m_prep",
    )(p, *tabs, cqw, ckvw, wq, wkv, qw, kw)


def _mla_kernel(q_ref, kx_ref, vx_ref, kl_ref, vl_ref, o_ref, *, nq, tk):
    qi = pl.program_id(1)
    s_len = kl_ref.shape[0]
    n_tiles = jnp.where(qi < nq, s_len // tk, 0)
    for h in range(M_HEADS):
        qh = q_ref[:, h * 256:(h + 1) * 256]
        ksl = slice(h * 256, (h + 1) * 256)
        vsl = slice(h * 128, (h + 1) * 128)
        s = _dot_nt(qh, kx_ref[:, ksl])
        m0 = jnp.max(s, axis=-1, keepdims=True)
        p = jnp.exp(s - m0)
        l0 = jnp.sum(p, axis=-1, keepdims=True)
        acc0 = _dot(p, vx_ref[:, vsl])

        def body(kt, carry):
            m, l, acc = carry
            r0 = pl.multiple_of(kt * tk, tk)
            s = _dot_nt(qh, kl_ref[pl.ds(r0, tk), ksl])
            m_new = jnp.maximum(m, jnp.max(s, axis=-1, keepdims=True))
            alpha = jnp.exp(m - m_new)
            p = jnp.exp(s - m_new)
            l = alpha * l + jnp.sum(p, axis=-1, keepdims=True)
            acc = alpha * acc + _dot(p, vl_ref[pl.ds(r0, tk), vsl])
            return m_new, l, acc

        m, l, acc = lax.fori_loop(0, n_tiles, body, (m0, l0, acc0))
        o_ref[:, vsl] = (acc * (1.0 / l)).astype(o_ref.dtype)


def _mla(qm, km, vm, dims, need_ctx, tq, tk):
    b, s, lc, nl, t = dims
    nq = s // tq
    assert lc == tq
    steps = nq + (1 if need_ctx else 0)
    rows = nl + (b * lc if need_ctx else 0)

    def qrow(bi, qi):
        return jnp.where(qi < nq, bi * nq + qi, nl // tq + bi)

    return pl.pallas_call(
        functools.partial(_mla_kernel, nq=nq, tk=tk),
        grid=(b, steps),
        in_specs=[pl.BlockSpec((tq, 1024), lambda bi, qi: (qrow(bi, qi), 0)),
                  pl.BlockSpec((lc, 1024), lambda bi, qi: (nl // lc + bi, 0)),
                  pl.BlockSpec((lc, 512), lambda bi, qi: (nl // lc + bi, 0)),
                  pl.BlockSpec((s, 1024), lambda bi, qi: (bi, 0)),
                  pl.BlockSpec((s, 512), lambda bi, qi: (bi, 0))],
        out_specs=pl.BlockSpec((tq, 512), lambda bi, qi: (qrow(bi, qi), 0)),
        out_shape=jax.ShapeDtypeStruct((rows, 512), BF),
        compiler_params=_cparams(("parallel", "arbitrary")),
        name="mla",
    )(qm, km, vm, km, vm)


def _merge_kernel(x_ref, mod_ref, ya_ref, yb_ref, yc_ref, yd_ref, g0_ref, g1_ref, g2_ref, g3_ref,
                  wb_ref, wo_ref, o_ref, acc_sc, *, tn):
    j = pl.program_id(1)
    acc = None
    for kk, (y_ref, g_ref) in enumerate(((ya_ref, g0_ref), (yb_ref, g1_ref),
                                         (yc_ref, g2_ref), (yd_ref, g3_ref))):
        gate = 1.0 / (1.0 + jnp.exp(-g_ref[...].astype(F32)))
        term = gate * jnp.dot(y_ref[...], wb_ref[kk], preferred_element_type=F32)
        acc = term if acc is None else acc + term
    acc_sc[:, pl.ds(pl.multiple_of(j * tn, tn), tn)] = acc.astype(BF)

    @pl.when(j == pl.num_programs(1) - 1)
    def _():
        o_ref[...] = x_ref[...] + mod_ref[2:3, :] * jnp.dot(
            acc_sc[...], wo_ref[...], preferred_element_type=F32)


def _merge(xs, mod, ys, p, wb, wo, dims, rows, tm, tn):
    b, s, lc, nl, t = dims
    d = xs.shape[1]
    nj = d // tn
    goff = OFF_G // tn
    yspec = pl.BlockSpec((tm, BRANCH_W), lambda i, j: (i, 0))
    gspecs = [pl.BlockSpec((tm, tn), functools.partial(lambda i, j, kk: (i, goff + kk * nj + j), kk=kk))
              for kk in range(N_BRANCH)]
    return pl.pallas_call(
        functools.partial(_merge_kernel, tn=tn),
        grid=(rows // tm, nj),
        in_specs=[pl.BlockSpec((tm, d), lambda i, j: (i, 0)),
                  pl.BlockSpec((None, N_MOD, d), lambda i, j: (_mod_index(i, tm, nl, s, b), 0, 0)),
                  yspec, yspec, yspec, yspec, *gspecs,
                  pl.BlockSpec((N_BRANCH, BRANCH_W, tn), lambda i, j: (0, 0, j)),
                  pl.BlockSpec((d, d), lambda i, j: (0, 0))],
        out_specs=pl.BlockSpec((tm, d), lambda i, j: (i, 0)),
        out_shape=jax.ShapeDtypeStruct((rows, d), F32),
        scratch_shapes=[pltpu.VMEM((tm, d), BF)],
        compiler_params=_cparams(("parallel", "arbitrary")),
        name="merge",
    )(xs, mod, *ys, p, p, p, p, wb, wo)


def _ffn_kernel(x_ref, mod_ref, nw_ref, w1_ref, w2_ref, o_ref, h_sc):
    f = pl.program_id(1)

    @pl.when(f == 0)
    def _():
        x = x_ref[...]
        y = _rms(x, x.shape[-1]) * nw_ref[...]
        h_sc[...] = (y * (1.0 + mod_ref[4:5, :]) + mod_ref[3:4, :]).astype(BF)

    u = jnp.maximum(jnp.dot(h_sc[...], w1_ref[...], preferred_element_type=F32), 0.0)
    part = jnp.dot((u * u).astype(BF), w2_ref[...], preferred_element_type=F32)

    @pl.when(f == 0)
    def _():
        o_ref[...] = part

    @pl.when(f > 0)
    def _():
        o_ref[...] += part

    @pl.when(f == pl.num_programs(1) - 1)
    def _():
        o_ref[...] = x_ref[...] + mod_ref[5:6, :] * o_ref[...]


def _ffn(xs, mod, nw, w1, w2, dims, rows, tm, tf):
    b, s, lc, nl, t = dims
    d = xs.shape[1]
    dff = w1.shape[1]
    return pl.pallas_call(
        _ffn_kernel,
        grid=(rows // tm, dff // tf),
        in_specs=[pl.BlockSpec((tm, d), lambda i, f: (i, 0)),
                  pl.BlockSpec((None, N_MOD, d), lambda i, f: (_mod_index(i, tm, nl, s, b), 0, 0)),
                  pl.BlockSpec((1, d), lambda i, f: (0, 0)),
                  pl.BlockSpec((d, tf), lambda i, f: (0, f)),
                  pl.BlockSpec((tf, d), lambda i, f: (f, 0))],
        out_specs=pl.BlockSpec((tm, d), lambda i, f: (i, 0)),
        out_shape=jax.ShapeDtypeStruct((rows, d), F32),
        scratch_shapes=[pltpu.VMEM((tm, d), BF)],
        compiler_params=_cparams(("parallel", "arbitrary")),
        name="ffn",
    )(xs, mod, nw, w1, w2)


def _rope_tables(pos_per_lane, freq_per_lane, lo_lane, pad_rows, live_lane=None):
    ang = pos_per_lane * freq_per_lane[None, :]
    cos, sin = jnp.cos(ang), jnp.sin(ang)
    if live_lane is not None:
        cos = jnp.where(live_lane[None, :], cos, 1.0)
        sin = jnp.where(live_lane[None, :], sin, 0.0)
    s_lo = jnp.where(lo_lane[None, :], -sin, 0.0)
    s_hi = jnp.where(lo_lane[None, :], 0.0, sin)
    w = cos.shape[1]
    ident = (jnp.ones((pad_rows, w), F32), jnp.zeros((pad_rows, w), F32), jnp.zeros((pad_rows, w), F32))
    return tuple(jnp.concatenate([tb.astype(F32), idn], axis=0)
                 for tb, idn in zip((cos, s_lo, s_hi), ident))


def _tables(s, pad_rows):
    t = jnp.arange(s)
    row = (t // GRID_W).astype(F32)[:, None]
    col = (t % GRID_W).astype(F32)[:, None]
    tpos = t.astype(F32)[:, None]
    lane = jnp.arange(128)
    fa = ROPE_BASE ** (-(lane % 32).astype(F32) / 32)
    pos_a = jnp.where((lane < 64)[None, :], row, col)
    tab_a = _rope_tables(pos_a, fa, (lane % 64) < 32, pad_rows)
    lane = jnp.arange(256)
    fb = ROPE_BASE ** (-(lane % 32).astype(F32) / 32)
    tab_r = _rope_tables(jnp.broadcast_to(tpos, (s, 256)), fb, (lane % 64) < 32, pad_rows)
    lane = jnp.arange(128)
    fm = ROPE_BASE ** (-(lane % 16).astype(F32) / 16)
    pos_m = jnp.where((lane < 32)[None, :], row, col)
    tab_m = _rope_tables(pos_m, fm, (lane % 32) < 16, pad_rows, live_lane=lane < 64)
    return tab_a, tab_r, tab_m


def _pad_cols(w, n):
    return jnp.pad(w, [(0, 0)] * (w.ndim - 1) + [(0, n - w.shape[-1])])


def kernel(x, c, ctx, c_ctx, w_ada, b_ada, norm1_w, norm2_w, w_in, a_q_norm, a_k_norm, a_sink,
           r_decay, r_norm, s_conv_w, s_conv_b, s_a_log, s_dt_bias, s_d, s_norm,
           m_cq_norm, m_ckv_norm, m_w_uq, m_w_ukv, m_q_norm, m_k_norm,
           w_branch, w_o, w_ff1, w_ff2):
    b, s, d = x.shape
    lc = ctx.shape[1]
    depth = w_ada.shape[0]
    nl, ncx = b * s, b * lc
    t = nl + ncx
    dims = (b, s, lc, nl, t)
    tm_big = max(m for m in (1024, 512, 256) if s % m == 0 and ncx % m == 0)
    tm = max(m for m in (512, 256) if s % m == 0 and ncx % m == 0)
    tp = 256
    assert lc == 256 and s % 256 == 0 and b + 1 <= 8

    a_w = w_in[..., 0:1024]
    r_w = w_in[..., 1024:2560]
    s_w = w_in[..., 2560:4096]
    dt_w = w_in[..., 4096:4112]
    m_w = w_in[..., 4112:4816]
    g_w = w_in[..., 4816:]
    w_in_p = jnp.concatenate([_pad_cols(m_w, W_M), _pad_cols(dt_w, W_DT), a_w, r_w, s_w, g_w],
                             axis=-1).astype(BF)
    wq = _pad_cols(m_w_uq.reshape(depth, M_Q_RANK, M_HEADS, M_QK_DIM), 256)
    wq = wq.reshape(depth, M_Q_RANK, M_HEADS * 256).astype(BF)
    wkv = m_w_ukv.reshape(depth, M_KV_RANK, M_HEADS, M_NOPE + M_V)
    wkv = jnp.concatenate([wkv[..., :M_NOPE].reshape(depth, M_KV_RANK, M_HEADS * M_NOPE),
                           wkv[..., M_NOPE:].reshape(depth, M_KV_RANK, M_HEADS * M_V)],
                          axis=-1).astype(BF)
    wb = w_branch.astype(BF)
    wo = w_o.astype(BF)
    w1 = w_ff1.astype(BF)
    w2 = w_ff2.astype(BF)
    log_gamma = jnp.log1p(-jnp.exp(r_decay.astype(F32)))
    a_neg = _pad_cols(-jnp.exp(s_a_log.astype(F32)), 128)
    dtb = _pad_cols(s_dt_bias.astype(F32), 128)
    dsk = jnp.repeat(s_d.astype(F32), S_HEAD_DIM, axis=-1)[:, None, :]
    mqw = _pad_cols(m_q_norm, 256)[:, None, :]
    mkw = _pad_cols(m_k_norm, 256)[:, None, :]
    tab_a, tab_r, tab_m = _tables(s, tp)

    c8 = jnp.zeros((8, d), F32).at[:b].set(c).at[b].set(c_ctx)
    mod_all = _ada(c8, w_ada, b_ada).reshape(depth, 8, N_MOD, d)

    xs = jnp.concatenate([x.reshape(nl, d), ctx.reshape(ncx, d)], axis=0)
    for i in range(depth):
        need_ctx = i < depth - 1
        rows = t if need_ctx else nl
        mod = mod_all[i]
        p = _inproj(xs, mod, norm1_w[i][None, :], w_in_p[i], dims, tm_big, 1024)
        qa, ka, va = _aprep(p, tab_a, a_q_norm[i][None, :], a_k_norm[i][None, :], dims, tp)
        ya = _aattn(a_sink[i].astype(F32), qa, ka, va, dims, need_ctx)
        yb = _retention(log_gamma[i], p, tab_r, r_norm[i].reshape(1, -1), dims)
        u = _conv(p, s_conv_w[i], s_conv_b[i][None, :], dims, tp)
        yc = _ssd(a_neg[i], dtb[i], dsk[i], s_norm[i][None, :], u, p, dims)
        qm, km, vm = _mprep(p, tab_m, m_cq_norm[i][None, :],
                            m_ckv_norm[i][None, :], wq[i], wkv[i], mqw[i], mkw[i], dims, tp)
        yd = _mla(qm, km, vm, dims, need_ctx, 256, 256)
        xs = _merge(xs, mod, (ya, yb, yc, yd), p, wb[i], wo[i], dims, rows, tm, 512)
        xs = _ffn(xs, mod, norm2_w[i][None, :], w1[i], w2[i], dims, rows, tm, 1024)
    return xs.reshape(b, s, d)
```

```python
import functools
import math

import jax
import jax.numpy as jnp
import numpy as np
from jax import lax
from jax.experimental import pallas as pl
from jax.experimental.pallas import tpu as pltpu

GRID_W = 64
EPS = 1e-6
ROPE_BASE = 10000.0
NEG_INF = -1e30

A_HEADS, A_KV_HEADS, A_HEAD_DIM, A_BLOCK = 4, 2, 128, 128
R_HEADS, R_QK_DIM, R_V_DIM, R_CHUNK = 4, 64, 128, 128
S_HEADS, S_HEAD_DIM, S_GROUPS, S_STATE, S_CONV, S_CHUNK = 8, 64, 2, 128, 5, 128
M_HEADS, M_Q_RANK, M_KV_RANK, M_NOPE, M_ROPE, M_V = 4, 512, 128, 128, 64, 128
M_QK_DIM = M_NOPE + M_ROPE
N_BRANCH, N_MOD = 4, 6
BRANCH_W = 512

OFF_M, W_M = 0, 768
OFF_DT, W_DT = 768, 256
OFF_A, W_A = 1024, 1024
OFF_R, W_R = 2048, 1536
OFF_S, W_S = 3584, 1536
OFF_G = 5120

VMEM_LIMIT = 56 * 1024 * 1024

BF = jnp.bfloat16
F32 = jnp.float32


def _cparams(sem):
    return pltpu.CompilerParams(dimension_semantics=sem, vmem_limit_bytes=VMEM_LIMIT)


def _dot(a, b):
    return jnp.dot(a.astype(BF), b.astype(BF), preferred_element_type=F32)


def _dot_nt(a, b):
    return lax.dot_general(a.astype(BF), b.astype(BF), (((1,), (1,)), ((), ())),
                           preferred_element_type=F32)


def _silu(x):
    return x * (1.0 / (1.0 + jnp.exp(-x)))


def _rms(x, n):
    return x * lax.rsqrt(jnp.sum(x * x, axis=-1, keepdims=True) * (1.0 / n) + EPS)


def _rope(y, cos, s_lo, s_hi, shift):
    w = y.shape[-1]
    return y * cos + pltpu.roll(y, w - shift, 1) * s_lo + pltpu.roll(y, shift, 1) * s_hi


def _ada_kernel(c_ref, w_ref, b_ref, o_ref):
    o_ref[...] = _dot(_silu(c_ref[...]), w_ref[...]) + b_ref[...]


def _ada(c8, w_ada, b_ada):
    depth, d, n = w_ada.shape
    tn = 512
    return pl.pallas_call(
        _ada_kernel,
        grid=(depth, n // tn),
        in_specs=[pl.BlockSpec((8, d), lambda l, j: (0, 0)),
                  pl.BlockSpec((None, d, tn), lambda l, j: (l, 0, j)),
                  pl.BlockSpec((None, 1, tn), lambda l, j: (l, 0, j))],
        out_specs=pl.BlockSpec((None, 8, tn), lambda l, j: (l, 0, j)),
        out_shape=jax.ShapeDtypeStruct((depth, 8, n), F32),
        compiler_params=_cparams(("parallel", "parallel")),
        name="ada",
    )(c8, w_ada, b_ada.reshape(depth, 1, n))


def _inproj_kernel(x_ref, mod_ref, nw_ref, w_ref, o_ref, dt_ref, h_sc):
    j = pl.program_id(1)

    @pl.when(j == 0)
    def _():
        x = x_ref[...]
        y = _rms(x, x.shape[-1]) * nw_ref[...]
        h_sc[...] = (y * (1.0 + mod_ref[1:2, :]) + mod_ref[0:1, :]).astype(BF)

    r = jnp.dot(h_sc[...], w_ref[...], preferred_element_type=F32)
    o_ref[...] = r.astype(o_ref.dtype)

    @pl.when(j == 0)
    def _():
        dt_ref[...] = r[:, OFF_DT:OFF_DT + W_DT]


def _mod_index(i, tm, nl, s, b):
    return jnp.where(i < nl // tm, i // (s // tm), b)


def _inproj(xs, mod, nw, w, layer, dims, tm, tn):
    b, s, lc, nl, t = dims
    d = xs.shape[1]
    n = w.shape[2]
    assert OFF_DT + W_DT <= tn
    return pl.pallas_call(
        _inproj_kernel,
        grid=(t // tm, n // tn),
        in_specs=[pl.BlockSpec((tm, d), lambda i, j: (i, 0)),
                  pl.BlockSpec((None, N_MOD, d), lambda i, j: (_mod_index(i, tm, nl, s, b), 0, 0)),
                  pl.BlockSpec((1, d), lambda i, j: (0, 0)),
                  pl.BlockSpec((None, d, tn), lambda i, j: (layer, 0, j))],
        out_specs=[pl.BlockSpec((tm, tn), lambda i, j: (i, j)),
                   pl.BlockSpec((tm, W_DT), lambda i, j: (i, 0))],
        out_shape=[jax.ShapeDtypeStruct((t, n), BF),
                   jax.ShapeDtypeStruct((t, W_DT), F32)],
        scratch_shapes=[pltpu.VMEM((tm, d), BF)],
        compiler_params=_cparams(("parallel", "arbitrary")),
        name="inproj",
    )(xs, mod, nw, w)


def _aprep_kernel(p_ref, cos_ref, slo_ref, shi_ref, qw_ref, kw_ref, q_ref, k_ref, v_ref):
    cos, slo, shi = cos_ref[...], slo_ref[...], shi_ref[...]
    scale = A_HEAD_DIM ** -0.5
    for h in range(A_HEADS):
        y = _rms(p_ref[:, h * 128:(h + 1) * 128].astype(F32), A_HEAD_DIM) * qw_ref[...]
        q_ref[:, h * 128:(h + 1) * 128] = (_rope(y, cos, slo, shi, 32) * scale).astype(BF)
    for h in range(A_KV_HEADS):
        y = _rms(p_ref[:, 512 + h * 128:512 + (h + 1) * 128].astype(F32), A_HEAD_DIM) * kw_ref[...]
        k_ref[:, h * 128:(h + 1) * 128] = _rope(y, cos, slo, shi, 32).astype(BF)
    v_ref[...] = p_ref[:, 768:1024].astype(BF)


def _table_index(i, tm, nl, s):
    return jnp.where(i < nl // tm, i % (s // tm), s // tm)


def _aprep(p, tabs, qw, kw, dims, tm):
    b, s, lc, nl, t = dims
    tab_spec = pl.BlockSpec((tm, 128), lambda i: (_table_index(i, tm, nl, s), 0))
    return pl.pallas_call(
        _aprep_kernel,
        grid=(t // tm,),
        in_specs=[pl.BlockSpec((tm, W_A), lambda i: (i, OFF_A // W_A)),
                  tab_spec, tab_spec, tab_spec,
                  pl.BlockSpec((1, 128), lambda i: (0, 0)),
                  pl.BlockSpec((1, 128), lambda i: (0, 0))],
        out_specs=[pl.BlockSpec((tm, 512), lambda i: (i, 0)),
                   pl.BlockSpec((tm, 256), lambda i: (i, 0)),
                   pl.BlockSpec((tm, 256), lambda i: (i, 0))],
        out_shape=[jax.ShapeDtypeStruct((t, 512), BF),
                   jax.ShapeDtypeStruct((t, 256), BF),
                   jax.ShapeDtypeStruct((t, 256), BF)],
        compiler_params=_cparams(("parallel",)),
        name="a_prep",
    )(p, *tabs, qw, kw)


def _aattn_kernel(sink_ref, q_ref, kp_ref, kc_ref, kn_ref, vp_ref, vc_ref, vn_ref,
                  kx_ref, vx_ref, o_ref, *, nb):
    n = pl.program_id(1)
    blk = A_BLOCK
    lc = kx_ref.shape[0]
    g = A_HEADS // A_KV_HEADS
    nk = 3 * blk + lc
    qi = lax.broadcasted_iota(jnp.int32, (g * blk, nk), 0) % blk
    kj = lax.broadcasted_iota(jnp.int32, (g * blk, nk), 1)
    lo_seq = jnp.where(n == 0, blk, 0)
    hi_seq = jnp.where(n < nb, jnp.where(n == nb - 1, 2 * blk - 1, 3 * blk - 1), -1)
    mask = ((kj >= jnp.maximum(qi, lo_seq)) & (kj <= jnp.minimum(qi + 2 * blk, hi_seq))) | (kj >= 3 * blk)
    row = lax.broadcasted_iota(jnp.int32, (g * blk, 1), 0)
    for hk in range(A_KV_HEADS):
        sl = slice(hk * 128, (hk + 1) * 128)
        q2 = jnp.concatenate([q_ref[:, (hk * g + gi) * 128:(hk * g + gi + 1) * 128]
                              for gi in range(g)], axis=0)
        kk = jnp.concatenate([kp_ref[:, sl], kc_ref[:, sl], kn_ref[:, sl], kx_ref[:, sl]], axis=0)
        vv = jnp.concatenate([vp_ref[:, sl], vc_ref[:, sl], vn_ref[:, sl], vx_ref[:, sl]], axis=0)
        s = _dot_nt(q2, kk)
        s = jnp.where(mask, s, NEG_INF)
        sink = jnp.where(row < blk, sink_ref[hk * g], sink_ref[hk * g + 1])
        m = jnp.maximum(jnp.max(s, axis=-1, keepdims=True), sink)
        p = jnp.exp(s - m)
        inv = 1.0 / (jnp.sum(p, axis=-1, keepdims=True) + jnp.exp(sink - m))
        o = _dot(p, vv) * inv
        for gi in range(g):
            o_ref[:, (hk * g + gi) * 128:(hk * g + gi + 1) * 128] = (
                o[gi * blk:(gi + 1) * blk]).astype(o_ref.dtype)


def _aattn(sink, qa, ka, va, dims, need_ctx):
    b, s, lc, nl, t = dims
    blk = A_BLOCK
    nb = s // blk
    ncb = lc // blk
    steps = nb + (ncb if need_ctx else 0)
    rows = nl + (b * lc if need_ctx else 0)

    def qrow(bi, n):
        return jnp.where(n < nb, bi * nb + n, nl // blk + bi * ncb + (n - nb))

    def krow(off):
        def f(bi, n):
            return (bi * nb + jnp.clip(n + off, 0, nb - 1), 0)
        return f

    kspec = [pl.BlockSpec((blk, 256), krow(o)) for o in (-1, 0, 1)]
    xspec = pl.BlockSpec((lc, 256), lambda bi, n: (nl // lc + bi, 0))
    return pl.pallas_call(
        functools.partial(_aattn_kernel, nb=nb),
        grid=(b, steps),
        in_specs=[pl.BlockSpec(memory_space=pltpu.SMEM),
                  pl.BlockSpec((blk, 512), lambda bi, n: (qrow(bi, n), 0)),
                  *kspec, *kspec, xspec, xspec],
        out_specs=pl.BlockSpec((blk, 512), lambda bi, n: (qrow(bi, n), 0)),
        out_shape=jax.ShapeDtypeStruct((rows, 512), BF),
        compiler_params=_cparams(("parallel", "arbitrary")),
        name="a_attn",
    )(sink, qa, ka, ka, ka, va, va, va, ka, va)


def _scan_slot(st, ncc, nlc):
    nc = ncc + nlc
    t = st % nc
    bwd = st >= nc
    slot_b = jnp.where(t < ncc, ncc - 1 - t, ncc + (nlc - 1 - (t - ncc)))
    return jnp.where(bwd, slot_b, t)


def _scan_row(bi, st, ncc, nlc, nl, chunk):
    slot = _scan_slot(st, ncc, nlc)
    return jnp.where(slot < ncc, nl // chunk + bi * ncc + slot, bi * nlc + (slot - ncc))


def _ret_kernel(lg_ref, q_ref, k_ref, v_ref, g_ref, cos_ref, slo_ref, shi_ref, nw_ref,
                o_ref, st_sc, of_sc, *, ncc, nlc):
    c = R_CHUNK
    nc = ncc + nlc
    st = pl.program_id(1)
    bwd = st >= nc
    slot = _scan_slot(st, ncc, nlc)

    @pl.when(st % nc == 0)
    def _():
        st_sc[...] = jnp.zeros_like(st_sc)

    cos, slo, shi = cos_ref[...], slo_ref[...], shi_ref[...]
    q = _rope(q_ref[...].astype(F32), cos, slo, shi, 32)
    k = _rope(k_ref[...].astype(F32), cos, slo, shi, 32) * (R_QK_DIM ** -0.5)
    kt = k.T
    v = v_ref[...]
    ii = lax.broadcasted_iota(jnp.int32, (c, c), 0)
    jj = lax.broadcasted_iota(jnp.int32, (c, c), 1)
    dd = jnp.where(bwd, jj - ii, ii - jj).astype(F32)
    pcol = lax.broadcasted_iota(jnp.int32, (c, 1), 0)
    pcol = jnp.where(bwd, c - 1 - pcol, pcol).astype(F32)
    prow = lax.broadcasted_iota(jnp.int32, (1, c), 1)
    prow = jnp.where(bwd, c - 1 - prow, prow).astype(F32)
    outs = []
    for h in range(R_HEADS):
        lg = jnp.where(bwd, lg_ref[1, h], lg_ref[0, h])
        qh = q[:, h * 64:(h + 1) * 64]
        kh = k[:, h * 64:(h + 1) * 64]
        vh = v[:, h * 128:(h + 1) * 128]
        dm = jnp.where(dd >= 0, jnp.exp(lg * jnp.maximum(dd, 0.0)), 0.0)
        intra = _dot(_dot_nt(qh, kh) * dm, vh)
        state = st_sc[h]
        inter = _dot(qh * jnp.exp(lg * (pcol + 1.0)), state)
        kdt = kt[h * 64:(h + 1) * 64, :] * jnp.exp(lg * (c - 1.0 - prow))
        st_sc[h] = jnp.exp(lg * jnp.full((1, R_V_DIM), float(c), F32)) * state + _dot(kdt, vh)
        outs.append(intra + inter)
    o = jnp.concatenate(outs, axis=1)

    @pl.when(jnp.logical_not(bwd))
    def _():
        of_sc[slot] = o

    @pl.when(bwd)
    def _():
        tot = o + of_sc[slot]
        g = g_ref[...].astype(F32)
        ys = [_rms(tot[:, h * 128:(h + 1) * 128], R_V_DIM) for h in range(R_HEADS)]
        y = jnp.concatenate(ys, axis=1) * nw_ref[...]
        o_ref[...] = (y * _silu(g)).astype(o_ref.dtype)


def _retention(lg, p, tabs, nw, dims):
    b, s, lc, nl, t = dims
    c = R_CHUNK
    ncc, nlc = lc // c, s // c
    nc = ncc + nlc

    def row(bi, st):
        return _scan_row(bi, st, ncc, nlc, nl, c)

    def pspec(width, col_off):
        return pl.BlockSpec((c, width), lambda bi, st: (row(bi, st), col_off // width))

    def trow(bi, st):
        slot = _scan_slot(st, ncc, nlc)
        return jnp.where(slot < ncc, nlc, slot - ncc)

    def grow(bi, st):
        return row(bi, jnp.maximum(st, nc))

    tab_spec = pl.BlockSpec((c, 256), lambda bi, st: (trow(bi, st), 0))
    return pl.pallas_call(
        functools.partial(_ret_kernel, ncc=ncc, nlc=nlc),
        grid=(b, 2 * nc),
        in_specs=[pl.BlockSpec(memory_space=pltpu.SMEM),
                  pspec(256, OFF_R), pspec(256, OFF_R + 256), pspec(512, OFF_R + 512),
                  pl.BlockSpec((c, 512), lambda bi, st: (grow(bi, st), (OFF_R + 1024) // 512)),
                  tab_spec, tab_spec, tab_spec,
                  pl.BlockSpec((1, 512), lambda bi, st: (0, 0))],
        out_specs=pl.BlockSpec((c, 512), lambda bi, st: (grow(bi, st), 0)),
        out_shape=jax.ShapeDtypeStruct((t, 512), BF),
        scratch_shapes=[pltpu.VMEM((R_HEADS, R_QK_DIM, R_V_DIM), F32),
                        pltpu.VMEM((nc, c, 512), F32)],
        compiler_params=_cparams(("parallel", "arbitrary")),
        name="retention",
    )(lg, p, p, p, p, *tabs, nw)


def _conv_kernel(x_ref, xp_ref, xn_ref, w_ref, b_ref, o_ref, e_sc, *, tm, blocks_per_seg, n_lat_blocks):
    i = pl.program_id(0)
    is_lat = i < n_lat_blocks
    first = jnp.logical_or(jnp.logical_not(is_lat), i % blocks_per_seg == 0)
    last = jnp.logical_or(jnp.logical_not(is_lat), i % blocks_per_seg == blocks_per_seg - 1)
    hr = HALO_ROWS
    e_sc[0:hr, :] = jnp.where(first, 0.0, xp_ref[...].astype(F32))
    e_sc[hr:hr + tm, :] = x_ref[...].astype(F32)
    e_sc[hr + tm:2 * hr + tm, :] = jnp.where(last, 0.0, xn_ref[...].astype(F32))
    pad = (S_CONV - 1) // 2
    acc = b_ref[...] + w_ref[0:1, :] * e_sc[hr - pad:hr - pad + tm, :]
    for kk in range(1, S_CONV):
        acc = acc + w_ref[kk:kk + 1, :] * e_sc[hr - pad + kk:hr - pad + kk + tm, :]
    o_ref[...] = _silu(acc)


HALO_ROWS = 16


def _conv(p, w, bias, dims, tm):
    b, s, lc, nl, t = dims
    ch = w.shape[1]
    r8 = tm // HALO_ROWS
    nblk = t // tm
    xoff = (OFF_S + 512) // ch
    assert (OFF_S + 512) % ch == 0
    assert lc == tm
    return pl.pallas_call(
        functools.partial(_conv_kernel, tm=tm, blocks_per_seg=s // tm, n_lat_blocks=nl // tm),
        grid=(nblk,),
        in_specs=[pl.BlockSpec((tm, ch), lambda i: (i, xoff)),
                  pl.BlockSpec((HALO_ROWS, ch), lambda i: (jnp.maximum(i * r8 - 1, 0), xoff)),
                  pl.BlockSpec((HALO_ROWS, ch),
                               lambda i: (jnp.minimum((i + 1) * r8, nblk * r8 - 1), xoff)),
                  pl.BlockSpec((S_CONV, ch), lambda i: (0, 0)),
                  pl.BlockSpec((1, ch), lambda i: (0, 0))],
        out_specs=pl.BlockSpec((tm, ch), lambda i: (i, 0)),
        out_shape=jax.ShapeDtypeStruct((t, ch), F32),
        scratch_shapes=[pltpu.VMEM((tm + 2 * HALO_ROWS, ch), F32)],
        compiler_params=_cparams(("parallel",)),
        name="s_conv",
    )(p, p, p, w, bias)


def _ssd_kernel(a_ref, dtb_ref, dsk_ref, nw_ref, u_ref, dt_ref, z_ref, o_ref, h_sc, yf_sc,
                *, ncc, nlc):
    qn = S_CHUNK
    nc = ncc + nlc
    st = pl.program_id(1)
    bwd = st >= nc
    slot = _scan_slot(st, ncc, nlc)

    @pl.when(st % nc == 0)
    def _():
        h_sc[...] = jnp.zeros_like(h_sc)

    u = u_ref[...]
    xs = u[:, :512]
    bm = u[:, 512:768]
    cm = u[:, 768:1024]
    raw = dt_ref[...]
    raw = jnp.where(bwd, pltpu.roll(raw, 128 - S_HEADS, 1), raw)
    a = jnp.where(bwd, a_ref[1:2, :], a_ref[0:1, :])
    dtb = jnp.where(bwd, dtb_ref[1:2, :], dtb_ref[0:1, :])
    xr = raw + dtb
    dt = jnp.maximum(xr, 0.0) + jnp.log1p(jnp.exp(-jnp.abs(xr)))
    dta = dt * a
    ii = lax.broadcasted_iota(jnp.int32, (qn, qn), 0)
    jj = lax.broadcasted_iota(jnp.int32, (qn, qn), 1)
    tri = jnp.where(jj <= ii, 1.0, 0.0).astype(BF)
    hi = dta.astype(BF)
    lo = (dta - hi.astype(F32)).astype(BF)
    lo2 = (dta - hi.astype(F32) - lo.astype(F32)).astype(BF)
    pfx = (jnp.dot(tri, hi, preferred_element_type=F32)
           + jnp.dot(tri, lo, preferred_element_type=F32)
           + jnp.dot(tri, lo2, preferred_element_type=F32))
    total = pfx[qn - 1:qn, :]
    cum = jnp.where(bwd, total - pfx + dta, pfx)
    cum_t = cum.T
    dt_t = dt.T
    mask = jnp.where(bwd, jj - ii, ii - jj) >= 0
    ys = []
    for g in range(S_GROUPS):
        bg = bm[:, g * 128:(g + 1) * 128]
        cg = cm[:, g * 128:(g + 1) * 128]
        cb = _dot_nt(cg, bg)
        bgt = bg.T
        hg = S_HEADS // S_GROUPS
        yoff = _dot(cg, h_sc[:, g * hg * 64:(g + 1) * hg * 64])
        for hh in range(hg):
            h = g * hg + hh
            ccol = cum[:, h:h + 1]
            crow = cum_t[h:h + 1, :]
            dtrow = dt_t[h:h + 1, :]
            dtcol = dt[:, h:h + 1]
            dec = jnp.exp(jnp.where(mask, ccol - crow, NEG_INF))
            w = cb * dec * dtrow
            xh = xs[:, h * 64:(h + 1) * 64]
            y = _dot(w, xh) + yoff[:, hh * 64:(hh + 1) * 64] * jnp.exp(ccol)
            tot = total[:, h:h + 1]
            xdec = xh * (jnp.exp(tot - ccol) * dtcol)
            h_sc[:, h * 64:(h + 1) * 64] = (jnp.exp(tot) * h_sc[:, h * 64:(h + 1) * 64]
                                            + _dot(bgt, xdec))
            ys.append(y)
    y = jnp.concatenate(ys, axis=1)

    @pl.when(jnp.logical_not(bwd))
    def _():
        yf_sc[slot] = y

    @pl.when(bwd)
    def _():
        z = z_ref[...].astype(F32)
        yt =(y + yf_sc[slot] + dsk_ref[...] * xs) * _silu(z)
        o_ref[...] = (_rms(yt, yt.shape[-1]) * nw_ref[...]).astype(o_ref.dtype)


def _ssd(a2, dtb2, dsk, nw, u, pdt, p, dims):
    b, s, lc, nl, t = dims
    c = S_CHUNK
    ncc, nlc = lc // c, s // c
    nc = ncc + nlc

    def row(bi, st):
        return _scan_row(bi, st, ncc, nlc, nl, c)

    def grow(bi, st):
        return row(bi, jnp.maximum(st, nc))

    const = lambda shape: pl.BlockSpec(shape, lambda bi, st: (0, 0))
    return pl.pallas_call(
        functools.partial(_ssd_kernel, ncc=ncc, nlc=nlc),
        grid=(b, 2 * nc),
        in_specs=[const((2, 128)), const((2, 128)), const((1, 512)), const((1, 512)),
                  pl.BlockSpec((c, 1024), lambda bi, st: (row(bi, st), 0)),
                  pl.BlockSpec((c, 128), lambda bi, st: (row(bi, st), 0)),
                  pl.BlockSpec((c, 512), lambda bi, st: (grow(bi, st), OFF_S // 512))],
        out_specs=pl.BlockSpec((c, 512), lambda bi, st: (grow(bi, st), 0)),
        out_shape=jax.ShapeDtypeStruct((t, 512), BF),
        scratch_shapes=[pltpu.VMEM((S_STATE, S_HEADS * S_HEAD_DIM), F32),
                        pltpu.VMEM((nc, c, 512), F32)],
        compiler_params=_cparams(("parallel", "arbitrary")),
        name="ssd",
    )(a2, dtb2, dsk, nw, u, pdt, p)


def _mprep_kernel(p_ref, cos_ref, slo_ref, shi_ref, cqw_ref, ckvw_ref, wq_ref, wkv_ref,
                  qw_ref, kw_ref, q_ref, k_ref, v_ref):
    cos, slo, shi = cos_ref[...], slo_ref[...], shi_ref[...]
    scale = M_QK_DIM ** -0.5 * math.log2(math.e)
    cq = _rms(p_ref[:, 0:512].astype(F32), M_Q_RANK) * cqw_ref[...]
    q = _dot(cq, wq_ref[...])
    ckv = _rms(p_ref[:, 512:640].astype(F32), M_KV_RANK) * ckvw_ref[...]
    kv = _dot(ckv, wkv_ref[...])
    kr = p_ref[:, 640:768].astype(F32)
    kr_ss = jnp.sum(kr * kr, axis=-1, keepdims=True)
    for h in range(M_HEADS):
        c0 = q[:, h * 256:h * 256 + 128]
        c1 = q[:, h * 256 + 128:h * 256 + 256]
        r = lax.rsqrt((jnp.sum(c0 * c0, axis=-1, keepdims=True)
                       + jnp.sum(c1 * c1, axis=-1, keepdims=True)) * (1.0 / M_QK_DIM) + EPS)
        q_ref[:, h * 256:h * 256 + 128] = (c0 * r * qw_ref[:, 0:128] * scale).astype(BF)
        q_ref[:, h * 256 + 128:h * 256 + 256] = (
            _rope(c1 * r * qw_ref[:, 128:256], cos, slo, shi, 16) * scale).astype(BF)
        k0 = kv[:, h * 128:(h + 1) * 128]
        r = lax.rsqrt((jnp.sum(k0 * k0, axis=-1, keepdims=True) + kr_ss) * (1.0 / M_QK_DIM) + EPS)
        k_ref[:, h * 256:h * 256 + 128] = (k0 * r * kw_ref[:, 0:128]).astype(BF)
        k_ref[:, h * 256 + 128:h * 256 + 256] = (
            _rope(kr * r * kw_ref[:, 128:256], cos, slo, shi, 16)).astype(BF)
    v_ref[...] = kv[:, 512:1024].T.astype(BF)


def _mprep(p, tabs, cqw, ckvw, wq, wkv, qw, kw, dims, tm):
    b, s, lc, nl, t = dims
    tab_spec = pl.BlockSpec((tm, 128), lambda i: (_table_index(i, tm, nl, s), 0))
    const = lambda shape: pl.BlockSpec(shape, lambda i: (0, 0))
    return pl.pallas_call(
        _mprep_kernel,
        grid=(t // tm,),
        in_specs=[pl.BlockSpec((tm, W_M), lambda i: (i, OFF_M // W_M)),
                  tab_spec, tab_spec, tab_spec,
                  const((1, 512)), const((1, 128)), const((512, 1024)), const((128, 1024)),
                  const((1, 256)), const((1, 256))],
        out_specs=[pl.BlockSpec((tm, 1024), lambda i: (i, 0)),
                   pl.BlockSpec((tm, 1024), lambda i: (i, 0)),
                   pl.BlockSpec((None, 512, tm), lambda i: (i, 0, 0))],
        out_shape=[jax.ShapeDtypeStruct((t, 1024), BF),
                   jax.ShapeDtypeStruct((t, 1024), BF),
                   jax.ShapeDtypeStruct((t // tm, 512, tm), BF)],
        compiler_params=_cparams(("parallel",)),
        name="m_prep",
    )(p, *tabs, cqw, ckvw, wq, wkv, qw, kw)


def _mla_kernel(q_ref, kx_ref, vx_ref, kl_ref, vl_ref, o_ref, acc_sc, s_sc, *, nq, tk):
    qi = pl.program_id(1)
    n_tiles = jnp.where(qi < nq, kl_ref.shape[0] // tk, 0)

    def scores(h, k_tile):
        return _dot_nt(k_tile[:, h * 256:(h + 1) * 256], q_ref[:, h * 256:(h + 1) * 256])

    def pv(h, vt_tile, p):
        return jnp.dot(vt_tile[h * 128:(h + 1) * 128, :], p.astype(BF), preferred_element_type=F32)

    ms, ls = [], []
    for h in range(M_HEADS):
        s = scores(h, kx_ref)
        m0 = jnp.max(s, axis=0, keepdims=True)
        p = jnp.exp2(s - m0)
        ms.append(m0)
        ls.append(jnp.sum(p, axis=0, keepdims=True))
        acc_sc[h] = pv(h, vx_ref, p)

    def k_tile(kt):
        return kl_ref.at[pl.ds(pl.multiple_of(kt * tk, tk), tk), :]

    for h in range(M_HEADS):
        s_sc[h] = scores(h, k_tile(0))

    def body(kt, carry):
        ms, ls = carry
        nxt = k_tile(jnp.minimum(kt + 1, n_tiles - 1))
        vt_tile = vl_ref.at[kt]
        ms_new, ls_new = [], []
        s_cur = [s_sc[h] for h in range(M_HEADS)]
        s_nxt = [scores(h, nxt) for h in range(M_HEADS)]
        for h in range(M_HEADS):
            s = s_cur[h]
            m_new = jnp.maximum(ms[h], jnp.max(s, axis=0, keepdims=True))
            alpha = jnp.exp2(ms[h] - m_new)
            p = jnp.exp2(s - m_new)
            ms_new.append(m_new)
            ls_new.append(alpha * ls[h] + jnp.sum(p, axis=0, keepdims=True))
            acc_sc[h] = alpha * acc_sc[h] + pv(h, vt_tile, p)
        for h in range(M_HEADS):
            s_sc[h] = s_nxt[h]
        return tuple(ms_new), tuple(ls_new)

    ms, ls = lax.fori_loop(0, n_tiles, body, (tuple(ms), tuple(ls)))
    for h in range(M_HEADS):
        o = acc_sc[h] * (1.0 / ls[h])
        o_ref[:, h * 128:(h + 1) * 128] = o.T.astype(o_ref.dtype)


def _mla(qm, km, vmt, dims, need_ctx, tq, tk):
    b, s, lc, nl, t = dims
    nq = s // tq
    assert lc == tq and lc == tk and vmt.shape[2] == tk
    steps = nq + (1 if need_ctx else 0)
    rows = nl + (b * lc if need_ctx else 0)

    def qrow(bi, qi):
        return jnp.where(qi < nq, bi * nq + qi, nl // tq + bi)

    return pl.pallas_call(
        functools.partial(_mla_kernel, nq=nq, tk=tk),
        grid=(b, steps),
        in_specs=[pl.BlockSpec((tq, 1024), lambda bi, qi: (qrow(bi, qi), 0)),
                  pl.BlockSpec((lc, 1024), lambda bi, qi: (nl // lc + bi, 0)),
                  pl.BlockSpec((None, 512, tk), lambda bi, qi: (nl // tk + bi, 0, 0)),
                  pl.BlockSpec((s, 1024), lambda bi, qi: (bi, 0)),
                  pl.BlockSpec((s // tk, 512, tk), lambda bi, qi: (bi, 0, 0))],
        out_specs=pl.BlockSpec((tq, 512), lambda bi, qi: (qrow(bi, qi), 0)),
        out_shape=jax.ShapeDtypeStruct((rows, 512), BF),
        scratch_shapes=[pltpu.VMEM((M_HEADS, M_V, tq), F32),
                        pltpu.VMEM((M_HEADS, tk, tq), F32)],
        compiler_params=_cparams(("parallel", "arbitrary")),
        name="mla",
    )(qm, km, vmt, km, vmt)


def _merge_kernel(x_ref, mod_ref, ya_ref, yb_ref, yc_ref, yd_ref, g0_ref, g1_ref, g2_ref, g3_ref,
                  wb_ref, wo_ref, o_ref, acc_sc, *, tn):
    j = pl.program_id(1)
    acc = None
    for kk, (y_ref, g_ref) in enumerate(((ya_ref, g0_ref), (yb_ref, g1_ref),
                                         (yc_ref, g2_ref), (yd_ref, g3_ref))):
        gate = 1.0 / (1.0 + jnp.exp(-g_ref[...].astype(F32)))
        term = gate * jnp.dot(y_ref[...], wb_ref[kk], preferred_element_type=F32)
        acc = term if acc is None else acc + term
    acc_sc[:, pl.ds(pl.multiple_of(j * tn, tn), tn)] = acc.astype(BF)

    @pl.when(j == pl.num_programs(1) - 1)
    def _():
        o_ref[...] = x_ref[...] + mod_ref[2:3, :] * jnp.dot(
            acc_sc[...], wo_ref[...], preferred_element_type=F32)


def _merge(xs, mod, ys, p, wb, wo, layer, dims, rows, tm, tn):
    b, s, lc, nl, t = dims
    d = xs.shape[1]
    nj = d // tn
    goff = OFF_G // tn
    yspec = pl.BlockSpec((tm, BRANCH_W), lambda i, j: (i, 0))
    gspecs = [pl.BlockSpec((tm, tn), functools.partial(lambda i, j, kk: (i, goff + kk * nj + j), kk=kk))
              for kk in range(N_BRANCH)]
    return pl.pallas_call(
        functools.partial(_merge_kernel, tn=tn),
        grid=(rows // tm, nj),
        in_specs=[pl.BlockSpec((tm, d), lambda i, j: (i, 0)),
                  pl.BlockSpec((None, N_MOD, d), lambda i, j: (_mod_index(i, tm, nl, s, b), 0, 0)),
                  yspec, yspec, yspec, yspec, *gspecs,
                  pl.BlockSpec((None, N_BRANCH, BRANCH_W, tn), lambda i, j: (layer, 0, 0, j)),
                  pl.BlockSpec((None, d, d), lambda i, j: (layer, 0, 0))],
        out_specs=pl.BlockSpec((tm, d), lambda i, j: (i, 0)),
        out_shape=jax.ShapeDtypeStruct((rows, d), F32),
        scratch_shapes=[pltpu.VMEM((tm, d), BF)],
        compiler_params=_cparams(("parallel", "arbitrary")),
        name="merge",
    )(xs, mod, *ys, p, p, p, p, wb, wo)


def _ffn_kernel(x_ref, mod_ref, nw_ref, w1_ref, w2_ref, o_ref, h_sc):
    f = pl.program_id(1)

    @pl.when(f == 0)
    def _():
        x = x_ref[...]
        y = _rms(x, x.shape[-1]) * nw_ref[...]
        h_sc[...] = (y * (1.0 + mod_ref[4:5, :]) + mod_ref[3:4, :]).astype(BF)

    u = jnp.maximum(jnp.dot(h_sc[...], w1_ref[...], preferred_element_type=F32), 0.0)
    part = jnp.dot((u * u).astype(BF), w2_ref[...], preferred_element_type=F32)

    @pl.when(f == 0)
    def _():
        o_ref[...] = part

    @pl.when(f > 0)
    def _():
        o_ref[...] += part

    @pl.when(f == pl.num_programs(1) - 1)
    def _():
        o_ref[...] = x_ref[...] + mod_ref[5:6, :] * o_ref[...]


def _ffn(xs, mod, nw, w1, w2, layer, dims, rows, tm, tf):
    b, s, lc, nl, t = dims
    d = xs.shape[1]
    dff = w1.shape[2]
    return pl.pallas_call(
        _ffn_kernel,
        grid=(rows // tm, dff // tf),
        in_specs=[pl.BlockSpec((tm, d), lambda i, f: (i, 0)),
                  pl.BlockSpec((None, N_MOD, d), lambda i, f: (_mod_index(i, tm, nl, s, b), 0, 0)),
                  pl.BlockSpec((1, d), lambda i, f: (0, 0)),
                  pl.BlockSpec((None, d, tf), lambda i, f: (layer, 0, f)),
                  pl.BlockSpec((None, tf, d), lambda i, f: (layer, f, 0))],
        out_specs=pl.BlockSpec((tm, d), lambda i, f: (i, 0)),
        out_shape=jax.ShapeDtypeStruct((rows, d), F32),
        scratch_shapes=[pltpu.VMEM((tm, d), BF)],
        compiler_params=_cparams(("parallel", "arbitrary")),
        name="ffn",
    )(xs, mod, nw, w1, w2)


def _rope_tables(pos_per_lane, freq_per_lane, lo_lane, pad_rows, live_lane=None):
    ang = pos_per_lane * freq_per_lane[None, :]
    cos, sin = jnp.cos(ang), jnp.sin(ang)
    if live_lane is not None:
        cos = jnp.where(live_lane[None, :], cos, 1.0)
        sin = jnp.where(live_lane[None, :], sin, 0.0)
    s_lo = jnp.where(lo_lane[None, :], -sin, 0.0)
    s_hi = jnp.where(lo_lane[None, :], 0.0, sin)
    w = cos.shape[1]
    ident = (jnp.ones((pad_rows, w), F32), jnp.zeros((pad_rows, w), F32), jnp.zeros((pad_rows, w), F32))
    return tuple(jnp.concatenate([tb.astype(F32), idn], axis=0)
                 for tb, idn in zip((cos, s_lo, s_hi), ident))


def _tables(s, pad_rows):
    t = jnp.arange(s)
    row = (t // GRID_W).astype(F32)[:, None]
    col = (t % GRID_W).astype(F32)[:, None]
    tpos = t.astype(F32)[:, None]
    lane = jnp.arange(128)
    fa = ROPE_BASE ** (-(lane % 32).astype(F32) / 32)
    pos_a = jnp.where((lane < 64)[None, :], row, col)
    tab_a = _rope_tables(pos_a, fa, (lane % 64) < 32, pad_rows)
    lane = jnp.arange(256)
    fb = ROPE_BASE ** (-(lane % 32).astype(F32) / 32)
    tab_r = _rope_tables(jnp.broadcast_to(tpos, (s, 256)), fb, (lane % 64) < 32, pad_rows)
    lane = jnp.arange(128)
    fm = ROPE_BASE ** (-(lane % 16).astype(F32) / 16)
    pos_m = jnp.where((lane < 32)[None, :], row, col)
    tab_m = _rope_tables(pos_m, fm, (lane % 32) < 16, pad_rows, live_lane=lane < 64)
    return tab_a, tab_r, tab_m


def _pad_cols(w, n):
    return jnp.pad(w, [(0, 0)] * (w.ndim - 1) + [(0, n - w.shape[-1])])


def kernel(x, c, ctx, c_ctx, w_ada, b_ada, norm1_w, norm2_w, w_in, a_q_norm, a_k_norm, a_sink,
           r_decay, r_norm, s_conv_w, s_conv_b, s_a_log, s_dt_bias, s_d, s_norm,
           m_cq_norm, m_ckv_norm, m_w_uq, m_w_ukv, m_q_norm, m_k_norm,
           w_branch, w_o, w_ff1, w_ff2):
    b, s, d = x.shape
    lc = ctx.shape[1]
    depth = w_ada.shape[0]
    nl, ncx = b * s, b * lc
    t = nl + ncx
    dims = (b, s, lc, nl, t)
    tm_big = max(m for m in (1024, 512, 256) if s % m == 0 and ncx % m == 0)
    tm = max(m for m in (512, 256) if s % m == 0 and ncx % m == 0)
    tp = 256
    assert lc == 256 and s % 256 == 0 and b + 1 <= 8

    a_w = w_in[..., 0:1024]
    r_w = w_in[..., 1024:2560]
    s_w = w_in[..., 2560:4096]
    dt_w = w_in[..., 4096:4112]
    m_w = w_in[..., 4112:4816]
    g_w = w_in[..., 4816:]
    w_in_p = jnp.concatenate([_pad_cols(m_w, W_M), _pad_cols(dt_w, W_DT), a_w, r_w, s_w, g_w],
                             axis=-1).astype(BF)
    wq = _pad_cols(m_w_uq.reshape(depth, M_Q_RANK, M_HEADS, M_QK_DIM), 256)
    wq = wq.reshape(depth, M_Q_RANK, M_HEADS * 256).astype(BF)
    wkv = m_w_ukv.reshape(depth, M_KV_RANK, M_HEADS, M_NOPE + M_V)
    wkv = jnp.concatenate([wkv[..., :M_NOPE].reshape(depth, M_KV_RANK, M_HEADS * M_NOPE),
                           wkv[..., M_NOPE:].reshape(depth, M_KV_RANK, M_HEADS * M_V)],
                          axis=-1).astype(BF)
    wb = w_branch.astype(BF)
    wo = w_o.astype(BF)
    w1 = w_ff1.astype(BF)
    w2 = w_ff2.astype(BF)
    log_gamma = jnp.log1p(-jnp.exp(r_decay.astype(F32)))
    a_neg = _pad_cols(-jnp.exp(s_a_log.astype(F32)), 128)
    dtb = _pad_cols(s_dt_bias.astype(F32), 128)
    dsk = jnp.repeat(s_d.astype(F32), S_HEAD_DIM, axis=-1)[:, None, :]
    mqw = _pad_cols(m_q_norm, 256)[:, None, :]
    mkw = _pad_cols(m_k_norm, 256)[:, None, :]
    tab_a, tab_r, tab_m = _tables(s, tp)

    c8 = jnp.zeros((8, d), F32).at[:b].set(c).at[b].set(c_ctx)
    mod_all = _ada(c8, w_ada, b_ada).reshape(depth, 8, N_MOD, d)

    xs = jnp.concatenate([x.reshape(nl, d), ctx.reshape(ncx, d)], axis=0)
    for i in range(depth):
        need_ctx = i < depth - 1
        rows = t if need_ctx else nl
        mod = mod_all[i]
        p, pdt = _inproj(xs, mod, norm1_w[i][None, :], w_in_p, i, dims, tm_big, 1024)
        qa, ka, va = _aprep(p, tab_a, a_q_norm[i][None, :], a_k_norm[i][None, :], dims, tp)
        ya = _aattn(a_sink[i].astype(F32), qa, ka, va, dims, need_ctx)
        yb = _retention(log_gamma[i], p, tab_r, r_norm[i].reshape(1, -1), dims)
        u = _conv(p, s_conv_w[i], s_conv_b[i][None, :], dims, tp)
        yc = _ssd(a_neg[i], dtb[i], dsk[i], s_norm[i][None, :], u, pdt, p, dims)
        qm, km, vm = _mprep(p, tab_m, m_cq_norm[i][None, :],
                            m_ckv_norm[i][None, :], wq[i], wkv[i], mqw[i], mkw[i], dims, tp)
        yd = _mla(qm, km, vm, dims, need_ctx, 256, 256)
        xs = _merge(xs, mod, (ya, yb, yc, yd), p, wb, wo, i, dims, rows, tm, 512)
        xs = _ffn(xs, mod, norm2_w[i][None, :], w1, w2, i, dims, rows, tm, 1024)
    return xs.reshape(b, s, d)
```

```python
import functools
import math

import jax
import jax.numpy as jnp
import numpy as np
from jax import lax
from jax.experimental import pallas as pl
from jax.experimental.pallas import tpu as pltpu

GRID_W = 64
EPS = 1e-6
ROPE_BASE = 10000.0
NEG_INF = -1e30

A_HEADS, A_KV_HEADS, A_HEAD_DIM, A_BLOCK = 4, 2, 128, 128
R_HEADS, R_QK_DIM, R_V_DIM, R_CHUNK = 4, 64, 128, 128
S_HEADS, S_HEAD_DIM, S_GROUPS, S_STATE, S_CONV, S_CHUNK = 8, 64, 2, 128, 5, 128
M_HEADS, M_Q_RANK, M_KV_RANK, M_NOPE, M_ROPE, M_V = 4, 512, 128, 128, 64, 128
M_QK_DIM = M_NOPE + M_ROPE
N_BRANCH, N_MOD = 4, 6
BRANCH_W = 512

OFF_M, W_M = 0, 768
OFF_DT, W_DT = 768, 256
OFF_A, W_A = 1024, 1024
OFF_R, W_R = 2048, 1536
OFF_S, W_S = 3584, 1536
OFF_G = 5120

VMEM_LIMIT = 56 * 1024 * 1024

BF = jnp.bfloat16
F32 = jnp.float32


def _cparams(sem):
    return pltpu.CompilerParams(dimension_semantics=sem, vmem_limit_bytes=VMEM_LIMIT)


def _dot(a, b):
    return jnp.dot(a.astype(BF), b.astype(BF), preferred_element_type=F32)


def _dot_nt(a, b):
    return lax.dot_general(a.astype(BF), b.astype(BF), (((1,), (1,)), ((), ())),
                           preferred_element_type=F32)


def _silu(x):
    return x * (1.0 / (1.0 + jnp.exp(-x)))


def _rms(x, n):
    return x * lax.rsqrt(jnp.sum(x * x, axis=-1, keepdims=True) * (1.0 / n) + EPS)


def _rope(y, cos, s_lo, s_hi, shift):
    w = y.shape[-1]
    return y * cos + pltpu.roll(y, w - shift, 1) * s_lo + pltpu.roll(y, shift, 1) * s_hi


def _ada_kernel(c_ref, w_ref, b_ref, o_ref):
    o_ref[...] = _dot(_silu(c_ref[...]), w_ref[...]) + b_ref[...]


def _ada(c8, w_ada, b_ada):
    depth, d, n = w_ada.shape
    tn = 512
    return pl.pallas_call(
        _ada_kernel,
        grid=(depth, n // tn),
        in_specs=[pl.BlockSpec((8, d), lambda l, j: (0, 0)),
                  pl.BlockSpec((None, d, tn), lambda l, j: (l, 0, j)),
                  pl.BlockSpec((None, 1, tn), lambda l, j: (l, 0, j))],
        out_specs=pl.BlockSpec((None, 8, tn), lambda l, j: (l, 0, j)),
        out_shape=jax.ShapeDtypeStruct((depth, 8, n), F32),
        compiler_params=_cparams(("parallel", "parallel")),
        name="ada",
    )(c8, w_ada, b_ada.reshape(depth, 1, n))


def _inproj_kernel(x_ref, mod_ref, nw_ref, w_ref, o_ref, dt_ref, h_sc):
    j = pl.program_id(1)

    @pl.when(j == 0)
    def _():
        x = x_ref[...]
        y = _rms(x, x.shape[-1]) * nw_ref[...]
        h_sc[...] = (y * (1.0 + mod_ref[1:2, :]) + mod_ref[0:1, :]).astype(BF)

    r = jnp.dot(h_sc[...], w_ref[...], preferred_element_type=F32)
    o_ref[...] = r.astype(o_ref.dtype)

    @pl.when(j == 0)
    def _():
        dt_ref[...] = r[:, OFF_DT:OFF_DT + W_DT]


def _mod_index(i, tm, nl, s, b):
    return jnp.where(i < nl // tm, i // (s // tm), b)


def _inproj(xs, mod, nw, w, layer, dims, tm, tn):
    b, s, lc, nl, t = dims
    d = xs.shape[1]
    n = w.shape[2]
    assert OFF_DT + W_DT <= tn
    return pl.pallas_call(
        _inproj_kernel,
        grid=(t // tm, n // tn),
        in_specs=[pl.BlockSpec((tm, d), lambda i, j: (i, 0)),
                  pl.BlockSpec((None, N_MOD, d), lambda i, j: (_mod_index(i, tm, nl, s, b), 0, 0)),
                  pl.BlockSpec((1, d), lambda i, j: (0, 0)),
                  pl.BlockSpec((None, d, tn), lambda i, j: (layer, 0, j))],
        out_specs=[pl.BlockSpec((tm, tn), lambda i, j: (i, j)),
                   pl.BlockSpec((tm, W_DT), lambda i, j: (i, 0))],
        out_shape=[jax.ShapeDtypeStruct((t, n), BF),
                   jax.ShapeDtypeStruct((t, W_DT), F32)],
        scratch_shapes=[pltpu.VMEM((tm, d), BF)],
        compiler_params=_cparams(("parallel", "arbitrary")),
        name="inproj",
    )(xs, mod, nw, w)


def _aprep_kernel(p_ref, cos_ref, slo_ref, shi_ref, qw_ref, kw_ref, q_ref, k_ref, v_ref):
    cos, slo, shi = cos_ref[...], slo_ref[...], shi_ref[...]
    scale = A_HEAD_DIM ** -0.5
    for h in range(A_HEADS):
        y = _rms(p_ref[:, h * 128:(h + 1) * 128].astype(F32), A_HEAD_DIM) * qw_ref[...]
        q_ref[:, h * 128:(h + 1) * 128] = (_rope(y, cos, slo, shi, 32) * scale).astype(BF)
    for h in range(A_KV_HEADS):
        y = _rms(p_ref[:, 512 + h * 128:512 + (h + 1) * 128].astype(F32), A_HEAD_DIM) * kw_ref[...]
        k_ref[:, h * 128:(h + 1) * 128] = _rope(y, cos, slo, shi, 32).astype(BF)
    v_ref[...] = p_ref[:, 768:1024].astype(BF)


def _table_index(i, tm, nl, s):
    return jnp.where(i < nl // tm, i % (s // tm), s // tm)


def _aprep(p, tabs, qw, kw, dims, tm):
    b, s, lc, nl, t = dims
    tab_spec = pl.BlockSpec((tm, 128), lambda i: (_table_index(i, tm, nl, s), 0))
    return pl.pallas_call(
        _aprep_kernel,
        grid=(t // tm,),
        in_specs=[pl.BlockSpec((tm, W_A), lambda i: (i, OFF_A // W_A)),
                  tab_spec, tab_spec, tab_spec,
                  pl.BlockSpec((1, 128), lambda i: (0, 0)),
                  pl.BlockSpec((1, 128), lambda i: (0, 0))],
        out_specs=[pl.BlockSpec((tm, 512), lambda i: (i, 0)),
                   pl.BlockSpec((tm, 256), lambda i: (i, 0)),
                   pl.BlockSpec((tm, 256), lambda i: (i, 0))],
        out_shape=[jax.ShapeDtypeStruct((t, 512), BF),
                   jax.ShapeDtypeStruct((t, 256), BF),
                   jax.ShapeDtypeStruct((t, 256), BF)],
        compiler_params=_cparams(("parallel",)),
        name="a_prep",
    )(p, *tabs, qw, kw)


def _aattn_kernel(sink_ref, q_ref, kp_ref, kc_ref, kn_ref, vp_ref, vc_ref, vn_ref,
                  kx_ref, vx_ref, o_ref, *, nb):
    n = pl.program_id(1)
    blk = A_BLOCK
    lc = kx_ref.shape[0]
    g = A_HEADS // A_KV_HEADS
    nk = 3 * blk + lc
    qi = lax.broadcasted_iota(jnp.int32, (g * blk, nk), 0) % blk
    kj = lax.broadcasted_iota(jnp.int32, (g * blk, nk), 1)
    lo_seq = jnp.where(n == 0, blk, 0)
    hi_seq = jnp.where(n < nb, jnp.where(n == nb - 1, 2 * blk - 1, 3 * blk - 1), -1)
    mask = ((kj >= jnp.maximum(qi, lo_seq)) & (kj <= jnp.minimum(qi + 2 * blk, hi_seq))) | (kj >= 3 * blk)
    row = lax.broadcasted_iota(jnp.int32, (g * blk, 1), 0)
    for hk in range(A_KV_HEADS):
        sl = slice(hk * 128, (hk + 1) * 128)
        q2 = jnp.concatenate([q_ref[:, (hk * g + gi) * 128:(hk * g + gi + 1) * 128]
                              for gi in range(g)], axis=0)
        kk = jnp.concatenate([kp_ref[:, sl], kc_ref[:, sl], kn_ref[:, sl], kx_ref[:, sl]], axis=0)
        vv = jnp.concatenate([vp_ref[:, sl], vc_ref[:, sl], vn_ref[:, sl], vx_ref[:, sl]], axis=0)
        s = _dot_nt(q2, kk)
        s = jnp.where(mask, s, NEG_INF)
        sink = jnp.where(row < blk, sink_ref[hk * g], sink_ref[hk * g + 1])
        m = jnp.maximum(jnp.max(s, axis=-1, keepdims=True), sink)
        p = jnp.exp(s - m)
        inv = 1.0 / (jnp.sum(p, axis=-1, keepdims=True) + jnp.exp(sink - m))
        o = _dot(p, vv) * inv
        for gi in range(g):
            o_ref[:, (hk * g + gi) * 128:(hk * g + gi + 1) * 128] = (
                o[gi * blk:(gi + 1) * blk]).astype(o_ref.dtype)


def _aattn(sink, qa, ka, va, dims, need_ctx):
    b, s, lc, nl, t = dims
    blk = A_BLOCK
    nb = s // blk
    ncb = lc // blk
    steps = nb + (ncb if need_ctx else 0)
    rows = nl + (b * lc if need_ctx else 0)

    def qrow(bi, n):
        return jnp.where(n < nb, bi * nb + n, nl // blk + bi * ncb + (n - nb))

    def krow(off):
        def f(bi, n):
            return (bi * nb + jnp.clip(n + off, 0, nb - 1), 0)
        return f

    kspec = [pl.BlockSpec((blk, 256), krow(o)) for o in (-1, 0, 1)]
    xspec = pl.BlockSpec((lc, 256), lambda bi, n: (nl // lc + bi, 0))
    return pl.pallas_call(
        functools.partial(_aattn_kernel, nb=nb),
        grid=(b, steps),
        in_specs=[pl.BlockSpec(memory_space=pltpu.SMEM),
                  pl.BlockSpec((blk, 512), lambda bi, n: (qrow(bi, n), 0)),
                  *kspec, *kspec, xspec, xspec],
        out_specs=pl.BlockSpec((blk, 512), lambda bi, n: (qrow(bi, n), 0)),
        out_shape=jax.ShapeDtypeStruct((rows, 512), BF),
        compiler_params=_cparams(("parallel", "arbitrary")),
        name="a_attn",
    )(sink, qa, ka, ka, ka, va, va, va, ka, va)


def _scan_slot(t, ncc, nlc, bwd):
    if not bwd:
        return t
    return jnp.where(t < ncc, ncc - 1 - t, ncc + (nlc - 1 - (t - ncc)))


def _scan_row(bi, t, ncc, nlc, nl, chunk, bwd):
    slot = _scan_slot(t, ncc, nlc, bwd)
    return jnp.where(slot < ncc, nl // chunk + bi * ncc + slot, bi * nlc + (slot - ncc))


def _ret_kernel(lg_ref, qf_ref, kf_ref, vf_ref, qb_ref, kb_ref, vb_ref,
                cosf_ref, slof_ref, shif_ref, cosb_ref, slob_ref, shib_ref,
                of_ref, ob_ref, st_sc, dm_sc, eq_sc, ek_sc):
    c = R_CHUNK
    hd = R_QK_DIM
    nh = R_HEADS

    def per_head(idx, vals):
        out = vals[nh - 1]
        for h in range(nh - 2, -1, -1):
            out = jnp.where(idx == h, vals[h], out)
        return out

    @pl.when(pl.program_id(1) == 0)
    def _():
        st_sc[...] = jnp.zeros_like(st_sc)
        ii = lax.broadcasted_iota(jnp.int32, (c, c), 0)
        jj = lax.broadcasted_iota(jnp.int32, (c, c), 1)
        col = lax.broadcasted_iota(jnp.int32, (c, 1), 0)
        row = lax.broadcasted_iota(jnp.int32, (1, c), 1)
        lane_head = lax.broadcasted_iota(jnp.int32, (1, nh * hd), 1) // hd
        sub_head = lax.broadcasted_iota(jnp.int32, (nh * hd, 1), 0) // hd
        for d in range(2):
            dd = ((jj - ii) if d else (ii - jj)).astype(F32)
            pcol = ((c - 1 - col) if d else col).astype(F32)
            prow = ((c - 1 - row) if d else row).astype(F32)
            lgs = [lg_ref[d, h] for h in range(nh)]
            eq_sc[d] = jnp.exp(per_head(lane_head, lgs) * (pcol + 1.0))
            ek_sc[d] = jnp.exp(per_head(sub_head, lgs) * (c - 1.0 - prow))
            for h in range(nh):
                dm_sc[d, h] = jnp.where(dd >= 0, jnp.exp(lgs[h] * jnp.maximum(dd, 0.0)), 0.0)

    lane_head = lax.broadcasted_iota(jnp.int32, (1, 2 * hd), 1) // hd
    dirs = []
    for d, (q_ref, k_ref, v_ref, cos_ref, slo_ref, shi_ref) in enumerate((
            (qf_ref, kf_ref, vf_ref, cosf_ref, slof_ref, shif_ref),
            (qb_ref, kb_ref, vb_ref, cosb_ref, slob_ref, shib_ref))):
        cos, slo, shi = cos_ref[...], slo_ref[...], shi_ref[...]
        q = _rope(q_ref[...].astype(F32), cos, slo, shi, 32)
        k = _rope(k_ref[...].astype(F32), cos, slo, shi, 32) * (hd ** -0.5)
        kt = k.T
        dirs.append((q, q * eq_sc[d], kt, kt * ek_sc[d], v_ref[...], st_sc[d]))
    outs = ([], [])
    new_states = ([], [])
    for h in range(nh):
        pr = h // 2
        for d, (q, qd, kt, kdt, v, state) in enumerate(dirs):
            sel = lane_head == (h % 2)
            qh = jnp.where(sel, q[:, pr * 128:(pr + 1) * 128], 0.0)
            qdh = jnp.where(sel, qd[:, pr * 128:(pr + 1) * 128], 0.0)
            vh = v[:, h * 128:(h + 1) * 128]
            s = _dot(qh, kt[pr * 128:(pr + 1) * 128, :]) * dm_sc[d, h]
            inter = _dot(qdh, state[pr * 128:(pr + 1) * 128, :])
            outs[d].append(_dot(s, vh) + inter)
            chunk_decay = jnp.exp(lg_ref[d, h] * jnp.full((1, R_V_DIM), float(c), F32))
            new_states[d].append(chunk_decay * state[h * hd:(h + 1) * hd, :]
                                 + _dot(kdt[h * hd:(h + 1) * hd, :], vh))
    for d, o_ref in enumerate((of_ref, ob_ref)):
        st_sc[d] = jnp.concatenate(new_states[d], axis=0)
        o_ref[...] = jnp.concatenate(outs[d], axis=1).astype(o_ref.dtype)


def _retention(lg, p, tabs, dims):
    b, s, lc, nl, t = dims
    c = R_CHUNK
    ncc, nlc = lc // c, s // c

    def row(bwd):
        return lambda bi, st: _scan_row(bi, st, ncc, nlc, nl, c, bwd)

    def pspec(width, col_off, bwd):
        r = row(bwd)
        return pl.BlockSpec((c, width), lambda bi, st: (r(bi, st), col_off // width))

    def tab_spec(bwd):
        def trow(bi, st):
            slot = _scan_slot(st, ncc, nlc, bwd)
            return (jnp.where(slot < ncc, nlc, slot - ncc), 0)
        return pl.BlockSpec((c, 256), trow)

    def qkv(bwd):
        return [pspec(256, OFF_R, bwd), pspec(256, OFF_R + 256, bwd), pspec(512, OFF_R + 512, bwd)]

    def ospec(bwd):
        r = row(bwd)
        return pl.BlockSpec((c, 512), lambda bi, st: (r(bi, st), 0))

    return pl.pallas_call(
        _ret_kernel,
        grid=(b, ncc + nlc),
        in_specs=[pl.BlockSpec(memory_space=pltpu.SMEM), *qkv(False), *qkv(True),
                  *[tab_spec(False)] * 3, *[tab_spec(True)] * 3],
        out_specs=[ospec(False), ospec(True)],
        out_shape=[jax.ShapeDtypeStruct((t, 512), BF)] * 2,
        scratch_shapes=[pltpu.VMEM((2, R_HEADS * R_QK_DIM, R_V_DIM), F32),
                        pltpu.VMEM((2, R_HEADS, c, c), F32),
                        pltpu.VMEM((2, c, R_HEADS * R_QK_DIM), F32),
                        pltpu.VMEM((2, R_HEADS * R_QK_DIM, c), F32)],
        compiler_params=_cparams(("parallel", "arbitrary")),
        name="retention",
    )(lg, p, p, p, p, p, p, *tabs, *tabs)


def _conv_kernel(x_ref, xp_ref, xn_ref, w_ref, b_ref, o_ref, e_sc, *, tm, blocks_per_seg, n_lat_blocks):
    i = pl.program_id(0)
    is_lat = i < n_lat_blocks
    first = jnp.logical_or(jnp.logical_not(is_lat), i % blocks_per_seg == 0)
    last = jnp.logical_or(jnp.logical_not(is_lat), i % blocks_per_seg == blocks_per_seg - 1)
    hr = HALO_ROWS
    e_sc[0:hr, :] = jnp.where(first, 0.0, xp_ref[...].astype(F32))
    e_sc[hr:hr + tm, :] = x_ref[...].astype(F32)
    e_sc[hr + tm:2 * hr + tm, :] = jnp.where(last, 0.0, xn_ref[...].astype(F32))
    pad = (S_CONV - 1) // 2
    acc = b_ref[...] + w_ref[0:1, :] * e_sc[hr - pad:hr - pad + tm, :]
    for kk in range(1, S_CONV):
        acc = acc + w_ref[kk:kk + 1, :] * e_sc[hr - pad + kk:hr - pad + kk + tm, :]
    o_ref[...] = _silu(acc)


HALO_ROWS = 16


def _conv(p, w, bias, dims, tm):
    b, s, lc, nl, t = dims
    ch = w.shape[1]
    r8 = tm // HALO_ROWS
    nblk = t // tm
    xoff = (OFF_S + 512) // ch
    assert (OFF_S + 512) % ch == 0
    assert lc == tm
    return pl.pallas_call(
        functools.partial(_conv_kernel, tm=tm, blocks_per_seg=s // tm, n_lat_blocks=nl // tm),
        grid=(nblk,),
        in_specs=[pl.BlockSpec((tm, ch), lambda i: (i, xoff)),
                  pl.BlockSpec((HALO_ROWS, ch), lambda i: (jnp.maximum(i * r8 - 1, 0), xoff)),
                  pl.BlockSpec((HALO_ROWS, ch),
                               lambda i: (jnp.minimum((i + 1) * r8, nblk * r8 - 1), xoff)),
                  pl.BlockSpec((S_CONV, ch), lambda i: (0, 0)),
                  pl.BlockSpec((1, ch), lambda i: (0, 0))],
        out_specs=pl.BlockSpec((tm, ch), lambda i: (i, 0)),
        out_shape=jax.ShapeDtypeStruct((t, ch), F32),
        scratch_shapes=[pltpu.VMEM((tm + 2 * HALO_ROWS, ch), F32)],
        compiler_params=_cparams(("parallel",)),
        name="s_conv",
    )(p, p, p, w, bias)


def _ssd_kernel(a_ref, dtb_ref, uf_ref, dtf_ref, ub_ref, dtbw_ref, yf_ref, yb_ref, h_sc):
    qn = S_CHUNK

    @pl.when(pl.program_id(1) == 0)
    def _():
        h_sc[...] = jnp.zeros_like(h_sc)

    ii = lax.broadcasted_iota(jnp.int32, (qn, qn), 0)
    jj = lax.broadcasted_iota(jnp.int32, (qn, qn), 1)
    tri = jnp.where(jj <= ii, 1.0, 0.0).astype(BF)
    hg = S_HEADS // S_GROUPS
    dirs = []
    for d, (u_ref, dt_ref) in enumerate(((uf_ref, dtf_ref), (ub_ref, dtbw_ref))):
        xr = dt_ref[...] + dtb_ref[...]
        dt = jnp.maximum(xr, 0.0) + jnp.log1p(jnp.exp(-jnp.abs(xr)))
        dta = dt * a_ref[...]
        hi = dta.astype(BF)
        lo = (dta - hi.astype(F32)).astype(BF)
        lo2 = (dta - hi.astype(F32) - lo.astype(F32)).astype(BF)
        pfx = (jnp.dot(tri, hi, preferred_element_type=F32)
               + jnp.dot(tri, lo, preferred_element_type=F32)
               + jnp.dot(tri, lo2, preferred_element_type=F32))
        total = pfx[qn - 1:qn, :]
        cum = (total - pfx + dta) if d else pfx
        mask = ((jj - ii) if d else (ii - jj)) >= 0
        dirs.append((u_ref, dt, cum, cum.T, total, mask, h_sc[d]))
    lane_lo = lax.broadcasted_iota(jnp.int32, (1, 128), 1) < S_HEAD_DIM
    ys = ([], [])
    hs = ([], [])
    for g in range(S_GROUPS):
        grp = []
        for d, (u_ref, dt, cum, cum_t, total, mask, hprev) in enumerate(dirs):
            bg = u_ref[:, 512 + g * 128:512 + (g + 1) * 128]
            cg = u_ref[:, 768 + g * 128:768 + (g + 1) * 128]
            cb = _dot_nt(cg, bg)
            yoff = _dot(cg, hprev[:, g * hg * 64:(g + 1) * hg * 64])
            grp.append((cb, bg.T, yoff))
        for pr in range(hg // 2):
            h0 = g * hg + 2 * pr
            for d, (u_ref, dt, cum, cum_t, total, mask, hprev) in enumerate(dirs):
                cb, bgt, yoff = grp[d]
                ws, cbs, dbs, tots = [], [], [], []
                for h in (h0, h0 + 1):
                    ln = d * S_HEADS + h
                    ccol_b = jnp.broadcast_to(cum[:, ln:ln + 1], (qn, 128))
                    ws.append(cb * jnp.exp(jnp.where(mask, ccol_b - cum_t[ln:ln + 1, :], NEG_INF)))
                    cbs.append(ccol_b)
                    dbs.append(jnp.broadcast_to(dt[:, ln:ln + 1], (qn, 128)))
                    tots.append(total[:, ln:ln + 1])
                csel = jnp.where(lane_lo, cbs[0], cbs[1])
                tsel = jnp.where(lane_lo, tots[0], tots[1])
                xdt = u_ref[:, h0 * 64:h0 * 64 + 128] * jnp.where(lane_lo, dbs[0], dbs[1])
                rhs = jnp.concatenate([jnp.where(lane_lo, xdt, 0.0), jnp.where(lane_lo, 0.0, xdt)],
                                      axis=0)
                y = (_dot(jnp.concatenate(ws, axis=1), rhs)
                     + yoff[:, pr * 128:(pr + 1) * 128] * jnp.exp(csel))
                hnew = (jnp.exp(tsel) * hprev[:, h0 * 64:h0 * 64 + 128]
                        + _dot(bgt, xdt * jnp.exp(tsel - csel)))
                ys[d].append(y)
                hs[d].append(hnew)
    for d, y_ref in enumerate((yf_ref, yb_ref)):
        h_sc[d] = jnp.concatenate(hs[d], axis=1)
        y_ref[...] = jnp.concatenate(ys[d], axis=1).astype(y_ref.dtype)


def _ssd(a_flat, dtb_flat, u, pdt, dims):
    b, s, lc, nl, t = dims
    c = S_CHUNK
    ncc, nlc = lc // c, s // c

    def spec(width, bwd):
        return pl.BlockSpec(
            (c, width), lambda bi, st: (_scan_row(bi, st, ncc, nlc, nl, c, bwd), 0))

    const = pl.BlockSpec((1, 128), lambda bi, st: (0, 0))
    return pl.pallas_call(
        _ssd_kernel,
        grid=(b, ncc + nlc),
        in_specs=[const, const, spec(1024, False), spec(128, False),
                  spec(1024, True), spec(128, True)],
        out_specs=[spec(512, False), spec(512, True)],
        out_shape=[jax.ShapeDtypeStruct((t, 512), BF)] * 2,
        scratch_shapes=[pltpu.VMEM((2, S_STATE, S_HEADS * S_HEAD_DIM), F32)],
        compiler_params=_cparams(("parallel", "arbitrary")),
        name="ssd",
    )(a_flat, dtb_flat, u, pdt, u, pdt)


def _scan_finish_kernel(rf_ref, rb_ref, g_ref, rnw_ref, sf_ref, sb_ref, x_ref, z_ref,
                        dsk_ref, snw_ref, yb_ref, yc_ref):
    tot = rf_ref[...].astype(F32) + rb_ref[...].astype(F32)
    ys = [_rms(tot[:, h * 128:(h + 1) * 128], R_V_DIM) for h in range(R_HEADS)]
    y = jnp.concatenate(ys, axis=1) * rnw_ref[...]
    yb_ref[...] = (y * _silu(g_ref[...].astype(F32))).astype(yb_ref.dtype)
    yt = (sf_ref[...].astype(F32) + sb_ref[...].astype(F32)
          + dsk_ref[...] * x_ref[...].astype(F32)) * _silu(z_ref[...].astype(F32))
    yc_ref[...] = (_rms(yt, yt.shape[-1]) * snw_ref[...]).astype(yc_ref.dtype)


def _scan_finish(rf, rb, p, rnw, sf, sb, u, dsk, snw, rows, tm):
    blk = lambda col: pl.BlockSpec((tm, 512), lambda i: (i, col))
    const = pl.BlockSpec((1, 512), lambda i: (0, 0))
    return pl.pallas_call(
        _scan_finish_kernel,
        grid=(rows // tm,),
        in_specs=[blk(0), blk(0), blk((OFF_R + 1024) // 512), const,
                  blk(0), blk(0), blk(0), blk(OFF_S // 512), const, const],
        out_specs=[blk(0), blk(0)],
        out_shape=[jax.ShapeDtypeStruct((rows, 512), BF)] * 2,
        compiler_params=_cparams(("parallel",)),
        name="scan_finish",
    )(rf, rb, p, rnw, sf, sb, u, p, dsk, snw)


def _mprep_kernel(p_ref, cos_ref, slo_ref, shi_ref, cqw_ref, ckvw_ref, wq_ref, wkv_ref,
                  qw_ref, kw_ref, q_ref, k_ref, v_ref):
    cos, slo, shi = cos_ref[...], slo_ref[...], shi_ref[...]
    scale = M_QK_DIM ** -0.5 * math.log2(math.e)
    cq = _rms(p_ref[:, 0:512].astype(F32), M_Q_RANK) * cqw_ref[...]
    q = _dot(cq, wq_ref[...])
    ckv = _rms(p_ref[:, 512:640].astype(F32), M_KV_RANK) * ckvw_ref[...]
    kv = _dot(ckv, wkv_ref[...])
    kr = p_ref[:, 640:768].astype(F32)
    kr_ss = jnp.sum(kr * kr, axis=-1, keepdims=True)
    for h in range(M_HEADS):
        c0 = q[:, h * 256:h * 256 + 128]
        c1 = q[:, h * 256 + 128:h * 256 + 256]
        r = lax.rsqrt((jnp.sum(c0 * c0, axis=-1, keepdims=True)
                       + jnp.sum(c1 * c1, axis=-1, keepdims=True)) * (1.0 / M_QK_DIM) + EPS)
        q_ref[:, h * 256:h * 256 + 128] = (c0 * r * qw_ref[:, 0:128] * scale).astype(BF)
        q_ref[:, h * 256 + 128:h * 256 + 256] = (
            _rope(c1 * r * qw_ref[:, 128:256], cos, slo, shi, 16) * scale).astype(BF)
        k0 = kv[:, h * 128:(h + 1) * 128]
        r = lax.rsqrt((jnp.sum(k0 * k0, axis=-1, keepdims=True) + kr_ss) * (1.0 / M_QK_DIM) + EPS)
        k_ref[:, h * 256:h * 256 + 128] = (k0 * r * kw_ref[:, 0:128]).astype(BF)
        k_ref[:, h * 256 + 128:h * 256 + 256] = (
            _rope(kr * r * kw_ref[:, 128:256], cos, slo, shi, 16)).astype(BF)
    v_ref[...] = kv[:, 512:1024].T.astype(BF)


def _mprep(p, tabs, cqw, ckvw, wq, wkv, qw, kw, dims, tm):
    b, s, lc, nl, t = dims
    tab_spec = pl.BlockSpec((tm, 128), lambda i: (_table_index(i, tm, nl, s), 0))
    const = lambda shape: pl.BlockSpec(shape, lambda i: (0, 0))
    return pl.pallas_call(
        _mprep_kernel,
        grid=(t // tm,),
        in_specs=[pl.BlockSpec((tm, W_M), lambda i: (i, OFF_M // W_M)),
                  tab_spec, tab_spec, tab_spec,
                  const((1, 512)), const((1, 128)), const((512, 1024)), const((128, 1024)),
                  const((1, 256)), const((1, 256))],
        out_specs=[pl.BlockSpec((tm, 1024), lambda i: (i, 0)),
                   pl.BlockSpec((tm, 1024), lambda i: (i, 0)),
                   pl.BlockSpec((None, 512, tm), lambda i: (i, 0, 0))],
        out_shape=[jax.ShapeDtypeStruct((t, 1024), BF),
                   jax.ShapeDtypeStruct((t, 1024), BF),
                   jax.ShapeDtypeStruct((t // tm, 512, tm), BF)],
        compiler_params=_cparams(("parallel",)),
        name="m_prep",
    )(p, *tabs, cqw, ckvw, wq, wkv, qw, kw)


def _mla_kernel(q_ref, kx_ref, vx_ref, kl_ref, vl_ref, o_ref, acc_sc, s_sc, *, nq, tk):
    qi = pl.program_id(1)
    n_tiles = jnp.where(qi < nq, kl_ref.shape[0] // tk, 0)

    def scores(h, k_tile):
        return _dot_nt(k_tile[:, h * 256:(h + 1) * 256], q_ref[:, h * 256:(h + 1) * 256])

    def pv(h, vt_tile, p):
        return jnp.dot(vt_tile[h * 128:(h + 1) * 128, :], p.astype(BF), preferred_element_type=F32)

    ms, ls = [], []
    for h in range(M_HEADS):
        s = scores(h, kx_ref)
        m0 = jnp.max(s, axis=0, keepdims=True)
        p = jnp.exp2(s - m0)
        ms.append(m0)
        ls.append(jnp.sum(p, axis=0, keepdims=True))
        acc_sc[h] = pv(h, vx_ref, p)

    def k_tile(kt):
        return kl_ref.at[pl.ds(pl.multiple_of(kt * tk, tk), tk), :]

    for h in range(M_HEADS):
        s_sc[h] = scores(h, k_tile(0))

    def body(kt, carry):
        ms, ls = carry
        nxt = k_tile(jnp.minimum(kt + 1, n_tiles - 1))
        vt_tile = vl_ref.at[kt]
        ms_new, ls_new = [], []
        s_cur = [s_sc[h] for h in range(M_HEADS)]
        s_nxt = [scores(h, nxt) for h in range(M_HEADS)]
        for h in range(M_HEADS):
            s = s_cur[h]
            m_new = jnp.maximum(ms[h], jnp.max(s, axis=0, keepdims=True))
            alpha = jnp.exp2(ms[h] - m_new)
            p = jnp.exp2(s - m_new)
            ms_new.append(m_new)
            ls_new.append(alpha * ls[h] + jnp.sum(p, axis=0, keepdims=True))
            acc_sc[h] = alpha * acc_sc[h] + pv(h, vt_tile, p)
        for h in range(M_HEADS):
            s_sc[h] = s_nxt[h]
        return tuple(ms_new), tuple(ls_new)

    ms, ls = lax.fori_loop(0, n_tiles, body, (tuple(ms), tuple(ls)))
    for h in range(M_HEADS):
        o = acc_sc[h] * (1.0 / ls[h])
        o_ref[:, h * 128:(h + 1) * 128] = o.T.astype(o_ref.dtype)


def _mla(qm, km, vmt, dims, need_ctx, tq, tk):
    b, s, lc, nl, t = dims
    nq = s // tq
    assert lc == tq and lc == tk and vmt.shape[2] == tk
    steps = nq + (1 if need_ctx else 0)
    rows = nl + (b * lc if need_ctx else 0)

    def qrow(bi, qi):
        return jnp.where(qi < nq, bi * nq + qi, nl // tq + bi)

    return pl.pallas_call(
        functools.partial(_mla_kernel, nq=nq, tk=tk),
        grid=(b, steps),
        in_specs=[pl.BlockSpec((tq, 1024), lambda bi, qi: (qrow(bi, qi), 0)),
                  pl.BlockSpec((lc, 1024), lambda bi, qi: (nl // lc + bi, 0)),
                  pl.BlockSpec((None, 512, tk), lambda bi, qi: (nl // tk + bi, 0, 0)),
                  pl.BlockSpec((s, 1024), lambda bi, qi: (bi, 0)),
                  pl.BlockSpec((s // tk, 512, tk), lambda bi, qi: (bi, 0, 0))],
        out_specs=pl.BlockSpec((tq, 512), lambda bi, qi: (qrow(bi, qi), 0)),
        out_shape=jax.ShapeDtypeStruct((rows, 512), BF),
        scratch_shapes=[pltpu.VMEM((M_HEADS, M_V, tq), F32),
                        pltpu.VMEM((M_HEADS, tk, tq), F32)],
        compiler_params=_cparams(("parallel", "arbitrary")),
        name="mla",
    )(qm, km, vmt, km, vmt)


def _merge_kernel(x_ref, mod_ref, ya_ref, yb_ref, yc_ref, yd_ref, g0_ref, g1_ref, g2_ref, g3_ref,
                  wb_ref, wo_ref, o_ref, acc_sc, *, tn):
    j = pl.program_id(1)
    acc = None
    for kk, (y_ref, g_ref) in enumerate(((ya_ref, g0_ref), (yb_ref, g1_ref),
                                         (yc_ref, g2_ref), (yd_ref, g3_ref))):
        gate = 1.0 / (1.0 + jnp.exp(-g_ref[...].astype(F32)))
        term = gate * jnp.dot(y_ref[...], wb_ref[kk], preferred_element_type=F32)
        acc = term if acc is None else acc + term
    acc_sc[:, pl.ds(pl.multiple_of(j * tn, tn), tn)] = acc.astype(BF)

    @pl.when(j == pl.num_programs(1) - 1)
    def _():
        o_ref[...] = x_ref[...] + mod_ref[2:3, :] * jnp.dot(
            acc_sc[...], wo_ref[...], preferred_element_type=F32)


def _merge(xs, mod, ys, p, wb, wo, layer, dims, rows, tm, tn):
    b, s, lc, nl, t = dims
    d = xs.shape[1]
    nj = d // tn
    goff = OFF_G // tn
    yspec = pl.BlockSpec((tm, BRANCH_W), lambda i, j: (i, 0))
    gspecs = [pl.BlockSpec((tm, tn), functools.partial(lambda i, j, kk: (i, goff + kk * nj + j), kk=kk))
              for kk in range(N_BRANCH)]
    return pl.pallas_call(
        functools.partial(_merge_kernel, tn=tn),
        grid=(rows // tm, nj),
        in_specs=[pl.BlockSpec((tm, d), lambda i, j: (i, 0)),
                  pl.BlockSpec((None, N_MOD, d), lambda i, j: (_mod_index(i, tm, nl, s, b), 0, 0)),
                  yspec, yspec, yspec, yspec, *gspecs,
                  pl.BlockSpec((None, N_BRANCH, BRANCH_W, tn), lambda i, j: (layer, 0, 0, j)),
                  pl.BlockSpec((None, d, d), lambda i, j: (layer, 0, 0))],
        out_specs=pl.BlockSpec((tm, d), lambda i, j: (i, 0)),
        out_shape=jax.ShapeDtypeStruct((rows, d), F32),
        scratch_shapes=[pltpu.VMEM((tm, d), BF)],
        compiler_params=_cparams(("parallel", "arbitrary")),
        name="merge",
    )(xs, mod, *ys, p, p, p, p, wb, wo)


def _ffn_kernel(x_ref, mod_ref, nw_ref, w1_ref, w2_ref, o_ref, h_sc):
    f = pl.program_id(1)

    @pl.when(f == 0)
    def _():
        x = x_ref[...]
        y = _rms(x, x.shape[-1]) * nw_ref[...]
        h_sc[...] = (y * (1.0 + mod_ref[4:5, :]) + mod_ref[3:4, :]).astype(BF)

    u = jnp.maximum(jnp.dot(h_sc[...], w1_ref[...], preferred_element_type=F32), 0.0)
    part = jnp.dot((u * u).astype(BF), w2_ref[...], preferred_element_type=F32)

    @pl.when(f == 0)
    def _():
        o_ref[...] = part

    @pl.when(f > 0)
    def _():
        o_ref[...] += part

    @pl.when(f == pl.num_programs(1) - 1)
    def _():
        o_ref[...] = x_ref[...] + mod_ref[5:6, :] * o_ref[...]


def _ffn(xs, mod, nw, w1, w2, layer, dims, rows, tm, tf):
    b, s, lc, nl, t = dims
    d = xs.shape[1]
    dff = w1.shape[2]
    return pl.pallas_call(
        _ffn_kernel,
        grid=(rows // tm, dff // tf),
        in_specs=[pl.BlockSpec((tm, d), lambda i, f: (i, 0)),
                  pl.BlockSpec((None, N_MOD, d), lambda i, f: (_mod_index(i, tm, nl, s, b), 0, 0)),
                  pl.BlockSpec((1, d), lambda i, f: (0, 0)),
                  pl.BlockSpec((None, d, tf), lambda i, f: (layer, 0, f)),
                  pl.BlockSpec((None, tf, d), lambda i, f: (layer, f, 0))],
        out_specs=pl.BlockSpec((tm, d), lambda i, f: (i, 0)),
        out_shape=jax.ShapeDtypeStruct((rows, d), F32),
        scratch_shapes=[pltpu.VMEM((tm, d), BF)],
        compiler_params=_cparams(("parallel", "arbitrary")),
        name="ffn",
    )(xs, mod, nw, w1, w2)


def _rope_tables(pos_per_lane, freq_per_lane, lo_lane, pad_rows, live_lane=None):
    ang = pos_per_lane * freq_per_lane[None, :]
    cos, sin = jnp.cos(ang), jnp.sin(ang)
    if live_lane is not None:
        cos = jnp.where(live_lane[None, :], cos, 1.0)
        sin = jnp.where(live_lane[None, :], sin, 0.0)
    s_lo = jnp.where(lo_lane[None, :], -sin, 0.0)
    s_hi = jnp.where(lo_lane[None, :], 0.0, sin)
    w = cos.shape[1]
    ident = (jnp.ones((pad_rows, w), F32), jnp.zeros((pad_rows, w), F32), jnp.zeros((pad_rows, w), F32))
    return tuple(jnp.concatenate([tb.astype(F32), idn], axis=0)
                 for tb, idn in zip((cos, s_lo, s_hi), ident))


def _tables(s, pad_rows):
    t = jnp.arange(s)
    row = (t // GRID_W).astype(F32)[:, None]
    col = (t % GRID_W).astype(F32)[:, None]
    tpos = t.astype(F32)[:, None]
    lane = jnp.arange(128)
    fa = ROPE_BASE ** (-(lane % 32).astype(F32) / 32)
    pos_a = jnp.where((lane < 64)[None, :], row, col)
    tab_a = _rope_tables(pos_a, fa, (lane % 64) < 32, pad_rows)
    lane = jnp.arange(256)
    fb = ROPE_BASE ** (-(lane % 32).astype(F32) / 32)
    tab_r = _rope_tables(jnp.broadcast_to(tpos, (s, 256)), fb, (lane % 64) < 32, pad_rows)
    lane = jnp.arange(128)
    fm = ROPE_BASE ** (-(lane % 16).astype(F32) / 16)
    pos_m = jnp.where((lane < 32)[None, :], row, col)
    tab_m = _rope_tables(pos_m, fm, (lane % 32) < 16, pad_rows, live_lane=lane < 64)
    return tab_a, tab_r, tab_m


def _pad_cols(w, n):
    return jnp.pad(w, [(0, 0)] * (w.ndim - 1) + [(0, n - w.shape[-1])])


def kernel(x, c, ctx, c_ctx, w_ada, b_ada, norm1_w, norm2_w, w_in, a_q_norm, a_k_norm, a_sink,
           r_decay, r_norm, s_conv_w, s_conv_b, s_a_log, s_dt_bias, s_d, s_norm,
           m_cq_norm, m_ckv_norm, m_w_uq, m_w_ukv, m_q_norm, m_k_norm,
           w_branch, w_o, w_ff1, w_ff2):
    b, s, d = x.shape
    lc = ctx.shape[1]
    depth = w_ada.shape[0]
    nl, ncx = b * s, b * lc
    t = nl + ncx
    dims = (b, s, lc, nl, t)
    tm_big = max(m for m in (1024, 512, 256) if s % m == 0 and ncx % m == 0)
    tm = max(m for m in (512, 256) if s % m == 0 and ncx % m == 0)
    tp = 256
    assert lc == 256 and s % 256 == 0 and b + 1 <= 8

    a_w = w_in[..., 0:1024]
    r_w = w_in[..., 1024:2560]
    s_w = w_in[..., 2560:4096]
    dt_w = w_in[..., 4096:4112]
    m_w = w_in[..., 4112:4816]
    g_w = w_in[..., 4816:]
    w_in_p = jnp.concatenate([_pad_cols(m_w, W_M), _pad_cols(dt_w, W_DT), a_w, r_w, s_w, g_w],
                             axis=-1).astype(BF)
    wq = _pad_cols(m_w_uq.reshape(depth, M_Q_RANK, M_HEADS, M_QK_DIM), 256)
    wq = wq.reshape(depth, M_Q_RANK, M_HEADS * 256).astype(BF)
    wkv = m_w_ukv.reshape(depth, M_KV_RANK, M_HEADS, M_NOPE + M_V)
    wkv = jnp.concatenate([wkv[..., :M_NOPE].reshape(depth, M_KV_RANK, M_HEADS * M_NOPE),
                           wkv[..., M_NOPE:].reshape(depth, M_KV_RANK, M_HEADS * M_V)],
                          axis=-1).astype(BF)
    wb = w_branch.astype(BF)
    wo = w_o.astype(BF)
    w1 = w_ff1.astype(BF)
    w2 = w_ff2.astype(BF)
    log_gamma = jnp.log1p(-jnp.exp(r_decay.astype(F32)))
    a_neg = _pad_cols(-jnp.exp(s_a_log.astype(F32)).reshape(depth, 1, 2 * S_HEADS), 128)
    dtb = _pad_cols(s_dt_bias.astype(F32).reshape(depth, 1, 2 * S_HEADS), 128)
    dsk = jnp.repeat(s_d.astype(F32), S_HEAD_DIM, axis=-1)[:, None, :]
    mqw = _pad_cols(m_q_norm, 256)[:, None, :]
    mkw = _pad_cols(m_k_norm, 256)[:, None, :]
    tab_a, tab_r, tab_m = _tables(s, tp)

    c8 = jnp.zeros((8, d), F32).at[:b].set(c).at[b].set(c_ctx)
    mod_all = _ada(c8, w_ada, b_ada).reshape(depth, 8, N_MOD, d)

    xs = jnp.concatenate([x.reshape(nl, d), ctx.reshape(ncx, d)], axis=0)
    for i in range(depth):
        need_ctx = i < depth - 1
        rows = t if need_ctx else nl
        mod = mod_all[i]
        p, pdt = _inproj(xs, mod, norm1_w[i][None, :], w_in_p, i, dims, tm_big, 1024)
        qa, ka, va = _aprep(p, tab_a, a_q_norm[i][None, :], a_k_norm[i][None, :], dims, tp)
        ya = _aattn(a_sink[i].astype(F32), qa, ka, va, dims, need_ctx)
        rf, rb = _retention(log_gamma[i], p, tab_r, dims)
        u = _conv(p, s_conv_w[i], s_conv_b[i][None, :], dims, tp)
        sf, sb = _ssd(a_neg[i], dtb[i], u, pdt, dims)
        yb, yc = _scan_finish(rf, rb, p, r_norm[i].reshape(1, -1), sf, sb, u, dsk[i],
                              s_norm[i][None, :], rows, tm)
        qm, km, vm = _mprep(p, tab_m, m_cq_norm[i][None, :],
                            m_ckv_norm[i][None, :], wq[i], wkv[i], mqw[i], mkw[i], dims, tp)
        yd = _mla(qm, km, vm, dims, need_ctx, 256, 256)
        xs = _merge(xs, mod, (ya, yb, yc, yd), p, wb, wo, i, dims, rows, tm, 512)
        xs = _ffn(xs, mod, norm2_w[i][None, :], w1, w2, i, dims, rows, tm, 1024)
    return xs.reshape(b, s, d)
```

```python
import functools
import math

import jax
import jax.numpy as jnp
import numpy as np
from jax import lax
from jax.experimental import pallas as pl
from jax.experimental.pallas import tpu as pltpu

GRID_W = 64
EPS = 1e-6
ROPE_BASE = 10000.0
NEG_INF = -1e30

A_HEADS, A_KV_HEADS, A_HEAD_DIM, A_BLOCK = 4, 2, 128, 128
R_HEADS, R_QK_DIM, R_V_DIM, R_CHUNK = 4, 64, 128, 128
S_HEADS, S_HEAD_DIM, S_GROUPS, S_STATE, S_CONV, S_CHUNK = 8, 64, 2, 128, 5, 128
M_HEADS, M_Q_RANK, M_KV_RANK, M_NOPE, M_ROPE, M_V = 4, 512, 128, 128, 64, 128
M_QK_DIM = M_NOPE + M_ROPE
N_BRANCH, N_MOD = 4, 6
BRANCH_W = 512

OFF_M, W_M = 0, 768
OFF_DT, W_DT = 768, 256
OFF_A, W_A = 1024, 1024
OFF_R, W_R = 2048, 1536
OFF_S, W_S = 3584, 1536
OFF_G = 5120

VMEM_LIMIT = 56 * 1024 * 1024

BF = jnp.bfloat16
F32 = jnp.float32


def _cparams(sem):
    return pltpu.CompilerParams(dimension_semantics=sem, vmem_limit_bytes=VMEM_LIMIT)


def _dot(a, b):
    return jnp.dot(a.astype(BF), b.astype(BF), preferred_element_type=F32)


def _dot_nt(a, b):
    return lax.dot_general(a.astype(BF), b.astype(BF), (((1,), (1,)), ((), ())),
                           preferred_element_type=F32)


def _silu(x):
    return x * (1.0 / (1.0 + jnp.exp(-x)))


def _rms(x, n):
    return x * lax.rsqrt(jnp.sum(x * x, axis=-1, keepdims=True) * (1.0 / n) + EPS)


def _rope(y, cos, s_lo, s_hi, shift):
    w = y.shape[-1]
    return y * cos + pltpu.roll(y, w - shift, 1) * s_lo + pltpu.roll(y, shift, 1) * s_hi


def _ada_kernel(c_ref, w_ref, b_ref, o_ref):
    o_ref[...] = _dot(_silu(c_ref[...]), w_ref[...]) + b_ref[...]


def _ada(c8, w_ada, b_ada):
    depth, d, n = w_ada.shape
    tn = 512
    return pl.pallas_call(
        _ada_kernel,
        grid=(depth, n // tn),
        in_specs=[pl.BlockSpec((8, d), lambda l, j: (0, 0)),
                  pl.BlockSpec((None, d, tn), lambda l, j: (l, 0, j)),
                  pl.BlockSpec((None, 1, tn), lambda l, j: (l, 0, j))],
        out_specs=pl.BlockSpec((None, 8, tn), lambda l, j: (l, 0, j)),
        out_shape=jax.ShapeDtypeStruct((depth, 8, n), F32),
        compiler_params=_cparams(("parallel", "parallel")),
        name="ada",
    )(c8, w_ada, b_ada.reshape(depth, 1, n))


def _inproj_kernel(x_ref, mod_ref, nw_ref, wm_ref, wg_ref, o_ref, dt_ref, h_sc, *, nm):
    j = pl.program_id(1)

    @pl.when(j == 0)
    def _():
        x = x_ref[...]
        y = _rms(x, x.shape[-1]) * nw_ref[...]
        h_sc[...] = (y * (1.0 + mod_ref[1:2, :]) + mod_ref[0:1, :]).astype(BF)
        r = jnp.dot(h_sc[...], wm_ref[...], preferred_element_type=F32)
        o_ref[...] = r.astype(o_ref.dtype)
        dt_ref[...] = r[:, OFF_DT:OFF_DT + W_DT]

    @pl.when(jnp.logical_and(j > 0, j < nm))
    def _():
        o_ref[...] = jnp.dot(h_sc[...], wm_ref[...],
                             preferred_element_type=F32).astype(o_ref.dtype)

    @pl.when(j >= nm)
    def _():
        o_ref[...] = jnp.dot(h_sc[...], wg_ref[...],
                             preferred_element_type=F32).astype(o_ref.dtype)


def _mod_index(i, tm, nl, s, b):
    return jnp.where(i < nl // tm, i // (s // tm), b)


def _inproj(xs, mod, nw, wm, wg, layer, dims, tm, tn):
    b, s, lc, nl, t = dims
    d = xs.shape[1]
    nm, ng = wm.shape[2] // tn, wg.shape[2] // tn
    assert OFF_DT + W_DT <= tn and wm.shape[2] == OFF_G
    return pl.pallas_call(
        functools.partial(_inproj_kernel, nm=nm),
        grid=(t // tm, nm + ng),
        in_specs=[pl.BlockSpec((tm, d), lambda i, j: (i, 0)),
                  pl.BlockSpec((None, N_MOD, d), lambda i, j: (_mod_index(i, tm, nl, s, b), 0, 0)),
                  pl.BlockSpec((1, d), lambda i, j: (0, 0)),
                  pl.BlockSpec((None, d, tn), lambda i, j: (layer, 0, jnp.minimum(j, nm - 1))),
                  pl.BlockSpec((None, d, tn), lambda i, j: (layer, 0, jnp.maximum(j - nm, 0)))],
        out_specs=[pl.BlockSpec((tm, tn), lambda i, j: (i, j)),
                   pl.BlockSpec((tm, W_DT), lambda i, j: (i, 0))],
        out_shape=[jax.ShapeDtypeStruct((t, (nm + ng) * tn), BF),
                   jax.ShapeDtypeStruct((t, W_DT), F32)],
        scratch_shapes=[pltpu.VMEM((tm, d), BF)],
        compiler_params=_cparams(("parallel", "arbitrary")),
        name="inproj",
    )(xs, mod, nw, wm, wg)


def _aprep_kernel(p_ref, cos_ref, slo_ref, shi_ref, qw_ref, kw_ref, q_ref, k_ref, v_ref):
    cos, slo, shi = cos_ref[...], slo_ref[...], shi_ref[...]
    scale = A_HEAD_DIM ** -0.5
    for h in range(A_HEADS):
        y = _rms(p_ref[:, h * 128:(h + 1) * 128].astype(F32), A_HEAD_DIM) * qw_ref[...]
        q_ref[:, h * 128:(h + 1) * 128] = (_rope(y, cos, slo, shi, 32) * scale).astype(BF)
    for h in range(A_KV_HEADS):
        y = _rms(p_ref[:, 512 + h * 128:512 + (h + 1) * 128].astype(F32), A_HEAD_DIM) * kw_ref[...]
        k_ref[:, h * 128:(h + 1) * 128] = _rope(y, cos, slo, shi, 32).astype(BF)
    for r in range(v_ref.shape[0]):
        v_ref[r] = p_ref[r * A_BLOCK:(r + 1) * A_BLOCK, 768:1024].astype(F32).T.astype(BF)


def _table_index(i, tm, nl, s):
    return jnp.where(i < nl // tm, i % (s // tm), s // tm)


def _aprep(p, tabs, qw, kw, dims, tm):
    b, s, lc, nl, t = dims
    tab_spec = pl.BlockSpec((tm, 128), lambda i: (_table_index(i, tm, nl, s), 0))
    return pl.pallas_call(
        _aprep_kernel,
        grid=(t // tm,),
        in_specs=[pl.BlockSpec((tm, W_A), lambda i: (i, OFF_A // W_A)),
                  tab_spec, tab_spec, tab_spec,
                  pl.BlockSpec((1, 128), lambda i: (0, 0)),
                  pl.BlockSpec((1, 128), lambda i: (0, 0))],
        out_specs=[pl.BlockSpec((tm, 512), lambda i: (i, 0)),
                   pl.BlockSpec((tm, 256), lambda i: (i, 0)),
                   pl.BlockSpec((tm // A_BLOCK, 256, A_BLOCK), lambda i: (i, 0, 0))],
        out_shape=[jax.ShapeDtypeStruct((t, 512), BF),
                   jax.ShapeDtypeStruct((t, 256), BF),
                   jax.ShapeDtypeStruct((t // A_BLOCK, 256, A_BLOCK), BF)],
        compiler_params=_cparams(("parallel",)),
        name="a_prep",
    )(p, *tabs, qw, kw)


def _aattn_kernel(sink_ref, q_ref, kp_ref, kc_ref, kn_ref, vp_ref, vc_ref, vn_ref,
                  kx_ref, vx_ref, o_ref, *, nb):
    n = pl.program_id(1)
    blk = A_BLOCK
    lc = kx_ref.shape[0]
    g = A_HEADS // A_KV_HEADS
    nk = 3 * blk + lc
    kj = lax.broadcasted_iota(jnp.int32, (nk, g * blk), 0)
    qi = lax.broadcasted_iota(jnp.int32, (nk, g * blk), 1) % blk
    lo_seq = jnp.where(n == 0, blk, 0)
    hi_seq = jnp.where(n < nb, jnp.where(n == nb - 1, 2 * blk - 1, 3 * blk - 1), -1)
    mask = ((kj >= jnp.maximum(qi, lo_seq)) & (kj <= jnp.minimum(qi + 2 * blk, hi_seq))) | (kj >= 3 * blk)
    lane = lax.broadcasted_iota(jnp.int32, (1, g * blk), 1)
    ss = []
    for hk in range(A_KV_HEADS):
        sl = slice(hk * 128, (hk + 1) * 128)
        q2 = jnp.concatenate([q_ref[:, (hk * g + gi) * 128:(hk * g + gi + 1) * 128]
                              for gi in range(g)], axis=0)
        kk = jnp.concatenate([kp_ref[:, sl], kc_ref[:, sl], kn_ref[:, sl], kx_ref[:, sl]], axis=0)
        ss.append(_dot_nt(kk, q2))
    for hk in range(A_KV_HEADS):
        sl = slice(hk * 128, (hk + 1) * 128)
        vvt = jnp.concatenate([vp_ref[sl, :], vc_ref[sl, :], vn_ref[sl, :]]
                              + [vx_ref[r, sl, :] for r in range(lc // blk)], axis=1)
        s = jnp.where(mask, ss[hk], NEG_INF)
        sink = jnp.where(lane < blk, sink_ref[hk * g], sink_ref[hk * g + 1])
        m = jnp.maximum(jnp.max(s, axis=0, keepdims=True), sink)
        p = jnp.exp(s - m)
        inv = 1.0 / (jnp.sum(p, axis=0, keepdims=True) + jnp.exp(sink - m))
        o = (jnp.dot(vvt, p.astype(BF), preferred_element_type=F32) * inv).T
        for gi in range(g):
            o_ref[:, (hk * g + gi) * 128:(hk * g + gi + 1) * 128] = (
                o[gi * blk:(gi + 1) * blk]).astype(o_ref.dtype)


def _aattn(sink, qa, ka, va, dims, need_ctx):
    b, s, lc, nl, t = dims
    blk = A_BLOCK
    nb = s // blk
    ncb = lc // blk
    steps = nb + (ncb if need_ctx else 0)
    rows = nl + (b * lc if need_ctx else 0)

    def qrow(bi, n):
        return jnp.where(n < nb, bi * nb + n, nl // blk + bi * ncb + (n - nb))

    def krow(off):
        def f(bi, n):
            return (bi * nb + jnp.clip(n + off, 0, nb - 1), 0)
        return f

    kspec = [pl.BlockSpec((blk, 256), krow(o)) for o in (-1, 0, 1)]
    vspec = [pl.BlockSpec((None, 256, blk), lambda bi, n, o=o: (*krow(o)(bi, n), 0))
             for o in (-1, 0, 1)]
    xspec = pl.BlockSpec((lc, 256), lambda bi, n: (nl // lc + bi, 0))
    vxspec = pl.BlockSpec((ncb, 256, blk), lambda bi, n: (nl // lc + bi, 0, 0))
    return pl.pallas_call(
        functools.partial(_aattn_kernel, nb=nb),
        grid=(b, steps),
        in_specs=[pl.BlockSpec(memory_space=pltpu.SMEM),
                  pl.BlockSpec((blk, 512), lambda bi, n: (qrow(bi, n), 0)),
                  *kspec, *vspec, xspec, vxspec],
        out_specs=pl.BlockSpec((blk, 512), lambda bi, n: (qrow(bi, n), 0)),
        out_shape=jax.ShapeDtypeStruct((rows, 512), BF),
        compiler_params=_cparams(("parallel", "arbitrary")),
        name="a_attn",
    )(sink, qa, ka, ka, ka, va, va, va, ka, va)


def _scan_slot(t, ncc, nlc, bwd):
    if not bwd:
        return t
    return jnp.where(t < ncc, ncc - 1 - t, ncc + (nlc - 1 - (t - ncc)))


def _scan_row(bi, t, ncc, nlc, nl, chunk, bwd):
    slot = _scan_slot(t, ncc, nlc, bwd)
    return jnp.where(slot < ncc, nl // chunk + bi * ncc + slot, bi * nlc + (slot - ncc))


def _ret_kernel(lg_ref, qf_ref, kf_ref, vf_ref, qb_ref, kb_ref, vb_ref,
                cosf_ref, slof_ref, shif_ref, cosb_ref, slob_ref, shib_ref,
                of_ref, ob_ref, st_sc, dm_sc, eq_sc, ek_sc):
    c = R_CHUNK
    hd = R_QK_DIM
    nh = R_HEADS

    def per_head(idx, vals):
        out = vals[nh - 1]
        for h in range(nh - 2, -1, -1):
            out = jnp.where(idx == h, vals[h], out)
        return out

    @pl.when(pl.program_id(1) == 0)
    def _():
        st_sc[...] = jnp.zeros_like(st_sc)
        ii = lax.broadcasted_iota(jnp.int32, (c, c), 0)
        jj = lax.broadcasted_iota(jnp.int32, (c, c), 1)
        col = lax.broadcasted_iota(jnp.int32, (c, 1), 0)
        row = lax.broadcasted_iota(jnp.int32, (1, c), 1)
        lane_head = lax.broadcasted_iota(jnp.int32, (1, nh * hd), 1) // hd
        sub_head = lax.broadcasted_iota(jnp.int32, (nh * hd, 1), 0) // hd
        for d in range(2):
            dd = ((jj - ii) if d else (ii - jj)).astype(F32)
            pcol = ((c - 1 - col) if d else col).astype(F32)
            prow = ((c - 1 - row) if d else row).astype(F32)
            lgs = [lg_ref[d, h] for h in range(nh)]
            eq_sc[d] = jnp.exp(per_head(lane_head, lgs) * (pcol + 1.0))
            ek_sc[d] = jnp.exp(per_head(sub_head, lgs) * (c - 1.0 - prow))
            for h in range(nh):
                dm_sc[d, h] = jnp.where(dd >= 0, jnp.exp(lgs[h] * jnp.maximum(dd, 0.0)), 0.0)

    lane_head = lax.broadcasted_iota(jnp.int32, (1, 2 * hd), 1) // hd
    dirs = []
    for d, (q_ref, k_ref, v_ref, cos_ref, slo_ref, shi_ref) in enumerate((
            (qf_ref, kf_ref, vf_ref, cosf_ref, slof_ref, shif_ref),
            (qb_ref, kb_ref, vb_ref, cosb_ref, slob_ref, shib_ref))):
        cos, slo, shi = cos_ref[...], slo_ref[...], shi_ref[...]
        q = _rope(q_ref[...].astype(F32), cos, slo, shi, 32)
        k = _rope(k_ref[...].astype(F32), cos, slo, shi, 32) * (hd ** -0.5)
        kt = k.T
        dirs.append((q, q * eq_sc[d], kt, kt * ek_sc[d], v_ref[...], st_sc[d]))
    outs = ([], [])
    new_states = ([], [])
    for h in range(nh):
        pr = h // 2
        for d, (q, qd, kt, kdt, v, state) in enumerate(dirs):
            sel = lane_head == (h % 2)
            qh = jnp.where(sel, q[:, pr * 128:(pr + 1) * 128], 0.0)
            qdh = jnp.where(sel, qd[:, pr * 128:(pr + 1) * 128], 0.0)
            vh = v[:, h * 128:(h + 1) * 128]
            s = _dot(qh, kt[pr * 128:(pr + 1) * 128, :]) * dm_sc[d, h]
            inter = _dot(qdh, state[pr * 128:(pr + 1) * 128, :])
            outs[d].append(_dot(s, vh) + inter)
            chunk_decay = jnp.exp(lg_ref[d, h] * jnp.full((1, R_V_DIM), float(c), F32))
            new_states[d].append(chunk_decay * state[h * hd:(h + 1) * hd, :]
                                 + _dot(kdt[h * hd:(h + 1) * hd, :], vh))
    for d, o_ref in enumerate((of_ref, ob_ref)):
        st_sc[d] = jnp.concatenate(new_states[d], axis=0)
        o_ref[...] = jnp.concatenate(outs[d], axis=1).astype(o_ref.dtype)


def _retention(lg, p, tabs, dims):
    b, s, lc, nl, t = dims
    c = R_CHUNK
    ncc, nlc = lc // c, s // c

    def row(bwd):
        return lambda bi, st: _scan_row(bi, st, ncc, nlc, nl, c, bwd)

    def pspec(width, col_off, bwd):
        r = row(bwd)
        return pl.BlockSpec((c, width), lambda bi, st: (r(bi, st), col_off // width))

    def tab_spec(bwd):
        def trow(bi, st):
            slot = _scan_slot(st, ncc, nlc, bwd)
            return (jnp.where(slot < ncc, nlc, slot - ncc), 0)
        return pl.BlockSpec((c, 256), trow)

    def qkv(bwd):
        return [pspec(256, OFF_R, bwd), pspec(256, OFF_R + 256, bwd), pspec(512, OFF_R + 512, bwd)]

    def ospec(bwd):
        r = row(bwd)
        return pl.BlockSpec((c, 512), lambda bi, st: (r(bi, st), 0))

    return pl.pallas_call(
        _ret_kernel,
        grid=(b, ncc + nlc),
        in_specs=[pl.BlockSpec(memory_space=pltpu.SMEM), *qkv(False), *qkv(True),
                  *[tab_spec(False)] * 3, *[tab_spec(True)] * 3],
        out_specs=[ospec(False), ospec(True)],
        out_shape=[jax.ShapeDtypeStruct((t, 512), BF)] * 2,
        scratch_shapes=[pltpu.VMEM((2, R_HEADS * R_QK_DIM, R_V_DIM), F32),
                        pltpu.VMEM((2, R_HEADS, c, c), F32),
                        pltpu.VMEM((2, c, R_HEADS * R_QK_DIM), F32),
                        pltpu.VMEM((2, R_HEADS * R_QK_DIM, c), F32)],
        compiler_params=_cparams(("parallel", "arbitrary")),
        name="retention",
    )(lg, p, p, p, p, p, p, *tabs, *tabs)


def _conv_kernel(x_ref, xp_ref, xn_ref, w_ref, b_ref, o_ref, e_sc, *, tm, blocks_per_seg, n_lat_blocks):
    i = pl.program_id(0)
    is_lat = i < n_lat_blocks
    first = jnp.logical_or(jnp.logical_not(is_lat), i % blocks_per_seg == 0)
    last = jnp.logical_or(jnp.logical_not(is_lat), i % blocks_per_seg == blocks_per_seg - 1)
    hr = HALO_ROWS
    e_sc[0:hr, :] = jnp.where(first, 0.0, xp_ref[...].astype(F32))
    e_sc[hr:hr + tm, :] = x_ref[...].astype(F32)
    e_sc[hr + tm:2 * hr + tm, :] = jnp.where(last, 0.0, xn_ref[...].astype(F32))
    pad = (S_CONV - 1) // 2
    acc = b_ref[...] + w_ref[0:1, :] * e_sc[hr - pad:hr - pad + tm, :]
    for kk in range(1, S_CONV):
        acc = acc + w_ref[kk:kk + 1, :] * e_sc[hr - pad + kk:hr - pad + kk + tm, :]
    o_ref[...] = _silu(acc)


HALO_ROWS = 16


def _conv(p, w, bias, dims, tm):
    b, s, lc, nl, t = dims
    ch = w.shape[1]
    r8 = tm // HALO_ROWS
    nblk = t // tm
    xoff = (OFF_S + 512) // ch
    assert (OFF_S + 512) % ch == 0
    assert lc == tm
    return pl.pallas_call(
        functools.partial(_conv_kernel, tm=tm, blocks_per_seg=s // tm, n_lat_blocks=nl // tm),
        grid=(nblk,),
        in_specs=[pl.BlockSpec((tm, ch), lambda i: (i, xoff)),
                  pl.BlockSpec((HALO_ROWS, ch), lambda i: (jnp.maximum(i * r8 - 1, 0), xoff)),
                  pl.BlockSpec((HALO_ROWS, ch),
                               lambda i: (jnp.minimum((i + 1) * r8, nblk * r8 - 1), xoff)),
                  pl.BlockSpec((S_CONV, ch), lambda i: (0, 0)),
                  pl.BlockSpec((1, ch), lambda i: (0, 0))],
        out_specs=pl.BlockSpec((tm, ch), lambda i: (i, 0)),
        out_shape=jax.ShapeDtypeStruct((t, ch), F32),
        scratch_shapes=[pltpu.VMEM((tm + 2 * HALO_ROWS, ch), F32)],
        compiler_params=_cparams(("parallel",)),
        name="s_conv",
    )(p, p, p, w, bias)


def _ssd_kernel(a_ref, dtb_ref, uf_ref, dtf_ref, ub_ref, dtbw_ref, yf_ref, yb_ref, h_sc):
    qn = S_CHUNK

    @pl.when(pl.program_id(1) == 0)
    def _():
        h_sc[...] = jnp.zeros_like(h_sc)

    ii = lax.broadcasted_iota(jnp.int32, (qn, qn), 0)
    jj = lax.broadcasted_iota(jnp.int32, (qn, qn), 1)
    tri = jnp.where(jj <= ii, 1.0, 0.0).astype(BF)
    hg = S_HEADS // S_GROUPS
    dirs = []
    for d, (u_ref, dt_ref) in enumerate(((uf_ref, dtf_ref), (ub_ref, dtbw_ref))):
        xr = dt_ref[...] + dtb_ref[...]
        dt = jnp.maximum(xr, 0.0) + jnp.log1p(jnp.exp(-jnp.abs(xr)))
        dta = dt * a_ref[...]
        hi = dta.astype(BF)
        lo = (dta - hi.astype(F32)).astype(BF)
        lo2 = (dta - hi.astype(F32) - lo.astype(F32)).astype(BF)
        pfx = (jnp.dot(tri, hi, preferred_element_type=F32)
               + jnp.dot(tri, lo, preferred_element_type=F32)
               + jnp.dot(tri, lo2, preferred_element_type=F32))
        total = pfx[qn - 1:qn, :]
        cum = (total - pfx + dta) if d else pfx
        mask = ((jj - ii) if d else (ii - jj)) >= 0
        dirs.append((u_ref, dt, cum, cum.T, total, mask, h_sc[d]))
    lane_lo = lax.broadcasted_iota(jnp.int32, (1, 128), 1) < S_HEAD_DIM
    ys = ([], [])
    hs = ([], [])
    for g in range(S_GROUPS):
        grp = []
        for d, (u_ref, dt, cum, cum_t, total, mask, hprev) in enumerate(dirs):
            bg = u_ref[:, 512 + g * 128:512 + (g + 1) * 128]
            cg = u_ref[:, 768 + g * 128:768 + (g + 1) * 128]
            cb = _dot_nt(cg, bg)
            yoff = _dot(cg, hprev[:, g * hg * 64:(g + 1) * hg * 64])
            grp.append((cb, bg.T, yoff))
        for pr in range(hg // 2):
            h0 = g * hg + 2 * pr
            for d, (u_ref, dt, cum, cum_t, total, mask, hprev) in enumerate(dirs):
                cb, bgt, yoff = grp[d]
                ws, cbs, dbs, tots = [], [], [], []
                for h in (h0, h0 + 1):
                    ln = d * S_HEADS + h
                    ccol_b = jnp.broadcast_to(cum[:, ln:ln + 1], (qn, 128))
                    ws.append(cb * jnp.exp(jnp.where(mask, ccol_b - cum_t[ln:ln + 1, :], NEG_INF)))
                    cbs.append(ccol_b)
                    dbs.append(jnp.broadcast_to(dt[:, ln:ln + 1], (qn, 128)))
                    tots.append(total[:, ln:ln + 1])
                csel = jnp.where(lane_lo, cbs[0], cbs[1])
                tsel = jnp.where(lane_lo, tots[0], tots[1])
                xdt = u_ref[:, h0 * 64:h0 * 64 + 128] * jnp.where(lane_lo, dbs[0], dbs[1])
                rhs = jnp.concatenate([jnp.where(lane_lo, xdt, 0.0), jnp.where(lane_lo, 0.0, xdt)],
                                      axis=0)
                y = (_dot(jnp.concatenate(ws, axis=1), rhs)
                     + yoff[:, pr * 128:(pr + 1) * 128] * jnp.exp(csel))
                hnew = (jnp.exp(tsel) * hprev[:, h0 * 64:h0 * 64 + 128]
                        + _dot(bgt, xdt * jnp.exp(tsel - csel)))
                ys[d].append(y)
                hs[d].append(hnew)
    for d, y_ref in enumerate((yf_ref, yb_ref)):
        h_sc[d] = jnp.concatenate(hs[d], axis=1)
        y_ref[...] = jnp.concatenate(ys[d], axis=1).astype(y_ref.dtype)


def _ssd(a_flat, dtb_flat, u, pdt, dims):
    b, s, lc, nl, t = dims
    c = S_CHUNK
    ncc, nlc = lc // c, s // c

    def spec(width, bwd):
        return pl.BlockSpec(
            (c, width), lambda bi, st: (_scan_row(bi, st, ncc, nlc, nl, c, bwd), 0))

    const = pl.BlockSpec((1, 128), lambda bi, st: (0, 0))
    return pl.pallas_call(
        _ssd_kernel,
        grid=(b, ncc + nlc),
        in_specs=[const, const, spec(1024, False), spec(128, False),
                  spec(1024, True), spec(128, True)],
        out_specs=[spec(512, False), spec(512, True)],
        out_shape=[jax.ShapeDtypeStruct((t, 512), BF)] * 2,
        scratch_shapes=[pltpu.VMEM((2, S_STATE, S_HEADS * S_HEAD_DIM), F32)],
        compiler_params=_cparams(("parallel", "arbitrary")),
        name="ssd",
    )(a_flat, dtb_flat, u, pdt, u, pdt)


def _scan_finish_kernel(rf_ref, rb_ref, g_ref, rnw_ref, sf_ref, sb_ref, x_ref, z_ref,
                        dsk_ref, snw_ref, yb_ref, yc_ref):
    tot = rf_ref[...].astype(F32) + rb_ref[...].astype(F32)
    ys = [_rms(tot[:, h * 128:(h + 1) * 128], R_V_DIM) for h in range(R_HEADS)]
    y = jnp.concatenate(ys, axis=1) * rnw_ref[...]
    yb_ref[...] = (y * _silu(g_ref[...].astype(F32))).astype(yb_ref.dtype)
    yt = (sf_ref[...].astype(F32) + sb_ref[...].astype(F32)
          + dsk_ref[...] * x_ref[...].astype(F32)) * _silu(z_ref[...].astype(F32))
    yc_ref[...] = (_rms(yt, yt.shape[-1]) * snw_ref[...]).astype(yc_ref.dtype)


def _scan_finish(rf, rb, p, rnw, sf, sb, u, dsk, snw, rows, tm):
    blk = lambda col: pl.BlockSpec((tm, 512), lambda i: (i, col))
    const = pl.BlockSpec((1, 512), lambda i: (0, 0))
    return pl.pallas_call(
        _scan_finish_kernel,
        grid=(rows // tm,),
        in_specs=[blk(0), blk(0), blk((OFF_R + 1024) // 512), const,
                  blk(0), blk(0), blk(0), blk(OFF_S // 512), const, const],
        out_specs=[blk(0), blk(0)],
        out_shape=[jax.ShapeDtypeStruct((rows, 512), BF)] * 2,
        compiler_params=_cparams(("parallel",)),
        name="scan_finish",
    )(rf, rb, p, rnw, sf, sb, u, p, dsk, snw)


def _mprep_kernel(p_ref, cos_ref, slo_ref, shi_ref, cqw_ref, ckvw_ref, wq_ref, wkv_ref,
                  qw_ref, kw_ref, q_ref, k_ref, v_ref):
    cos, slo, shi = cos_ref[...], slo_ref[...], shi_ref[...]
    scale = M_QK_DIM ** -0.5 * math.log2(math.e)
    cq = _rms(p_ref[:, 0:512].astype(F32), M_Q_RANK) * cqw_ref[...]
    q = _dot(cq, wq_ref[...])
    ckv = _rms(p_ref[:, 512:640].astype(F32), M_KV_RANK) * ckvw_ref[...]
    kv = _dot(ckv, wkv_ref[...])
    kr = p_ref[:, 640:768].astype(F32)
    kr_ss = jnp.sum(kr * kr, axis=-1, keepdims=True)
    for h in range(M_HEADS):
        c0 = q[:, h * 256:h * 256 + 128]
        c1 = q[:, h * 256 + 128:h * 256 + 256]
        r = lax.rsqrt((jnp.sum(c0 * c0, axis=-1, keepdims=True)
                       + jnp.sum(c1 * c1, axis=-1, keepdims=True)) * (1.0 / M_QK_DIM) + EPS)
        q_ref[:, h * 256:h * 256 + 128] = (c0 * r * qw_ref[:, 0:128] * scale).astype(BF)
        q_ref[:, h * 256 + 128:h * 256 + 256] = (
            _rope(c1 * r * qw_ref[:, 128:256], cos, slo, shi, 16) * scale).astype(BF)
        k0 = kv[:, h * 128:(h + 1) * 128]
        r = lax.rsqrt((jnp.sum(k0 * k0, axis=-1, keepdims=True) + kr_ss) * (1.0 / M_QK_DIM) + EPS)
        k_ref[:, h * 256:h * 256 + 128] = (k0 * r * kw_ref[:, 0:128]).astype(BF)
        k_ref[:, h * 256 + 128:h * 256 + 256] = (
            _rope(kr * r * kw_ref[:, 128:256], cos, slo, shi, 16)).astype(BF)
    v_ref[...] = kv[:, 512:1024].T.astype(BF)


def _mprep(p, tabs, cqw, ckvw, wq, wkv, qw, kw, dims, tm):
    b, s, lc, nl, t = dims
    tab_spec = pl.BlockSpec((tm, 128), lambda i: (_table_index(i, tm, nl, s), 0))
    const = lambda shape: pl.BlockSpec(shape, lambda i: (0, 0))
    return pl.pallas_call(
        _mprep_kernel,
        grid=(t // tm,),
        in_specs=[pl.BlockSpec((tm, W_M), lambda i: (i, OFF_M // W_M)),
                  tab_spec, tab_spec, tab_spec,
                  const((1, 512)), const((1, 128)), const((512, 1024)), const((128, 1024)),
                  const((1, 256)), const((1, 256))],
        out_specs=[pl.BlockSpec((tm, 1024), lambda i: (i, 0)),
                   pl.BlockSpec((tm, 1024), lambda i: (i, 0)),
                   pl.BlockSpec((None, 512, tm), lambda i: (i, 0, 0))],
        out_shape=[jax.ShapeDtypeStruct((t, 1024), BF),
                   jax.ShapeDtypeStruct((t, 1024), BF),
                   jax.ShapeDtypeStruct((t // tm, 512, tm), BF)],
        compiler_params=_cparams(("parallel",)),
        name="m_prep",
    )(p, *tabs, cqw, ckvw, wq, wkv, qw, kw)


def _mla_kernel(q_ref, kx_ref, vx_ref, kl_ref, vl_ref, o_ref, acc_sc, s_sc, *, nq, tk):
    qi = pl.program_id(1)
    n_tiles = kl_ref.shape[0] // tk

    def scores(h, k_tile):
        return _dot_nt(k_tile[:, h * 256:(h + 1) * 256], q_ref[:, h * 256:(h + 1) * 256])

    def pv(h, vt_tile, p):
        return jnp.dot(vt_tile[h * 128:(h + 1) * 128, :], p.astype(BF), preferred_element_type=F32)

    def k_tile(kt):
        return kl_ref.at[pl.ds(pl.multiple_of(kt * tk, tk), tk), :]

    s_ctx = [scores(h, kx_ref) for h in range(M_HEADS)]
    for h in range(M_HEADS):
        s_sc[h] = scores(h, k_tile(0))
    ms, ls = [], []
    for h in range(M_HEADS):
        s = s_ctx[h]
        m0 = jnp.max(s, axis=0, keepdims=True)
        p = jnp.exp2(s - m0)
        ms.append(m0)
        ls.append(jnp.sum(p, axis=0, keepdims=True))
        acc_sc[h] = pv(h, vx_ref, p)

    def body(kt, carry):
        ms, ls = carry
        nxt = k_tile(jnp.minimum(kt + 1, n_tiles - 1))
        vt_tile = vl_ref.at[kt]
        ms_new, ls_new = [], []
        s_cur = [s_sc[h] for h in range(M_HEADS)]
        s_nxt = [scores(h, nxt) for h in range(M_HEADS)]
        for h in range(M_HEADS):
            s = s_cur[h]
            m_new = jnp.maximum(ms[h], jnp.max(s, axis=0, keepdims=True))
            alpha = jnp.exp2(ms[h] - m_new)
            p = jnp.exp2(s - m_new)
            ms_new.append(m_new)
            ls_new.append(alpha * ls[h] + jnp.sum(p, axis=0, keepdims=True))
            acc_sc[h] = alpha * acc_sc[h] + pv(h, vt_tile, p)
        for h in range(M_HEADS):
            s_sc[h] = s_nxt[h]
        return tuple(ms_new), tuple(ls_new)

    def finish(ls):
        for h in range(M_HEADS):
            o = acc_sc[h] * (1.0 / ls[h])
            o_ref[:, h * 128:(h + 1) * 128] = o.T.astype(o_ref.dtype)

    @pl.when(qi < nq)
    def _():
        finish(lax.fori_loop(0, n_tiles, body, (tuple(ms), tuple(ls)), unroll=4)[1])

    @pl.when(qi >= nq)
    def _():
        finish(ls)


def _mla(qm, km, vmt, dims, need_ctx, tq, tk):
    b, s, lc, nl, t = dims
    nq = s // tq
    assert lc == tq and lc == tk and vmt.shape[2] == tk
    steps = nq + (1 if need_ctx else 0)
    rows = nl + (b * lc if need_ctx else 0)

    def qrow(bi, qi):
        return jnp.where(qi < nq, bi * nq + qi, nl // tq + bi)

    return pl.pallas_call(
        functools.partial(_mla_kernel, nq=nq, tk=tk),
        grid=(b, steps),
        in_specs=[pl.BlockSpec((tq, 1024), lambda bi, qi: (qrow(bi, qi), 0)),
                  pl.BlockSpec((lc, 1024), lambda bi, qi: (nl // lc + bi, 0)),
                  pl.BlockSpec((None, 512, tk), lambda bi, qi: (nl // tk + bi, 0, 0)),
                  pl.BlockSpec((s, 1024), lambda bi, qi: (bi, 0)),
                  pl.BlockSpec((s // tk, 512, tk), lambda bi, qi: (bi, 0, 0))],
        out_specs=pl.BlockSpec((tq, 512), lambda bi, qi: (qrow(bi, qi), 0)),
        out_shape=jax.ShapeDtypeStruct((rows, 512), BF),
        scratch_shapes=[pltpu.VMEM((M_HEADS, M_V, tq), F32),
                        pltpu.VMEM((M_HEADS, tk, tq), F32)],
        compiler_params=_cparams(("parallel", "arbitrary")),
        name="mla",
    )(qm, km, vmt, km, vmt)


def _merge_kernel(x_ref, mod_ref, ya_ref, yb_ref, yc_ref, yd_ref, g0_ref, g1_ref, g2_ref, g3_ref,
                  wb_ref, wo_ref, o_ref, acc_sc, *, tn):
    j = pl.program_id(1)
    acc = None
    for kk, (y_ref, g_ref) in enumerate(((ya_ref, g0_ref), (yb_ref, g1_ref),
                                         (yc_ref, g2_ref), (yd_ref, g3_ref))):
        gate = 1.0 / (1.0 + jnp.exp(-g_ref[...].astype(F32)))
        term = gate * jnp.dot(y_ref[...], wb_ref[kk], preferred_element_type=F32)
        acc = term if acc is None else acc + term
    acc_sc[:, pl.ds(pl.multiple_of(j * tn, tn), tn)] = acc.astype(BF)

    @pl.when(j == pl.num_programs(1) - 1)
    def _():
        o_ref[...] = x_ref[...] + mod_ref[2:3, :] * jnp.dot(
            acc_sc[...], wo_ref[...], preferred_element_type=F32)


def _merge(xs, mod, ys, p, wb, wo, layer, dims, rows, tm, tn):
    b, s, lc, nl, t = dims
    d = xs.shape[1]
    nj = d // tn
    goff = OFF_G // tn
    yspec = pl.BlockSpec((tm, BRANCH_W), lambda i, j: (i, 0))
    gspecs = [pl.BlockSpec((tm, tn), functools.partial(lambda i, j, kk: (i, goff + kk * nj + j), kk=kk))
              for kk in range(N_BRANCH)]
    return pl.pallas_call(
        functools.partial(_merge_kernel, tn=tn),
        grid=(rows // tm, nj),
        in_specs=[pl.BlockSpec((tm, d), lambda i, j: (i, 0)),
                  pl.BlockSpec((None, N_MOD, d), lambda i, j: (_mod_index(i, tm, nl, s, b), 0, 0)),
                  yspec, yspec, yspec, yspec, *gspecs,
                  pl.BlockSpec((None, N_BRANCH, BRANCH_W, tn), lambda i, j: (layer, 0, 0, j)),
                  pl.BlockSpec((None, d, d), lambda i, j: (layer, 0, 0))],
        out_specs=pl.BlockSpec((tm, d), lambda i, j: (i, 0)),
        out_shape=jax.ShapeDtypeStruct((rows, d), F32),
        scratch_shapes=[pltpu.VMEM((tm, d), BF)],
        compiler_params=_cparams(("parallel", "arbitrary")),
        name="merge",
    )(xs, mod, *ys, p, p, p, p, wb, wo)


def _ffn_kernel(x_ref, mod_ref, nw_ref, w1_ref, w2_ref, o_ref, h_sc):
    f = pl.program_id(1)

    @pl.when(f == 0)
    def _():
        x = x_ref[...]
        y = _rms(x, x.shape[-1]) * nw_ref[...]
        h_sc[...] = (y * (1.0 + mod_ref[4:5, :]) + mod_ref[3:4, :]).astype(BF)

    u = jnp.maximum(jnp.dot(h_sc[...], w1_ref[...], preferred_element_type=F32), 0.0)
    part = jnp.dot((u * u).astype(BF), w2_ref[...], preferred_element_type=F32)

    @pl.when(f == 0)
    def _():
        o_ref[...] = part

    @pl.when(f > 0)
    def _():
        o_ref[...] += part

    @pl.when(f == pl.num_programs(1) - 1)
    def _():
        o_ref[...] = x_ref[...] + mod_ref[5:6, :] * o_ref[...]


def _ffn(xs, mod, nw, w1, w2, layer, dims, rows, tm, tf):
    b, s, lc, nl, t = dims
    d = xs.shape[1]
    dff = w1.shape[2]
    return pl.pallas_call(
        _ffn_kernel,
        grid=(rows // tm, dff // tf),
        in_specs=[pl.BlockSpec((tm, d), lambda i, f: (i, 0)),
                  pl.BlockSpec((None, N_MOD, d), lambda i, f: (_mod_index(i, tm, nl, s, b), 0, 0)),
                  pl.BlockSpec((1, d), lambda i, f: (0, 0)),
                  pl.BlockSpec((None, d, tf), lambda i, f: (layer, 0, f)),
                  pl.BlockSpec((None, tf, d), lambda i, f: (layer, f, 0))],
        out_specs=pl.BlockSpec((tm, d), lambda i, f: (i, 0)),
        out_shape=jax.ShapeDtypeStruct((rows, d), F32),
        scratch_shapes=[pltpu.VMEM((tm, d), BF)],
        compiler_params=_cparams(("parallel", "arbitrary")),
        name="ffn",
    )(xs, mod, nw, w1, w2)


def _rope_tables(pos_per_lane, freq_per_lane, lo_lane, pad_rows, live_lane=None):
    ang = pos_per_lane * freq_per_lane[None, :]
    cos, sin = jnp.cos(ang), jnp.sin(ang)
    if live_lane is not None:
        cos = jnp.where(live_lane[None, :], cos, 1.0)
        sin = jnp.where(live_lane[None, :], sin, 0.0)
    s_lo = jnp.where(lo_lane[None, :], -sin, 0.0)
    s_hi = jnp.where(lo_lane[None, :], 0.0, sin)
    w = cos.shape[1]
    ident = (jnp.ones((pad_rows, w), F32), jnp.zeros((pad_rows, w), F32), jnp.zeros((pad_rows, w), F32))
    return tuple(jnp.concatenate([tb.astype(F32), idn], axis=0)
                 for tb, idn in zip((cos, s_lo, s_hi), ident))


def _tables(s, pad_rows):
    t = jnp.arange(s)
    row = (t // GRID_W).astype(F32)[:, None]
    col = (t % GRID_W).astype(F32)[:, None]
    tpos = t.astype(F32)[:, None]
    lane = jnp.arange(128)
    fa = ROPE_BASE ** (-(lane % 32).astype(F32) / 32)
    pos_a = jnp.where((lane < 64)[None, :], row, col)
    tab_a = _rope_tables(pos_a, fa, (lane % 64) < 32, pad_rows)
    lane = jnp.arange(256)
    fb = ROPE_BASE ** (-(lane % 32).astype(F32) / 32)
    tab_r = _rope_tables(jnp.broadcast_to(tpos, (s, 256)), fb, (lane % 64) < 32, pad_rows)
    lane = jnp.arange(128)
    fm = ROPE_BASE ** (-(lane % 16).astype(F32) / 16)
    pos_m = jnp.where((lane < 32)[None, :], row, col)
    tab_m = _rope_tables(pos_m, fm, (lane % 32) < 16, pad_rows, live_lane=lane < 64)
    return tab_a, tab_r, tab_m


def _pad_cols(w, n):
    return jnp.pad(w, [(0, 0)] * (w.ndim - 1) + [(0, n - w.shape[-1])])


def kernel(x, c, ctx, c_ctx, w_ada, b_ada, norm1_w, norm2_w, w_in, a_q_norm, a_k_norm, a_sink,
           r_decay, r_norm, s_conv_w, s_conv_b, s_a_log, s_dt_bias, s_d, s_norm,
           m_cq_norm, m_ckv_norm, m_w_uq, m_w_ukv, m_q_norm, m_k_norm,
           w_branch, w_o, w_ff1, w_ff2):
    b, s, d = x.shape
    lc = ctx.shape[1]
    depth = w_ada.shape[0]
    nl, ncx = b * s, b * lc
    t = nl + ncx
    dims = (b, s, lc, nl, t)
    tm_big = max(m for m in (1024, 512, 256) if s % m == 0 and ncx % m == 0)
    tm = max(m for m in (512, 256) if s % m == 0 and ncx % m == 0)
    tp = 256
    assert lc == 256 and s % 256 == 0 and b + 1 <= 8

    ars_w = w_in[..., 0:4096].astype(BF)
    dt_w = w_in[..., 4096:4112].astype(BF)
    m_w = w_in[..., 4112:4816].astype(BF)
    w_mix = jnp.concatenate([_pad_cols(m_w, W_M), _pad_cols(dt_w, W_DT), ars_w], axis=-1)
    w_gate = w_in[..., 4816:].astype(BF)
    wq = _pad_cols(m_w_uq.reshape(depth, M_Q_RANK, M_HEADS, M_QK_DIM), 256)
    wq = wq.reshape(depth, M_Q_RANK, M_HEADS * 256).astype(BF)
    wkv = m_w_ukv.reshape(depth, M_KV_RANK, M_HEADS, M_NOPE + M_V)
    wkv = jnp.concatenate([wkv[..., :M_NOPE].reshape(depth, M_KV_RANK, M_HEADS * M_NOPE),
                           wkv[..., M_NOPE:].reshape(depth, M_KV_RANK, M_HEADS * M_V)],
                          axis=-1).astype(BF)
    wb = w_branch.astype(BF)
    wo = w_o.astype(BF)
    w1 = w_ff1.astype(BF)
    w2 = w_ff2.astype(BF)
    log_gamma = jnp.log1p(-jnp.exp(r_decay.astype(F32)))
    a_neg = _pad_cols(-jnp.exp(s_a_log.astype(F32)).reshape(depth, 1, 2 * S_HEADS), 128)
    dtb = _pad_cols(s_dt_bias.astype(F32).reshape(depth, 1, 2 * S_HEADS), 128)
    dsk = jnp.repeat(s_d.astype(F32), S_HEAD_DIM, axis=-1)[:, None, :]
    mqw = _pad_cols(m_q_norm, 256)[:, None, :]
    mkw = _pad_cols(m_k_norm, 256)[:, None, :]
    tab_a, tab_r, tab_m = _tables(s, tp)

    c8 = jnp.zeros((8, d), F32).at[:b].set(c).at[b].set(c_ctx)
    mod_all = _ada(c8, w_ada, b_ada).reshape(depth, 8, N_MOD, d)

    xs = jnp.concatenate([x.reshape(nl, d), ctx.reshape(ncx, d)], axis=0)
    for i in range(depth):
        need_ctx = i < depth - 1
        rows = t if need_ctx else nl
        mod = mod_all[i]
        p, pdt = _inproj(xs, mod, norm1_w[i][None, :], w_mix, w_gate, i, dims, tm_big, 1024)
        qa, ka, va = _aprep(p, tab_a, a_q_norm[i][None, :], a_k_norm[i][None, :], dims, tp)
        ya = _aattn(a_sink[i].astype(F32), qa, ka, va, dims, need_ctx)
        rf, rb = _retention(log_gamma[i], p, tab_r, dims)
        u = _conv(p, s_conv_w[i], s_conv_b[i][None, :], dims, tp)
        sf, sb = _ssd(a_neg[i], dtb[i], u, pdt, dims)
        yb, yc = _scan_finish(rf, rb, p, r_norm[i].reshape(1, -1), sf, sb, u, dsk[i],
                              s_norm[i][None, :], rows, tm)
        qm, km, vm = _mprep(p, tab_m, m_cq_norm[i][None, :],
                            m_ckv_norm[i][None, :], wq[i], wkv[i], mqw[i], mkw[i], dims, tp)
        yd = _mla(qm, km, vm, dims, need_ctx, 256, 256)
        xs = _merge(xs, mod, (ya, yb, yc, yd), p, wb, wo, i, dims, rows, tm, 512)
        xs = _ffn(xs, mod, norm2_w[i][None, :], w1, w2, i, dims, rows, tm, 1024)
    return xs.reshape(b, s, d)
```

```python
import functools
import math

import jax
import jax.numpy as jnp
import numpy as np
from jax import lax
from jax.experimental import pallas as pl
from jax.experimental.pallas import tpu as pltpu

GRID_W = 64
EPS = 1e-6
ROPE_BASE = 10000.0
NEG_INF = -1e30

A_HEADS, A_KV_HEADS, A_HEAD_DIM, A_BLOCK = 4, 2, 128, 128
R_HEADS, R_QK_DIM, R_V_DIM, R_CHUNK = 4, 64, 128, 128
S_HEADS, S_HEAD_DIM, S_GROUPS, S_STATE, S_CONV, S_CHUNK = 8, 64, 2, 128, 5, 128
M_HEADS, M_Q_RANK, M_KV_RANK, M_NOPE, M_ROPE, M_V = 4, 512, 128, 128, 64, 128
M_QK_DIM = M_NOPE + M_ROPE
MV_ROWS = M_V + 16
N_BRANCH, N_MOD = 4, 6
BRANCH_W = 512

OFF_M, W_M = 0, 768
OFF_DT, W_DT = 768, 256
OFF_A, W_A = 1024, 1024
OFF_R, W_R = 2048, 1536
OFF_S, W_S = 3584, 1536
OFF_G = 5120

VMEM_LIMIT = 56 * 1024 * 1024

BF = jnp.bfloat16
F32 = jnp.float32


def _cparams(sem):
    return pltpu.CompilerParams(dimension_semantics=sem, vmem_limit_bytes=VMEM_LIMIT)


def _dot(a, b):
    return jnp.dot(a.astype(BF), b.astype(BF), preferred_element_type=F32)


def _dot_nt(a, b):
    return lax.dot_general(a.astype(BF), b.astype(BF), (((1,), (1,)), ((), ())),
                           preferred_element_type=F32)


def _silu(x):
    return x * (1.0 / (1.0 + jnp.exp(-x)))


def _rms(x, n):
    return x * lax.rsqrt(jnp.sum(x * x, axis=-1, keepdims=True) * (1.0 / n) + EPS)


def _rope(y, cos, s_lo, s_hi, shift):
    w = y.shape[-1]
    return y * cos + pltpu.roll(y, w - shift, 1) * s_lo + pltpu.roll(y, shift, 1) * s_hi


def _ada_kernel(c_ref, w_ref, b_ref, o_ref):
    o_ref[...] = _dot(_silu(c_ref[...]), w_ref[...]) + b_ref[...]


def _ada(c8, w_ada, b_ada):
    depth, d, n = w_ada.shape
    tn = 512
    return pl.pallas_call(
        _ada_kernel,
        grid=(depth, n // tn),
        in_specs=[pl.BlockSpec((8, d), lambda l, j: (0, 0)),
                  pl.BlockSpec((None, d, tn), lambda l, j: (l, 0, j)),
                  pl.BlockSpec((None, 1, tn), lambda l, j: (l, 0, j))],
        out_specs=pl.BlockSpec((None, 8, tn), lambda l, j: (l, 0, j)),
        out_shape=jax.ShapeDtypeStruct((depth, 8, n), F32),
        compiler_params=_cparams(("parallel", "parallel")),
        name="ada",
    )(c8, w_ada, b_ada.reshape(depth, 1, n))


def _inproj_kernel(x_ref, mod_ref, nw_ref, wm_ref, wg_ref, o_ref, dt_ref, h_sc, *, nm):
    j = pl.program_id(1)

    @pl.when(j == 0)
    def _():
        x = x_ref[...]
        y = _rms(x, x.shape[-1]) * nw_ref[...]
        h_sc[...] = (y * (1.0 + mod_ref[1:2, :]) + mod_ref[0:1, :]).astype(BF)
        r = jnp.dot(h_sc[...], wm_ref[...], preferred_element_type=F32)
        o_ref[...] = r.astype(o_ref.dtype)
        dt_ref[...] = r[:, OFF_DT:OFF_DT + W_DT]

    @pl.when(jnp.logical_and(j > 0, j < nm))
    def _():
        o_ref[...] = jnp.dot(h_sc[...], wm_ref[...],
                             preferred_element_type=F32).astype(o_ref.dtype)

    @pl.when(j >= nm)
    def _():
        o_ref[...] = jnp.dot(h_sc[...], wg_ref[...],
                             preferred_element_type=F32).astype(o_ref.dtype)


def _mod_index(i, tm, nl, s, b):
    return jnp.where(i < nl // tm, i // (s // tm), b)


def _inproj(xs, mod, nw, wm, wg, layer, dims, tm, tn):
    b, s, lc, nl, t = dims
    d = xs.shape[1]
    nm, ng = wm.shape[2] // tn, wg.shape[2] // tn
    assert OFF_DT + W_DT <= tn and wm.shape[2] == OFF_G
    return pl.pallas_call(
        functools.partial(_inproj_kernel, nm=nm),
        grid=(t // tm, nm + ng),
        in_specs=[pl.BlockSpec((tm, d), lambda i, j: (i, 0)),
                  pl.BlockSpec((None, N_MOD, d), lambda i, j: (_mod_index(i, tm, nl, s, b), 0, 0)),
                  pl.BlockSpec((1, d), lambda i, j: (0, 0)),
                  pl.BlockSpec((None, d, tn), lambda i, j: (layer, 0, jnp.minimum(j, nm - 1))),
                  pl.BlockSpec((None, d, tn), lambda i, j: (layer, 0, jnp.maximum(j - nm, 0)))],
        out_specs=[pl.BlockSpec((tm, tn), lambda i, j: (i, j)),
                   pl.BlockSpec((tm, W_DT), lambda i, j: (i, 0))],
        out_shape=[jax.ShapeDtypeStruct((t, (nm + ng) * tn), BF),
                   jax.ShapeDtypeStruct((t, W_DT), F32)],
        scratch_shapes=[pltpu.VMEM((tm, d), BF)],
        compiler_params=_cparams(("parallel", "arbitrary")),
        name="inproj",
    )(xs, mod, nw, wm, wg)


def _aprep_kernel(p_ref, cos_ref, slo_ref, shi_ref, qw_ref, kw_ref, q_ref, k_ref, v_ref):
    cos, slo, shi = cos_ref[...], slo_ref[...], shi_ref[...]
    scale = A_HEAD_DIM ** -0.5
    for h in range(A_HEADS):
        y = _rms(p_ref[:, h * 128:(h + 1) * 128].astype(F32), A_HEAD_DIM) * qw_ref[...]
        q_ref[:, h * 128:(h + 1) * 128] = (_rope(y, cos, slo, shi, 32) * scale).astype(BF)
    for h in range(A_KV_HEADS):
        y = _rms(p_ref[:, 512 + h * 128:512 + (h + 1) * 128].astype(F32), A_HEAD_DIM) * kw_ref[...]
        k_ref[:, h * 128:(h + 1) * 128] = _rope(y, cos, slo, shi, 32).astype(BF)
    for r in range(v_ref.shape[0]):
        v_ref[r] = p_ref[r * A_BLOCK:(r + 1) * A_BLOCK, 768:1024].astype(F32).T.astype(BF)


def _table_index(i, tm, nl, s):
    return jnp.where(i < nl // tm, i % (s // tm), s // tm)


def _aprep(p, tabs, qw, kw, dims, tm):
    b, s, lc, nl, t = dims
    tab_spec = pl.BlockSpec((tm, 128), lambda i: (_table_index(i, tm, nl, s), 0))
    return pl.pallas_call(
        _aprep_kernel,
        grid=(t // tm,),
        in_specs=[pl.BlockSpec((tm, W_A), lambda i: (i, OFF_A // W_A)),
                  tab_spec, tab_spec, tab_spec,
                  pl.BlockSpec((1, 128), lambda i: (0, 0)),
                  pl.BlockSpec((1, 128), lambda i: (0, 0))],
        out_specs=[pl.BlockSpec((tm, 512), lambda i: (i, 0)),
                   pl.BlockSpec((tm, 256), lambda i: (i, 0)),
                   pl.BlockSpec((tm // A_BLOCK, 256, A_BLOCK), lambda i: (i, 0, 0))],
        out_shape=[jax.ShapeDtypeStruct((t, 512), BF),
                   jax.ShapeDtypeStruct((t, 256), BF),
                   jax.ShapeDtypeStruct((t // A_BLOCK, 256, A_BLOCK), BF)],
        compiler_params=_cparams(("parallel",)),
        name="a_prep",
    )(p, *tabs, qw, kw)


def _aattn_kernel(sink_ref, q_ref, kp_ref, kc_ref, kn_ref, vp_ref, vc_ref, vn_ref,
                  kx_ref, vx_ref, o_ref, *, nb):
    n = pl.program_id(1)
    blk = A_BLOCK
    lc = kx_ref.shape[0]
    g = A_HEADS // A_KV_HEADS
    nk = 3 * blk + lc
    kj = lax.broadcasted_iota(jnp.int32, (nk, g * blk), 0)
    qi = lax.broadcasted_iota(jnp.int32, (nk, g * blk), 1) % blk
    lo_seq = jnp.where(n == 0, blk, 0)
    hi_seq = jnp.where(n < nb, jnp.where(n == nb - 1, 2 * blk - 1, 3 * blk - 1), -1)
    mask = ((kj >= jnp.maximum(qi, lo_seq)) & (kj <= jnp.minimum(qi + 2 * blk, hi_seq))) | (kj >= 3 * blk)
    lane = lax.broadcasted_iota(jnp.int32, (1, g * blk), 1)
    ss = []
    for hk in range(A_KV_HEADS):
        sl = slice(hk * 128, (hk + 1) * 128)
        q2 = jnp.concatenate([q_ref[:, (hk * g + gi) * 128:(hk * g + gi + 1) * 128]
                              for gi in range(g)], axis=0)
        kk = jnp.concatenate([kp_ref[:, sl], kc_ref[:, sl], kn_ref[:, sl], kx_ref[:, sl]], axis=0)
        ss.append(_dot_nt(kk, q2))
    for hk in range(A_KV_HEADS):
        sl = slice(hk * 128, (hk + 1) * 128)
        vvt = jnp.concatenate([vp_ref[sl, :], vc_ref[sl, :], vn_ref[sl, :]]
                              + [vx_ref[r, sl, :] for r in range(lc // blk)], axis=1)
        s = jnp.where(mask, ss[hk], NEG_INF)
        sink = jnp.where(lane < blk, sink_ref[hk * g], sink_ref[hk * g + 1])
        m = jnp.maximum(jnp.max(s, axis=0, keepdims=True), sink)
        p = jnp.exp(s - m)
        inv = 1.0 / (jnp.sum(p, axis=0, keepdims=True) + jnp.exp(sink - m))
        o = (jnp.dot(vvt, p.astype(BF), preferred_element_type=F32) * inv).T
        for gi in range(g):
            o_ref[:, (hk * g + gi) * 128:(hk * g + gi + 1) * 128] = (
                o[gi * blk:(gi + 1) * blk]).astype(o_ref.dtype)


def _aattn(sink, qa, ka, va, dims, need_ctx):
    b, s, lc, nl, t = dims
    blk = A_BLOCK
    nb = s // blk
    ncb = lc // blk
    steps = nb + (ncb if need_ctx else 0)
    rows = nl + (b * lc if need_ctx else 0)

    def qrow(bi, n):
        return jnp.where(n < nb, bi * nb + n, nl // blk + bi * ncb + (n - nb))

    def krow(off):
        def f(bi, n):
            return (bi * nb + jnp.clip(n + off, 0, nb - 1), 0)
        return f

    kspec = [pl.BlockSpec((blk, 256), krow(o)) for o in (-1, 0, 1)]
    vspec = [pl.BlockSpec((None, 256, blk), lambda bi, n, o=o: (*krow(o)(bi, n), 0))
             for o in (-1, 0, 1)]
    xspec = pl.BlockSpec((lc, 256), lambda bi, n: (nl // lc + bi, 0))
    vxspec = pl.BlockSpec((ncb, 256, blk), lambda bi, n: (nl // lc + bi, 0, 0))
    return pl.pallas_call(
        functools.partial(_aattn_kernel, nb=nb),
        grid=(b, steps),
        in_specs=[pl.BlockSpec(memory_space=pltpu.SMEM),
                  pl.BlockSpec((blk, 512), lambda bi, n: (qrow(bi, n), 0)),
                  *kspec, *vspec, xspec, vxspec],
        out_specs=pl.BlockSpec((blk, 512), lambda bi, n: (qrow(bi, n), 0)),
        out_shape=jax.ShapeDtypeStruct((rows, 512), BF),
        compiler_params=_cparams(("parallel", "arbitrary")),
        name="a_attn",
    )(sink, qa, ka, ka, ka, va, va, va, ka, va)


def _scan_slot(t, ncc, nlc, bwd):
    if not bwd:
        return t
    return jnp.where(t < ncc, ncc - 1 - t, ncc + (nlc - 1 - (t - ncc)))


def _scan_row(bi, t, ncc, nlc, nl, chunk, bwd):
    slot = _scan_slot(t, ncc, nlc, bwd)
    return jnp.where(slot < ncc, nl // chunk + bi * ncc + slot, bi * nlc + (slot - ncc))


def _ret_kernel(lg_ref, qf_ref, kf_ref, vf_ref, qb_ref, kb_ref, vb_ref,
                cosf_ref, slof_ref, shif_ref, cosb_ref, slob_ref, shib_ref,
                of_ref, ob_ref, st_sc, dm_sc, eq_sc, ek_sc):
    c = R_CHUNK
    hd = R_QK_DIM
    nh = R_HEADS

    def per_head(idx, vals):
        out = vals[nh - 1]
        for h in range(nh - 2, -1, -1):
            out = jnp.where(idx == h, vals[h], out)
        return out

    @pl.when(pl.program_id(1) == 0)
    def _():
        st_sc[...] = jnp.zeros_like(st_sc)
        ii = lax.broadcasted_iota(jnp.int32, (c, c), 0)
        jj = lax.broadcasted_iota(jnp.int32, (c, c), 1)
        col = lax.broadcasted_iota(jnp.int32, (c, 1), 0)
        row = lax.broadcasted_iota(jnp.int32, (1, c), 1)
        lane_head = lax.broadcasted_iota(jnp.int32, (1, nh * hd), 1) // hd
        sub_head = lax.broadcasted_iota(jnp.int32, (nh * hd, 1), 0) // hd
        for d in range(2):
            dd = ((jj - ii) if d else (ii - jj)).astype(F32)
            pcol = ((c - 1 - col) if d else col).astype(F32)
            prow = ((c - 1 - row) if d else row).astype(F32)
            lgs = [lg_ref[d, h] for h in range(nh)]
            eq_sc[d] = jnp.exp(per_head(lane_head, lgs) * (pcol + 1.0))
            ek_sc[d] = jnp.exp(per_head(sub_head, lgs) * (c - 1.0 - prow))
            for h in range(nh):
                dm_sc[d, h] = jnp.where(dd >= 0, jnp.exp(lgs[h] * jnp.maximum(dd, 0.0)), 0.0)

    lane_head = lax.broadcasted_iota(jnp.int32, (1, 2 * hd), 1) // hd
    dirs = []
    for d, (q_ref, k_ref, v_ref, cos_ref, slo_ref, shi_ref) in enumerate((
            (qf_ref, kf_ref, vf_ref, cosf_ref, slof_ref, shif_ref),
            (qb_ref, kb_ref, vb_ref, cosb_ref, slob_ref, shib_ref))):
        cos, slo, shi = cos_ref[...], slo_ref[...], shi_ref[...]
        q = _rope(q_ref[...].astype(F32), cos, slo, shi, 32)
        k = _rope(k_ref[...].astype(F32), cos, slo, shi, 32) * (hd ** -0.5)
        kt = k.T
        dirs.append((q, q * eq_sc[d], kt, kt * ek_sc[d], v_ref[...], st_sc[d]))
    outs = ([], [])
    new_states = ([], [])
    for h in range(nh):
        pr = h // 2
        for d, (q, qd, kt, kdt, v, state) in enumerate(dirs):
            sel = lane_head == (h % 2)
            qh = jnp.where(sel, q[:, pr * 128:(pr + 1) * 128], 0.0)
            qdh = jnp.where(sel, qd[:, pr * 128:(pr + 1) * 128], 0.0)
            vh = v[:, h * 128:(h + 1) * 128]
            s = _dot(qh, kt[pr * 128:(pr + 1) * 128, :]) * dm_sc[d, h]
            inter = _dot(qdh, state[pr * 128:(pr + 1) * 128, :])
            outs[d].append(_dot(s, vh) + inter)
            chunk_decay = jnp.exp(lg_ref[d, h] * jnp.full((1, R_V_DIM), float(c), F32))
            new_states[d].append(chunk_decay * state[h * hd:(h + 1) * hd, :]
                                 + _dot(kdt[h * hd:(h + 1) * hd, :], vh))
    for d, o_ref in enumerate((of_ref, ob_ref)):
        st_sc[d] = jnp.concatenate(new_states[d], axis=0)
        o_ref[...] = jnp.concatenate(outs[d], axis=1).astype(o_ref.dtype)


def _retention(lg, p, tabs, dims):
    b, s, lc, nl, t = dims
    c = R_CHUNK
    ncc, nlc = lc // c, s // c

    def row(bwd):
        return lambda bi, st: _scan_row(bi, st, ncc, nlc, nl, c, bwd)

    def pspec(width, col_off, bwd):
        r = row(bwd)
        return pl.BlockSpec((c, width), lambda bi, st: (r(bi, st), col_off // width))

    def tab_spec(bwd):
        def trow(bi, st):
            slot = _scan_slot(st, ncc, nlc, bwd)
            return (jnp.where(slot < ncc, nlc, slot - ncc), 0)
        return pl.BlockSpec((c, 256), trow)

    def qkv(bwd):
        return [pspec(256, OFF_R, bwd), pspec(256, OFF_R + 256, bwd), pspec(512, OFF_R + 512, bwd)]

    def ospec(bwd):
        r = row(bwd)
        return pl.BlockSpec((c, 512), lambda bi, st: (r(bi, st), 0))

    return pl.pallas_call(
        _ret_kernel,
        grid=(b, ncc + nlc),
        in_specs=[pl.BlockSpec(memory_space=pltpu.SMEM), *qkv(False), *qkv(True),
                  *[tab_spec(False)] * 3, *[tab_spec(True)] * 3],
        out_specs=[ospec(False), ospec(True)],
        out_shape=[jax.ShapeDtypeStruct((t, 512), BF)] * 2,
        scratch_shapes=[pltpu.VMEM((2, R_HEADS * R_QK_DIM, R_V_DIM), F32),
                        pltpu.VMEM((2, R_HEADS, c, c), F32),
                        pltpu.VMEM((2, c, R_HEADS * R_QK_DIM), F32),
                        pltpu.VMEM((2, R_HEADS * R_QK_DIM, c), F32)],
        compiler_params=_cparams(("parallel", "arbitrary")),
        name="retention",
    )(lg, p, p, p, p, p, p, *tabs, *tabs)


def _conv_kernel(x_ref, xp_ref, xn_ref, w_ref, b_ref, o_ref, e_sc, *, tm, blocks_per_seg, n_lat_blocks):
    i = pl.program_id(0)
    is_lat = i < n_lat_blocks
    first = jnp.logical_or(jnp.logical_not(is_lat), i % blocks_per_seg == 0)
    last = jnp.logical_or(jnp.logical_not(is_lat), i % blocks_per_seg == blocks_per_seg - 1)
    hr = HALO_ROWS
    e_sc[0:hr, :] = jnp.where(first, 0.0, xp_ref[...].astype(F32))
    e_sc[hr:hr + tm, :] = x_ref[...].astype(F32)
    e_sc[hr + tm:2 * hr + tm, :] = jnp.where(last, 0.0, xn_ref[...].astype(F32))
    pad = (S_CONV - 1) // 2
    acc = b_ref[...] + w_ref[0:1, :] * e_sc[hr - pad:hr - pad + tm, :]
    for kk in range(1, S_CONV):
        acc = acc + w_ref[kk:kk + 1, :] * e_sc[hr - pad + kk:hr - pad + kk + tm, :]
    o_ref[...] = _silu(acc)


HALO_ROWS = 16


def _conv(p, w, bias, dims, tm):
    b, s, lc, nl, t = dims
    ch = w.shape[1]
    r8 = tm // HALO_ROWS
    nblk = t // tm
    xoff = (OFF_S + 512) // ch
    assert (OFF_S + 512) % ch == 0
    assert lc == tm
    return pl.pallas_call(
        functools.partial(_conv_kernel, tm=tm, blocks_per_seg=s // tm, n_lat_blocks=nl // tm),
        grid=(nblk,),
        in_specs=[pl.BlockSpec((tm, ch), lambda i: (i, xoff)),
                  pl.BlockSpec((HALO_ROWS, ch), lambda i: (jnp.maximum(i * r8 - 1, 0), xoff)),
                  pl.BlockSpec((HALO_ROWS, ch),
                               lambda i: (jnp.minimum((i + 1) * r8, nblk * r8 - 1), xoff)),
                  pl.BlockSpec((S_CONV, ch), lambda i: (0, 0)),
                  pl.BlockSpec((1, ch), lambda i: (0, 0))],
        out_specs=pl.BlockSpec((tm, ch), lambda i: (i, 0)),
        out_shape=jax.ShapeDtypeStruct((t, ch), F32),
        scratch_shapes=[pltpu.VMEM((tm + 2 * HALO_ROWS, ch), F32)],
        compiler_params=_cparams(("parallel",)),
        name="s_conv",
    )(p, p, p, w, bias)


def _ssd_kernel(a_ref, dtb_ref, uf_ref, dtf_ref, ub_ref, dtbw_ref, yf_ref, yb_ref, h_sc):
    qn = S_CHUNK

    @pl.when(pl.program_id(1) == 0)
    def _():
        h_sc[...] = jnp.zeros_like(h_sc)

    ii = lax.broadcasted_iota(jnp.int32, (qn, qn), 0)
    jj = lax.broadcasted_iota(jnp.int32, (qn, qn), 1)
    tri = jnp.where(jj <= ii, 1.0, 0.0).astype(BF)
    hg = S_HEADS // S_GROUPS
    dirs = []
    for d, (u_ref, dt_ref) in enumerate(((uf_ref, dtf_ref), (ub_ref, dtbw_ref))):
        xr = dt_ref[...] + dtb_ref[...]
        dt = jnp.maximum(xr, 0.0) + jnp.log1p(jnp.exp(-jnp.abs(xr)))
        dta = dt * a_ref[...]
        hi = dta.astype(BF)
        lo = (dta - hi.astype(F32)).astype(BF)
        lo2 = (dta - hi.astype(F32) - lo.astype(F32)).astype(BF)
        pfx = (jnp.dot(tri, hi, preferred_element_type=F32)
               + jnp.dot(tri, lo, preferred_element_type=F32)
               + jnp.dot(tri, lo2, preferred_element_type=F32))
        total = pfx[qn - 1:qn, :]
        cum = (total - pfx + dta) if d else pfx
        mask = ((jj - ii) if d else (ii - jj)) >= 0
        dirs.append((u_ref, dt, cum, cum.T, total, mask, h_sc[d]))
    lane_lo = lax.broadcasted_iota(jnp.int32, (1, 128), 1) < S_HEAD_DIM
    ys = ([], [])
    hs = ([], [])
    for g in range(S_GROUPS):
        grp = []
        for d, (u_ref, dt, cum, cum_t, total, mask, hprev) in enumerate(dirs):
            bg = u_ref[:, 512 + g * 128:512 + (g + 1) * 128]
            cg = u_ref[:, 768 + g * 128:768 + (g + 1) * 128]
            cb = _dot_nt(cg, bg)
            yoff = _dot(cg, hprev[:, g * hg * 64:(g + 1) * hg * 64])
            grp.append((cb, bg.T, yoff))
        for pr in range(hg // 2):
            h0 = g * hg + 2 * pr
            for d, (u_ref, dt, cum, cum_t, total, mask, hprev) in enumerate(dirs):
                cb, bgt, yoff = grp[d]
                ws, cbs, dbs, tots = [], [], [], []
                for h in (h0, h0 + 1):
                    ln = d * S_HEADS + h
                    ccol_b = jnp.broadcast_to(cum[:, ln:ln + 1], (qn, 128))
                    ws.append(cb * jnp.exp(jnp.where(mask, ccol_b - cum_t[ln:ln + 1, :], NEG_INF)))
                    cbs.append(ccol_b)
                    dbs.append(jnp.broadcast_to(dt[:, ln:ln + 1], (qn, 128)))
                    tots.append(total[:, ln:ln + 1])
                csel = jnp.where(lane_lo, cbs[0], cbs[1])
                tsel = jnp.where(lane_lo, tots[0], tots[1])
                xdt = u_ref[:, h0 * 64:h0 * 64 + 128] * jnp.where(lane_lo, dbs[0], dbs[1])
                rhs = jnp.concatenate([jnp.where(lane_lo, xdt, 0.0), jnp.where(lane_lo, 0.0, xdt)],
                                      axis=0)
                y = (_dot(jnp.concatenate(ws, axis=1), rhs)
                     + yoff[:, pr * 128:(pr + 1) * 128] * jnp.exp(csel))
                hnew = (jnp.exp(tsel) * hprev[:, h0 * 64:h0 * 64 + 128]
                        + _dot(bgt, xdt * jnp.exp(tsel - csel)))
                ys[d].append(y)
                hs[d].append(hnew)
    for d, y_ref in enumerate((yf_ref, yb_ref)):
        h_sc[d] = jnp.concatenate(hs[d], axis=1)
        y_ref[...] = jnp.concatenate(ys[d], axis=1).astype(y_ref.dtype)


def _ssd(a_flat, dtb_flat, u, pdt, dims):
    b, s, lc, nl, t = dims
    c = S_CHUNK
    ncc, nlc = lc // c, s // c

    def spec(width, bwd):
        return pl.BlockSpec(
            (c, width), lambda bi, st: (_scan_row(bi, st, ncc, nlc, nl, c, bwd), 0))

    const = pl.BlockSpec((1, 128), lambda bi, st: (0, 0))
    return pl.pallas_call(
        _ssd_kernel,
        grid=(b, ncc + nlc),
        in_specs=[const, const, spec(1024, False), spec(128, False),
                  spec(1024, True), spec(128, True)],
        out_specs=[spec(512, False), spec(512, True)],
        out_shape=[jax.ShapeDtypeStruct((t, 512), BF)] * 2,
        scratch_shapes=[pltpu.VMEM((2, S_STATE, S_HEADS * S_HEAD_DIM), F32)],
        compiler_params=_cparams(("parallel", "arbitrary")),
        name="ssd",
    )(a_flat, dtb_flat, u, pdt, u, pdt)


def _scan_finish_kernel(rf_ref, rb_ref, g_ref, rnw_ref, sf_ref, sb_ref, x_ref, z_ref,
                        dsk_ref, snw_ref, yb_ref, yc_ref):
    tot = rf_ref[...].astype(F32) + rb_ref[...].astype(F32)
    ys = [_rms(tot[:, h * 128:(h + 1) * 128], R_V_DIM) for h in range(R_HEADS)]
    y = jnp.concatenate(ys, axis=1) * rnw_ref[...]
    yb_ref[...] = (y * _silu(g_ref[...].astype(F32))).astype(yb_ref.dtype)
    yt = (sf_ref[...].astype(F32) + sb_ref[...].astype(F32)
          + dsk_ref[...] * x_ref[...].astype(F32)) * _silu(z_ref[...].astype(F32))
    yc_ref[...] = (_rms(yt, yt.shape[-1]) * snw_ref[...]).astype(yc_ref.dtype)


def _scan_finish(rf, rb, p, rnw, sf, sb, u, dsk, snw, rows, tm):
    blk = lambda col: pl.BlockSpec((tm, 512), lambda i: (i, col))
    const = pl.BlockSpec((1, 512), lambda i: (0, 0))
    return pl.pallas_call(
        _scan_finish_kernel,
        grid=(rows // tm,),
        in_specs=[blk(0), blk(0), blk((OFF_R + 1024) // 512), const,
                  blk(0), blk(0), blk(0), blk(OFF_S // 512), const, const],
        out_specs=[blk(0), blk(0)],
        out_shape=[jax.ShapeDtypeStruct((rows, 512), BF)] * 2,
        compiler_params=_cparams(("parallel",)),
        name="scan_finish",
    )(rf, rb, p, rnw, sf, sb, u, p, dsk, snw)


def _mprep_kernel(p_ref, cos_ref, slo_ref, shi_ref, cqw_ref, ckvw_ref, wq_ref, wkv_ref,
                  qw_ref, kw_ref, q_ref, k_ref, v_ref):
    cos, slo, shi = cos_ref[...], slo_ref[...], shi_ref[...]
    scale = M_QK_DIM ** -0.5 * math.log2(math.e)
    cq = _rms(p_ref[:, 0:512].astype(F32), M_Q_RANK) * cqw_ref[...]
    q = _dot(cq, wq_ref[...])
    ckv = _rms(p_ref[:, 512:640].astype(F32), M_KV_RANK) * ckvw_ref[...]
    kv = _dot(ckv, wkv_ref[...])
    kr = p_ref[:, 640:768].astype(F32)
    kr_ss = jnp.sum(kr * kr, axis=-1, keepdims=True)
    for h in range(M_HEADS):
        c0 = q[:, h * 256:h * 256 + 128]
        c1 = q[:, h * 256 + 128:h * 256 + 256]
        r = lax.rsqrt((jnp.sum(c0 * c0, axis=-1, keepdims=True)
                       + jnp.sum(c1 * c1, axis=-1, keepdims=True)) * (1.0 / M_QK_DIM) + EPS)
        q_ref[:, h * 256:h * 256 + 128] = (c0 * r * qw_ref[:, 0:128] * scale).astype(BF)
        q_ref[:, h * 256 + 128:h * 256 + 256] = (
            _rope(c1 * r * qw_ref[:, 128:256], cos, slo, shi, 16) * scale).astype(BF)
        k0 = kv[:, h * 128:(h + 1) * 128]
        r = lax.rsqrt((jnp.sum(k0 * k0, axis=-1, keepdims=True) + kr_ss) * (1.0 / M_QK_DIM) + EPS)
        k_ref[:, h * 256:h * 256 + 128] = (k0 * r * kw_ref[:, 0:128]).astype(BF)
        k_ref[:, h * 256 + 128:h * 256 + 256] = (
            _rope(kr * r * kw_ref[:, 128:256], cos, slo, shi, 16)).astype(BF)
    ones = jnp.ones((MV_ROWS - M_V, kv.shape[0]), BF)
    for h in range(M_HEADS):
        v_ref[h * MV_ROWS:h * MV_ROWS + M_V, :] = kv[:, 512 + h * M_V:512 + (h + 1) * M_V].T.astype(BF)
        v_ref[h * MV_ROWS + M_V:(h + 1) * MV_ROWS, :] = ones


def _mprep(p, tabs, cqw, ckvw, wq, wkv, qw, kw, dims, tm):
    b, s, lc, nl, t = dims
    tab_spec = pl.BlockSpec((tm, 128), lambda i: (_table_index(i, tm, nl, s), 0))
    const = lambda shape: pl.BlockSpec(shape, lambda i: (0, 0))
    return pl.pallas_call(
        _mprep_kernel,
        grid=(t // tm,),
        in_specs=[pl.BlockSpec((tm, W_M), lambda i: (i, OFF_M // W_M)),
                  tab_spec, tab_spec, tab_spec,
                  const((1, 512)), const((1, 128)), const((512, 1024)), const((128, 1024)),
                  const((1, 256)), const((1, 256))],
        out_specs=[pl.BlockSpec((tm, 1024), lambda i: (i, 0)),
                   pl.BlockSpec((tm, 1024), lambda i: (i, 0)),
                   pl.BlockSpec((None, M_HEADS * MV_ROWS, tm), lambda i: (i, 0, 0))],
        out_shape=[jax.ShapeDtypeStruct((t, 1024), BF),
                   jax.ShapeDtypeStruct((t, 1024), BF),
                   jax.ShapeDtypeStruct((t // tm, M_HEADS * MV_ROWS, tm), BF)],
        compiler_params=_cparams(("parallel",)),
        name="m_prep",
    )(p, *tabs, cqw, ckvw, wq, wkv, qw, kw)


def _mla_kernel(q_ref, kx_ref, vx_ref, kl_ref, vl_ref, o_ref, acc_sc, s_sc, *, nq, tk):
    qi = pl.program_id(1)
    n_tiles = kl_ref.shape[0] // tk

    def scores(h, k_tile):
        return _dot_nt(k_tile[:, h * 256:(h + 1) * 256], q_ref[:, h * 256:(h + 1) * 256])

    def pv(h, vt_tile, p):
        return jnp.dot(vt_tile[h * MV_ROWS:(h + 1) * MV_ROWS, :], p.astype(BF),
                       preferred_element_type=F32)

    def k_tile(kt):
        return kl_ref.at[pl.ds(pl.multiple_of(kt * tk, tk), tk), :]

    s_ctx = [scores(h, kx_ref) for h in range(M_HEADS)]
    for h in range(M_HEADS):
        s_sc[h] = scores(h, k_tile(0))
    ms = []
    for h in range(M_HEADS):
        s = s_ctx[h]
        m0 = jnp.max(s, axis=0, keepdims=True)
        ms.append(m0)
        acc_sc[h] = pv(h, vx_ref, jnp.exp2(s - m0))

    def body(kt, ms):
        nxt = k_tile(jnp.minimum(kt + 1, n_tiles - 1))
        vt_tile = vl_ref.at[kt]
        ms_new = []
        s_cur = [s_sc[h] for h in range(M_HEADS)]
        s_nxt = [scores(h, nxt) for h in range(M_HEADS)]
        for h in range(M_HEADS):
            s = s_cur[h]
            m_new = jnp.maximum(ms[h], jnp.max(s, axis=0, keepdims=True))
            alpha = jnp.exp2(ms[h] - m_new)
            ms_new.append(m_new)
            acc_sc[h] = alpha * acc_sc[h] + pv(h, vt_tile, jnp.exp2(s - m_new))
        for h in range(M_HEADS):
            s_sc[h] = s_nxt[h]
        return tuple(ms_new)

    def finish():
        for h in range(M_HEADS):
            o = acc_sc[h, 0:M_V, :] * (1.0 / acc_sc[h, M_V:M_V + 1, :])
            o_ref[:, h * 128:(h + 1) * 128] = o.T.astype(o_ref.dtype)

    @pl.when(qi < nq)
    def _():
        lax.fori_loop(0, n_tiles, body, tuple(ms), unroll=4)
        finish()

    @pl.when(qi >= nq)
    def _():
        finish()


def _mla(qm, km, vmt, dims, need_ctx, tq, tk):
    b, s, lc, nl, t = dims
    nq = s // tq
    assert lc == tq and lc == tk and vmt.shape[2] == tk
    steps = nq + (1 if need_ctx else 0)
    rows = nl + (b * lc if need_ctx else 0)

    def qrow(bi, qi):
        return jnp.where(qi < nq, bi * nq + qi, nl // tq + bi)

    return pl.pallas_call(
        functools.partial(_mla_kernel, nq=nq, tk=tk),
        grid=(b, steps),
        in_specs=[pl.BlockSpec((tq, 1024), lambda bi, qi: (qrow(bi, qi), 0)),
                  pl.BlockSpec((lc, 1024), lambda bi, qi: (nl // lc + bi, 0)),
                  pl.BlockSpec((None, M_HEADS * MV_ROWS, tk), lambda bi, qi: (nl // tk + bi, 0, 0)),
                  pl.BlockSpec((s, 1024), lambda bi, qi: (bi, 0)),
                  pl.BlockSpec((s // tk, M_HEADS * MV_ROWS, tk), lambda bi, qi: (bi, 0, 0))],
        out_specs=pl.BlockSpec((tq, 512), lambda bi, qi: (qrow(bi, qi), 0)),
        out_shape=jax.ShapeDtypeStruct((rows, 512), BF),
        scratch_shapes=[pltpu.VMEM((M_HEADS, MV_ROWS, tq), F32),
                        pltpu.VMEM((M_HEADS, tk, tq), F32)],
        compiler_params=_cparams(("parallel", "arbitrary")),
        name="mla",
    )(qm, km, vmt, km, vmt)


def _merge_kernel(x_ref, mod_ref, ya_ref, yb_ref, yc_ref, yd_ref, g0_ref, g1_ref, g2_ref, g3_ref,
                  wb_ref, wo_ref, o_ref, acc_sc, *, tn):
    j = pl.program_id(1)
    acc = None
    for kk, (y_ref, g_ref) in enumerate(((ya_ref, g0_ref), (yb_ref, g1_ref),
                                         (yc_ref, g2_ref), (yd_ref, g3_ref))):
        gate = 1.0 / (1.0 + jnp.exp(-g_ref[...].astype(F32)))
        term = gate * jnp.dot(y_ref[...], wb_ref[kk], preferred_element_type=F32)
        acc = term if acc is None else acc + term
    acc_sc[:, pl.ds(pl.multiple_of(j * tn, tn), tn)] = acc.astype(BF)

    @pl.when(j == pl.num_programs(1) - 1)
    def _():
        o_ref[...] = x_ref[...] + mod_ref[2:3, :] * jnp.dot(
            acc_sc[...], wo_ref[...], preferred_element_type=F32)


def _merge(xs, mod, ys, p, wb, wo, layer, dims, rows, tm, tn):
    b, s, lc, nl, t = dims
    d = xs.shape[1]
    nj = d // tn
    goff = OFF_G // tn
    yspec = pl.BlockSpec((tm, BRANCH_W), lambda i, j: (i, 0))
    gspecs = [pl.BlockSpec((tm, tn), functools.partial(lambda i, j, kk: (i, goff + kk * nj + j), kk=kk))
              for kk in range(N_BRANCH)]
    return pl.pallas_call(
        functools.partial(_merge_kernel, tn=tn),
        grid=(rows // tm, nj),
        in_specs=[pl.BlockSpec((tm, d), lambda i, j: (i, 0)),
                  pl.BlockSpec((None, N_MOD, d), lambda i, j: (_mod_index(i, tm, nl, s, b), 0, 0)),
                  yspec, yspec, yspec, yspec, *gspecs,
                  pl.BlockSpec((None, N_BRANCH, BRANCH_W, tn), lambda i, j: (layer, 0, 0, j)),
                  pl.BlockSpec((None, d, d), lambda i, j: (layer, 0, 0))],
        out_specs=pl.BlockSpec((tm, d), lambda i, j: (i, 0)),
        out_shape=jax.ShapeDtypeStruct((rows, d), F32),
        scratch_shapes=[pltpu.VMEM((tm, d), BF)],
        compiler_params=_cparams(("parallel", "arbitrary")),
        name="merge",
    )(xs, mod, *ys, p, p, p, p, wb, wo)


def _ffn_kernel(x_ref, mod_ref, xn_ref, modn_ref, nw_ref, w1_ref, w2_ref, o_ref, h_sc):
    i = pl.program_id(0)
    f = pl.program_id(1)
    last = pl.num_programs(1) - 1
    slot = i % 2

    def normed(xr, mr):
        x = xr[...]
        y = _rms(x, x.shape[-1]) * nw_ref[...]
        return (y * (1.0 + mr[4:5, :]) + mr[3:4, :]).astype(BF)

    def mlp_part():
        u = jnp.maximum(jnp.dot(h_sc[slot], w1_ref[...], preferred_element_type=F32), 0.0)
        return jnp.dot((u * u).astype(BF), w2_ref[...], preferred_element_type=F32)

    @pl.when(jnp.logical_and(i == 0, f == 0))
    def _():
        h_sc[0] = normed(x_ref, mod_ref)

    @pl.when(f == 0)
    def _():
        o_ref[...] = mlp_part()

    @pl.when(jnp.logical_and(f > 0, f < last))
    def _():
        o_ref[...] += mlp_part()

    @pl.when(f == last)
    def _():
        part = mlp_part()
        h_sc[1 - slot] = normed(xn_ref, modn_ref)
        o_ref[...] = x_ref[...] + mod_ref[5:6, :] * (o_ref[...] + part)


def _ffn(xs, mod, nw, w1, w2, layer, dims, rows, tm, tf):
    b, s, lc, nl, t = dims
    d = xs.shape[1]
    dff = w1.shape[2]
    nblk = rows // tm
    assert dff // tf >= 2
    nxt = lambda i: jnp.minimum(i + 1, nblk - 1)
    return pl.pallas_call(
        _ffn_kernel,
        grid=(nblk, dff // tf),
        in_specs=[pl.BlockSpec((tm, d), lambda i, f: (i, 0)),
                  pl.BlockSpec((None, N_MOD, d), lambda i, f: (_mod_index(i, tm, nl, s, b), 0, 0)),
                  pl.BlockSpec((tm, d), lambda i, f: (nxt(i), 0)),
                  pl.BlockSpec((None, N_MOD, d),
                               lambda i, f: (_mod_index(nxt(i), tm, nl, s, b), 0, 0)),
                  pl.BlockSpec((1, d), lambda i, f: (0, 0)),
                  pl.BlockSpec((None, d, tf), lambda i, f: (layer, 0, f)),
                  pl.BlockSpec((None, tf, d), lambda i, f: (layer, f, 0))],
        out_specs=pl.BlockSpec((tm, d), lambda i, f: (i, 0)),
        out_shape=jax.ShapeDtypeStruct((rows, d), F32),
        scratch_shapes=[pltpu.VMEM((2, tm, d), BF)],
        compiler_params=_cparams(("arbitrary", "arbitrary")),
        name="ffn",
    )(xs, mod, xs, mod, nw, w1, w2)


def _rope_tables(pos_per_lane, freq_per_lane, lo_lane, pad_rows, live_lane=None):
    ang = pos_per_lane * freq_per_lane[None, :]
    cos, sin = jnp.cos(ang), jnp.sin(ang)
    if live_lane is not None:
        cos = jnp.where(live_lane[None, :], cos, 1.0)
        sin = jnp.where(live_lane[None, :], sin, 0.0)
    s_lo = jnp.where(lo_lane[None, :], -sin, 0.0)
    s_hi = jnp.where(lo_lane[None, :], 0.0, sin)
    w = cos.shape[1]
    ident = (jnp.ones((pad_rows, w), F32), jnp.zeros((pad_rows, w), F32), jnp.zeros((pad_rows, w), F32))
    return tuple(jnp.concatenate([tb.astype(F32), idn], axis=0)
                 for tb, idn in zip((cos, s_lo, s_hi), ident))


def _tables(s, pad_rows):
    t = jnp.arange(s)
    row = (t // GRID_W).astype(F32)[:, None]
    col = (t % GRID_W).astype(F32)[:, None]
    tpos = t.astype(F32)[:, None]
    lane = jnp.arange(128)
    fa = ROPE_BASE ** (-(lane % 32).astype(F32) / 32)
    pos_a = jnp.where((lane < 64)[None, :], row, col)
    tab_a = _rope_tables(pos_a, fa, (lane % 64) < 32, pad_rows)
    lane = jnp.arange(256)
    fb = ROPE_BASE ** (-(lane % 32).astype(F32) / 32)
    tab_r = _rope_tables(jnp.broadcast_to(tpos, (s, 256)), fb, (lane % 64) < 32, pad_rows)
    lane = jnp.arange(128)
    fm = ROPE_BASE ** (-(lane % 16).astype(F32) / 16)
    pos_m = jnp.where((lane < 32)[None, :], row, col)
    tab_m = _rope_tables(pos_m, fm, (lane % 32) < 16, pad_rows, live_lane=lane < 64)
    return tab_a, tab_r, tab_m


def _pad_cols(w, n):
    return jnp.pad(w, [(0, 0)] * (w.ndim - 1) + [(0, n - w.shape[-1])])


def kernel(x, c, ctx, c_ctx, w_ada, b_ada, norm1_w, norm2_w, w_in, a_q_norm, a_k_norm, a_sink,
           r_decay, r_norm, s_conv_w, s_conv_b, s_a_log, s_dt_bias, s_d, s_norm,
           m_cq_norm, m_ckv_norm, m_w_uq, m_w_ukv, m_q_norm, m_k_norm,
           w_branch, w_o, w_ff1, w_ff2):
    b, s, d = x.shape
    lc = ctx.shape[1]
    depth = w_ada.shape[0]
    nl, ncx = b * s, b * lc
    t = nl + ncx
    dims = (b, s, lc, nl, t)
    tm_big = max(m for m in (1024, 512, 256) if s % m == 0 and ncx % m == 0)
    tm = max(m for m in (512, 256) if s % m == 0 and ncx % m == 0)
    tp = 256
    assert lc == 256 and s % 256 == 0 and b + 1 <= 8

    w_in16 = w_in.astype(BF)
    w_mix = jnp.concatenate([_pad_cols(w_in16[..., 4112:4816], W_M),
                             _pad_cols(w_in16[..., 4096:4112], W_DT), w_in16[..., 0:4096]], axis=-1)
    w_gate = w_in16[..., 4816:]
    wq = _pad_cols(m_w_uq.reshape(depth, M_Q_RANK, M_HEADS, M_QK_DIM), 256)
    wq = wq.reshape(depth, M_Q_RANK, M_HEADS * 256).astype(BF)
    wkv = m_w_ukv.reshape(depth, M_KV_RANK, M_HEADS, M_NOPE + M_V)
    wkv = jnp.concatenate([wkv[..., :M_NOPE].reshape(depth, M_KV_RANK, M_HEADS * M_NOPE),
                           wkv[..., M_NOPE:].reshape(depth, M_KV_RANK, M_HEADS * M_V)],
                          axis=-1).astype(BF)
    wb = w_branch.astype(BF)
    wo = w_o.astype(BF)
    w1 = w_ff1.astype(BF)
    w2 = w_ff2.astype(BF)
    log_gamma = jnp.log1p(-jnp.exp(r_decay.astype(F32)))
    a_neg = _pad_cols(-jnp.exp(s_a_log.astype(F32)).reshape(depth, 1, 2 * S_HEADS), 128)
    dtb = _pad_cols(s_dt_bias.astype(F32).reshape(depth, 1, 2 * S_HEADS), 128)
    dsk = jnp.repeat(s_d.astype(F32), S_HEAD_DIM, axis=-1)[:, None, :]
    mqw = _pad_cols(m_q_norm, 256)[:, None, :]
    mkw = _pad_cols(m_k_norm, 256)[:, None, :]
    tab_a, tab_r, tab_m = _tables(s, tp)

    c8 = jnp.zeros((8, d), F32).at[:b].set(c).at[b].set(c_ctx)
    mod_all = _ada(c8, w_ada, b_ada).reshape(depth, 8, N_MOD, d)

    xs = jnp.concatenate([x.reshape(nl, d), ctx.reshape(ncx, d)], axis=0)
    for i in range(depth):
        need_ctx = i < depth - 1
        rows = t if need_ctx else nl
        mod = mod_all[i]
        p, pdt = _inproj(xs, mod, norm1_w[i][None, :], w_mix, w_gate, i, dims, tm_big, 1024)
        qa, ka, va = _aprep(p, tab_a, a_q_norm[i][None, :], a_k_norm[i][None, :], dims, tp)
        ya = _aattn(a_sink[i].astype(F32), qa, ka, va, dims, need_ctx)
        rf, rb = _retention(log_gamma[i], p, tab_r, dims)
        u = _conv(p, s_conv_w[i], s_conv_b[i][None, :], dims, tp)
        sf, sb = _ssd(a_neg[i], dtb[i], u, pdt, dims)
        yb, yc = _scan_finish(rf, rb, p, r_norm[i].reshape(1, -1), sf, sb, u, dsk[i],
                              s_norm[i][None, :], rows, tm)
        qm, km, vm = _mprep(p, tab_m, m_cq_norm[i][None, :],
                            m_ckv_norm[i][None, :], wq[i], wkv[i], mqw[i], mkw[i], dims, tp)
        yd = _mla(qm, km, vm, dims, need_ctx, 256, 256)
        xs = _merge(xs, mod, (ya, yb, yc, yd), p, wb, wo, i, dims, rows, tm, 512)
        xs = _ffn(xs, mod, norm2_w[i][None, :], w1, w2, i, dims, rows, tm, 1024)
    return xs.reshape(b, s, d)
```

```python
import functools
import math

import jax
import jax.numpy as jnp
from jax import lax
from jax.experimental import pallas as pl
from jax.experimental.pallas import tpu as pltpu

GRID_W = 64
EPS = 1e-6
ROPE_BASE = 10000.0
NEG_INF = -1e30

A_HEADS, A_KV_HEADS, A_HEAD_DIM, A_BLOCK = 4, 2, 128, 128
R_HEADS, R_QK_DIM, R_V_DIM, R_CHUNK = 4, 64, 128, 128
S_HEADS, S_HEAD_DIM, S_GROUPS, S_STATE, S_CONV, S_CHUNK = 8, 64, 2, 128, 5, 128
M_HEADS, M_Q_RANK, M_KV_RANK, M_NOPE, M_ROPE, M_V = 4, 512, 128, 128, 64, 128
M_QK_DIM = M_NOPE + M_ROPE
MV_ROWS = M_V + 16
N_BRANCH, N_MOD = 4, 6
BRANCH_W = 512

OFF_M, W_M = 0, 768
OFF_DT, W_DT = 768, 256
OFF_A, W_A = 1024, 1024
OFF_R, W_R = 2048, 1536
OFF_S, W_S = 3584, 1536
OFF_G = 5120

VMEM_LIMIT = 56 * 1024 * 1024

BF = jnp.bfloat16
F32 = jnp.float32


def _cparams(sem):
    return pltpu.CompilerParams(dimension_semantics=sem, vmem_limit_bytes=VMEM_LIMIT)


def _dot(a, b):
    return jnp.dot(a.astype(BF), b.astype(BF), preferred_element_type=F32)


def _dot_nt(a, b):
    return lax.dot_general(a.astype(BF), b.astype(BF), (((1,), (1,)), ((), ())),
                           preferred_element_type=F32)


def _silu(x):
    return x * (1.0 / (1.0 + jnp.exp(-x)))


def _rms(x, n):
    return x * lax.rsqrt(jnp.sum(x * x, axis=-1, keepdims=True) * (1.0 / n) + EPS)


def _rope(y, cos, s_lo, s_hi, shift):
    w = y.shape[-1]
    return y * cos + pltpu.roll(y, w - shift, 1) * s_lo + pltpu.roll(y, shift, 1) * s_hi


def _ada_kernel(c_ref, w_ref, b_ref, o_ref):
    o_ref[...] = _dot(_silu(c_ref[...]), w_ref[...]) + b_ref[...]


def _ada(c8, w_ada, b_ada):
    depth, d, n = w_ada.shape
    tn = 512
    return pl.pallas_call(
        _ada_kernel,
        grid=(depth, n // tn),
        in_specs=[pl.BlockSpec((8, d), lambda l, j: (0, 0)),
                  pl.BlockSpec((None, d, tn), lambda l, j: (l, 0, j)),
                  pl.BlockSpec((None, 1, tn), lambda l, j: (l, 0, j))],
        out_specs=pl.BlockSpec((None, 8, tn), lambda l, j: (l, 0, j)),
        out_shape=jax.ShapeDtypeStruct((depth, 8, n), F32),
        compiler_params=_cparams(("parallel", "parallel")),
        name="ada",
    )(c8, w_ada, b_ada.reshape(depth, 1, n))


def _inproj_kernel(x_ref, mod_ref, nw_ref, wm_ref, wg_ref, o_ref, dt_ref, h_sc, *, nm):
    j = pl.program_id(1)

    @pl.when(j == 0)
    def _():
        x = x_ref[...]
        y = _rms(x, x.shape[-1]) * nw_ref[...]
        h_sc[...] = (y * (1.0 + mod_ref[1:2, :]) + mod_ref[0:1, :]).astype(BF)
        r = jnp.dot(h_sc[...], wm_ref[...], preferred_element_type=F32)
        o_ref[...] = r.astype(o_ref.dtype)
        dt_ref[...] = r[:, OFF_DT:OFF_DT + W_DT]

    @pl.when(jnp.logical_and(j > 0, j < nm))
    def _():
        o_ref[...] = jnp.dot(h_sc[...], wm_ref[...],
                             preferred_element_type=F32).astype(o_ref.dtype)

    @pl.when(j >= nm)
    def _():
        o_ref[...] = jnp.dot(h_sc[...], wg_ref[...],
                             preferred_element_type=F32).astype(o_ref.dtype)


def _mod_index(i, tm, nl, s, b):
    return jnp.where(i < nl // tm, i // (s // tm), b)


def _inproj(xs, mod, nw, wm, wg, layer, dims, tm, tn):
    b, s, lc, nl, t = dims
    d = xs.shape[1]
    nm, ng = wm.shape[2] // tn, wg.shape[2] // tn
    assert OFF_DT + W_DT <= tn and wm.shape[2] == OFF_G
    return pl.pallas_call(
        functools.partial(_inproj_kernel, nm=nm),
        grid=(t // tm, nm + ng),
        in_specs=[pl.BlockSpec((tm, d), lambda i, j: (i, 0)),
                  pl.BlockSpec((None, N_MOD, d), lambda i, j: (_mod_index(i, tm, nl, s, b), 0, 0)),
                  pl.BlockSpec((1, d), lambda i, j: (0, 0)),
                  pl.BlockSpec((None, d, tn), lambda i, j: (layer, 0, jnp.minimum(j, nm - 1))),
                  pl.BlockSpec((None, d, tn), lambda i, j: (layer, 0, jnp.maximum(j - nm, 0)))],
        out_specs=[pl.BlockSpec((tm, tn), lambda i, j: (i, j)),
                   pl.BlockSpec((tm, W_DT), lambda i, j: (i, 0))],
        out_shape=[jax.ShapeDtypeStruct((t, (nm + ng) * tn), BF),
                   jax.ShapeDtypeStruct((t, W_DT), F32)],
        scratch_shapes=[pltpu.VMEM((tm, d), BF)],
        compiler_params=_cparams(("parallel", "arbitrary")),
        name="inproj",
    )(xs, mod, nw, wm, wg)


def _aprep_kernel(p_ref, cos_ref, slo_ref, shi_ref, qw_ref, kw_ref, q_ref, k_ref, v_ref):
    cos, slo, shi = cos_ref[...], slo_ref[...], shi_ref[...]
    scale = A_HEAD_DIM ** -0.5
    for h in range(A_HEADS):
        y = _rms(p_ref[:, h * 128:(h + 1) * 128].astype(F32), A_HEAD_DIM) * qw_ref[...]
        q_ref[:, h * 128:(h + 1) * 128] = (_rope(y, cos, slo, shi, 32) * scale).astype(BF)
    for h in range(A_KV_HEADS):
        y = _rms(p_ref[:, 512 + h * 128:512 + (h + 1) * 128].astype(F32), A_HEAD_DIM) * kw_ref[...]
        k_ref[:, h * 128:(h + 1) * 128] = _rope(y, cos, slo, shi, 32).astype(BF)
    for r in range(v_ref.shape[0]):
        v_ref[r] = p_ref[r * A_BLOCK:(r + 1) * A_BLOCK, 768:1024].astype(F32).T.astype(BF)


def _table_index(i, tm, nl, s):
    return jnp.where(i < nl // tm, i % (s // tm), s // tm)


def _aprep(p, tabs, qw, kw, dims, tm):
    b, s, lc, nl, t = dims
    tab_spec = pl.BlockSpec((tm, 128), lambda i: (_table_index(i, tm, nl, s), 0))
    return pl.pallas_call(
        _aprep_kernel,
        grid=(t // tm,),
        in_specs=[pl.BlockSpec((tm, W_A), lambda i: (i, OFF_A // W_A)),
                  tab_spec, tab_spec, tab_spec,
                  pl.BlockSpec((1, 128), lambda i: (0, 0)),
                  pl.BlockSpec((1, 128), lambda i: (0, 0))],
        out_specs=[pl.BlockSpec((tm, 512), lambda i: (i, 0)),
                   pl.BlockSpec((tm, 256), lambda i: (i, 0)),
                   pl.BlockSpec((tm // A_BLOCK, 256, A_BLOCK), lambda i: (i, 0, 0))],
        out_shape=[jax.ShapeDtypeStruct((t, 512), BF),
                   jax.ShapeDtypeStruct((t, 256), BF),
                   jax.ShapeDtypeStruct((t // A_BLOCK, 256, A_BLOCK), BF)],
        compiler_params=_cparams(("parallel",)),
        name="a_prep",
    )(p, *tabs, qw, kw)


def _aattn_kernel(sink_ref, q_ref, kp_ref, kc_ref, kn_ref, vp_ref, vc_ref, vn_ref,
                  kx_ref, vx_ref, o_ref, *, nb):
    n = pl.program_id(1)
    blk = A_BLOCK
    lc = kx_ref.shape[0]
    g = A_HEADS // A_KV_HEADS
    nk = 3 * blk + lc
    kj = lax.broadcasted_iota(jnp.int32, (nk, g * blk), 0)
    qi = lax.broadcasted_iota(jnp.int32, (nk, g * blk), 1) % blk
    lo_seq = jnp.where(n == 0, blk, 0)
    hi_seq = jnp.where(n < nb, jnp.where(n == nb - 1, 2 * blk - 1, 3 * blk - 1), -1)
    mask = ((kj >= jnp.maximum(qi, lo_seq)) & (kj <= jnp.minimum(qi + 2 * blk, hi_seq))) | (kj >= 3 * blk)
    lane = lax.broadcasted_iota(jnp.int32, (1, g * blk), 1)
    ss = []
    for hk in range(A_KV_HEADS):
        sl = slice(hk * 128, (hk + 1) * 128)
        q2 = jnp.concatenate([q_ref[:, (hk * g + gi) * 128:(hk * g + gi + 1) * 128]
                              for gi in range(g)], axis=0)
        kk = jnp.concatenate([kp_ref[:, sl], kc_ref[:, sl], kn_ref[:, sl], kx_ref[:, sl]], axis=0)
        ss.append(_dot_nt(kk, q2))
    for hk in range(A_KV_HEADS):
        sl = slice(hk * 128, (hk + 1) * 128)
        vvt = jnp.concatenate([vp_ref[sl, :], vc_ref[sl, :], vn_ref[sl, :]]
                              + [vx_ref[r, sl, :] for r in range(lc // blk)], axis=1)
        s = jnp.where(mask, ss[hk], NEG_INF)
        sink = jnp.where(lane < blk, sink_ref[hk * g], sink_ref[hk * g + 1])
        m = jnp.maximum(jnp.max(s, axis=0, keepdims=True), sink)
        p = jnp.exp(s - m)
        inv = 1.0 / (jnp.sum(p, axis=0, keepdims=True) + jnp.exp(sink - m))
        o = (jnp.dot(vvt, p.astype(BF), preferred_element_type=F32) * inv).T
        for gi in range(g):
            o_ref[:, (hk * g + gi) * 128:(hk * g + gi + 1) * 128] = (
                o[gi * blk:(gi + 1) * blk]).astype(o_ref.dtype)


def _aattn(sink, qa, ka, va, dims, need_ctx):
    b, s, lc, nl, t = dims
    blk = A_BLOCK
    nb = s // blk
    ncb = lc // blk
    steps = nb + (ncb if need_ctx else 0)
    rows = nl + (b * lc if need_ctx else 0)

    def qrow(bi, n):
        return jnp.where(n < nb, bi * nb + n, nl // blk + bi * ncb + (n - nb))

    def krow(off):
        def f(bi, n):
            return (bi * nb + jnp.clip(n + off, 0, nb - 1), 0)
        return f

    kspec = [pl.BlockSpec((blk, 256), krow(o)) for o in (-1, 0, 1)]
    vspec = [pl.BlockSpec((None, 256, blk), lambda bi, n, o=o: (*krow(o)(bi, n), 0))
             for o in (-1, 0, 1)]
    xspec = pl.BlockSpec((lc, 256), lambda bi, n: (nl // lc + bi, 0))
    vxspec = pl.BlockSpec((ncb, 256, blk), lambda bi, n: (nl // lc + bi, 0, 0))
    return pl.pallas_call(
        functools.partial(_aattn_kernel, nb=nb),
        grid=(b, steps),
        in_specs=[pl.BlockSpec(memory_space=pltpu.SMEM),
                  pl.BlockSpec((blk, 512), lambda bi, n: (qrow(bi, n), 0)),
                  *kspec, *vspec, xspec, vxspec],
        out_specs=pl.BlockSpec((blk, 512), lambda bi, n: (qrow(bi, n), 0)),
        out_shape=jax.ShapeDtypeStruct((rows, 512), BF),
        compiler_params=_cparams(("parallel", "arbitrary")),
        name="a_attn",
    )(sink, qa, ka, ka, ka, va, va, va, ka, va)


def _scan_slot(t, ncc, nlc, bwd):
    if not bwd:
        return t
    return jnp.where(t < ncc, ncc - 1 - t, ncc + (nlc - 1 - (t - ncc)))


def _scan_row(bi, t, ncc, nlc, nl, chunk, bwd):
    slot = _scan_slot(t, ncc, nlc, bwd)
    return jnp.where(slot < ncc, nl // chunk + bi * ncc + slot, bi * nlc + (slot - ncc))


def _ret_init(lg_ref, st_sc, dm_sc, eq_sc, ek_sc):
    c = R_CHUNK
    hd = R_QK_DIM
    nh = R_HEADS

    def per_head(idx, vals):
        out = vals[nh - 1]
        for h in range(nh - 2, -1, -1):
            out = jnp.where(idx == h, vals[h], out)
        return out

    st_sc[...] = jnp.zeros_like(st_sc)
    ii = lax.broadcasted_iota(jnp.int32, (c, c), 0)
    jj = lax.broadcasted_iota(jnp.int32, (c, c), 1)
    col = lax.broadcasted_iota(jnp.int32, (c, 1), 0)
    row = lax.broadcasted_iota(jnp.int32, (1, c), 1)
    lane_head = lax.broadcasted_iota(jnp.int32, (1, nh * hd), 1) // hd
    sub_head = lax.broadcasted_iota(jnp.int32, (nh * hd, 1), 0) // hd
    for d in range(2):
        dd = ((jj - ii) if d else (ii - jj)).astype(F32)
        pcol = ((c - 1 - col) if d else col).astype(F32)
        prow = ((c - 1 - row) if d else row).astype(F32)
        lgs = [lg_ref[d, h] for h in range(nh)]
        eq_sc[d] = jnp.exp(per_head(lane_head, lgs) * (pcol + 1.0))
        ek_sc[d] = jnp.exp(per_head(sub_head, lgs) * (c - 1.0 - prow))
        for h in range(nh):
            dm_sc[d, h] = jnp.where(dd >= 0, jnp.exp(lgs[h] * jnp.maximum(dd, 0.0)), 0.0)


def _ret_step(lg_ref, qf_ref, kf_ref, vf_ref, qb_ref, kb_ref, vb_ref,
              cosf_ref, slof_ref, shif_ref, cosb_ref, slob_ref, shib_ref,
              of_ref, ob_ref, st_sc, dm_sc, eq_sc, ek_sc):
    c = R_CHUNK
    hd = R_QK_DIM
    nh = R_HEADS
    lane_head = lax.broadcasted_iota(jnp.int32, (1, 2 * hd), 1) // hd
    dirs = []
    for d, (q_ref, k_ref, v_ref, cos_ref, slo_ref, shi_ref) in enumerate((
            (qf_ref, kf_ref, vf_ref, cosf_ref, slof_ref, shif_ref),
            (qb_ref, kb_ref, vb_ref, cosb_ref, slob_ref, shib_ref))):
        cos, slo, shi = cos_ref[...], slo_ref[...], shi_ref[...]
        q = _rope(q_ref[...].astype(F32), cos, slo, shi, 32)
        k = _rope(k_ref[...].astype(F32), cos, slo, shi, 32) * (hd ** -0.5)
        kt = k.T
        dirs.append((q, q * eq_sc[d], kt, kt * ek_sc[d], v_ref[...], st_sc[d]))
    outs = ([], [])
    new_states = ([], [])
    for h in range(nh):
        pr = h // 2
        for d, (q, qd, kt, kdt, v, state) in enumerate(dirs):
            sel = lane_head == (h % 2)
            qh = jnp.where(sel, q[:, pr * 128:(pr + 1) * 128], 0.0)
            qdh = jnp.where(sel, qd[:, pr * 128:(pr + 1) * 128], 0.0)
            vh = v[:, h * 128:(h + 1) * 128]
            s = _dot(qh, kt[pr * 128:(pr + 1) * 128, :]) * dm_sc[d, h]
            inter = _dot(qdh, state[pr * 128:(pr + 1) * 128, :])
            outs[d].append(_dot(s, vh) + inter)
            chunk_decay = jnp.exp(lg_ref[d, h] * jnp.full((1, R_V_DIM), float(c), F32))
            new_states[d].append(chunk_decay * state[h * hd:(h + 1) * hd, :]
                                 + _dot(kdt[h * hd:(h + 1) * hd, :], vh))
    for d, o_ref in enumerate((of_ref, ob_ref)):
        st_sc[d] = jnp.concatenate(new_states[d], axis=0)
        o_ref[...] = jnp.concatenate(outs[d], axis=1).astype(o_ref.dtype)


N_RET_IN, N_SSD_IN = 13, 6


def _scan_kernel(*refs):
    ret_in = refs[:N_RET_IN]
    ssd_in = refs[N_RET_IN:N_RET_IN + N_SSD_IN]
    of_ref, ob_ref, yf_ref, yb_ref, st_sc, dm_sc, eq_sc, ek_sc, h_sc = refs[N_RET_IN + N_SSD_IN:]

    @pl.when(pl.program_id(1) == 0)
    def _():
        _ret_init(ret_in[0], st_sc, dm_sc, eq_sc, ek_sc)
        h_sc[...] = jnp.zeros_like(h_sc)

    _ret_step(*ret_in, of_ref, ob_ref, st_sc, dm_sc, eq_sc, ek_sc)
    _ssd_step(*ssd_in, yf_ref, yb_ref, h_sc)


def _scans(lg, p, tabs, a_flat, dtb_flat, u, pdt, dims):
    b, s, lc, nl, t = dims
    c = R_CHUNK
    assert S_CHUNK == c
    ncc, nlc = lc // c, s // c

    def row(bwd):
        return lambda bi, st: _scan_row(bi, st, ncc, nlc, nl, c, bwd)

    def spec(width, col_off, bwd):
        r = row(bwd)
        return pl.BlockSpec((c, width), lambda bi, st: (r(bi, st), col_off // width))

    def tab_spec(bwd):
        def trow(bi, st):
            slot = _scan_slot(st, ncc, nlc, bwd)
            return (jnp.where(slot < ncc, nlc, slot - ncc), 0)
        return pl.BlockSpec((c, 256), trow)

    def qkv(bwd):
        return [spec(256, OFF_R, bwd), spec(256, OFF_R + 256, bwd), spec(512, OFF_R + 512, bwd)]

    const = pl.BlockSpec((1, 128), lambda bi, st: (0, 0))
    ret_specs = [pl.BlockSpec(memory_space=pltpu.SMEM), *qkv(False), *qkv(True),
                 *[tab_spec(False)] * 3, *[tab_spec(True)] * 3]
    ssd_specs = [const, const, spec(1024, 0, False), spec(128, 0, False),
                 spec(1024, 0, True), spec(128, 0, True)]
    assert len(ret_specs) == N_RET_IN and len(ssd_specs) == N_SSD_IN
    return pl.pallas_call(
        _scan_kernel,
        grid=(b, ncc + nlc),
        in_specs=ret_specs + ssd_specs,
        out_specs=[spec(512, 0, False), spec(512, 0, True)] * 2,
        out_shape=[jax.ShapeDtypeStruct((t, 512), BF)] * 4,
        scratch_shapes=[pltpu.VMEM((2, R_HEADS * R_QK_DIM, R_V_DIM), F32),
                        pltpu.VMEM((2, R_HEADS, c, c), F32),
                        pltpu.VMEM((2, c, R_HEADS * R_QK_DIM), F32),
                        pltpu.VMEM((2, R_HEADS * R_QK_DIM, c), F32),
                        pltpu.VMEM((2, S_STATE, S_HEADS * S_HEAD_DIM), F32)],
        compiler_params=_cparams(("parallel", "arbitrary")),
        name="scans",
    )(lg, p, p, p, p, p, p, *tabs, *tabs, a_flat, dtb_flat, u, pdt, u, pdt)


def _conv_kernel(x_ref, xp_ref, xn_ref, w_ref, b_ref, o_ref, e_sc, *, tm, blocks_per_seg, n_lat_blocks):
    i = pl.program_id(0)
    is_lat = i < n_lat_blocks
    first = jnp.logical_or(jnp.logical_not(is_lat), i % blocks_per_seg == 0)
    last = jnp.logical_or(jnp.logical_not(is_lat), i % blocks_per_seg == blocks_per_seg - 1)
    hr = HALO_ROWS
    e_sc[0:hr, :] = jnp.where(first, 0.0, xp_ref[...].astype(F32))
    e_sc[hr:hr + tm, :] = x_ref[...].astype(F32)
    e_sc[hr + tm:2 * hr + tm, :] = jnp.where(last, 0.0, xn_ref[...].astype(F32))
    pad = (S_CONV - 1) // 2
    acc = b_ref[...] + w_ref[0:1, :] * e_sc[hr - pad:hr - pad + tm, :]
    for kk in range(1, S_CONV):
        acc = acc + w_ref[kk:kk + 1, :] * e_sc[hr - pad + kk:hr - pad + kk + tm, :]
    o_ref[...] = _silu(acc)


HALO_ROWS = 16


def _conv(p, w, bias, dims, tm):
    b, s, lc, nl, t = dims
    ch = w.shape[1]
    r8 = tm // HALO_ROWS
    nblk = t // tm
    xoff = (OFF_S + 512) // ch
    assert (OFF_S + 512) % ch == 0
    assert lc == tm
    return pl.pallas_call(
        functools.partial(_conv_kernel, tm=tm, blocks_per_seg=s // tm, n_lat_blocks=nl // tm),
        grid=(nblk,),
        in_specs=[pl.BlockSpec((tm, ch), lambda i: (i, xoff)),
                  pl.BlockSpec((HALO_ROWS, ch), lambda i: (jnp.maximum(i * r8 - 1, 0), xoff)),
                  pl.BlockSpec((HALO_ROWS, ch),
                               lambda i: (jnp.minimum((i + 1) * r8, nblk * r8 - 1), xoff)),
                  pl.BlockSpec((S_CONV, ch), lambda i: (0, 0)),
                  pl.BlockSpec((1, ch), lambda i: (0, 0))],
        out_specs=pl.BlockSpec((tm, ch), lambda i: (i, 0)),
        out_shape=jax.ShapeDtypeStruct((t, ch), F32),
        scratch_shapes=[pltpu.VMEM((tm + 2 * HALO_ROWS, ch), F32)],
        compiler_params=_cparams(("parallel",)),
        name="s_conv",
    )(p, p, p, w, bias)


def _ssd_step(a_ref, dtb_ref, uf_ref, dtf_ref, ub_ref, dtbw_ref, yf_ref, yb_ref, h_sc):
    qn = S_CHUNK
    ii = lax.broadcasted_iota(jnp.int32, (qn, qn), 0)
    jj = lax.broadcasted_iota(jnp.int32, (qn, qn), 1)
    tri = jnp.where(jj <= ii, 1.0, 0.0).astype(BF)
    hg = S_HEADS // S_GROUPS
    dirs = []
    for d, (u_ref, dt_ref) in enumerate(((uf_ref, dtf_ref), (ub_ref, dtbw_ref))):
        xr = dt_ref[...] + dtb_ref[...]
        dt = jnp.maximum(xr, 0.0) + jnp.log1p(jnp.exp(-jnp.abs(xr)))
        dta = dt * a_ref[...]
        hi = dta.astype(BF)
        lo = (dta - hi.astype(F32)).astype(BF)
        lo2 = (dta - hi.astype(F32) - lo.astype(F32)).astype(BF)
        pfx = (jnp.dot(tri, hi, preferred_element_type=F32)
               + jnp.dot(tri, lo, preferred_element_type=F32)
               + jnp.dot(tri, lo2, preferred_element_type=F32))
        total = pfx[qn - 1:qn, :]
        cum = (total - pfx + dta) if d else pfx
        mask = ((jj - ii) if d else (ii - jj)) >= 0
        dirs.append((u_ref, dt, cum, cum.T, total, mask, h_sc[d]))
    lane_lo = lax.broadcasted_iota(jnp.int32, (1, 128), 1) < S_HEAD_DIM
    ys = ([], [])
    hs = ([], [])
    for g in range(S_GROUPS):
        grp = []
        for d, (u_ref, dt, cum, cum_t, total, mask, hprev) in enumerate(dirs):
            bg = u_ref[:, 512 + g * 128:512 + (g + 1) * 128]
            cg = u_ref[:, 768 + g * 128:768 + (g + 1) * 128]
            cb = _dot_nt(cg, bg)
            yoff = _dot(cg, hprev[:, g * hg * 64:(g + 1) * hg * 64])
            grp.append((cb, bg.T, yoff))
        for pr in range(hg // 2):
            h0 = g * hg + 2 * pr
            for d, (u_ref, dt, cum, cum_t, total, mask, hprev) in enumerate(dirs):
                cb, bgt, yoff = grp[d]
                ws, cbs, dbs, tots = [], [], [], []
                for h in (h0, h0 + 1):
                    ln = d * S_HEADS + h
                    ccol_b = jnp.broadcast_to(cum[:, ln:ln + 1], (qn, 128))
                    ws.append(cb * jnp.exp(jnp.where(mask, ccol_b - cum_t[ln:ln + 1, :], NEG_INF)))
                    cbs.append(ccol_b)
                    dbs.append(jnp.broadcast_to(dt[:, ln:ln + 1], (qn, 128)))
                    tots.append(total[:, ln:ln + 1])
                csel = jnp.where(lane_lo, cbs[0], cbs[1])
                tsel = jnp.where(lane_lo, tots[0], tots[1])
                xdt = u_ref[:, h0 * 64:h0 * 64 + 128] * jnp.where(lane_lo, dbs[0], dbs[1])
                rhs = jnp.concatenate([jnp.where(lane_lo, xdt, 0.0), jnp.where(lane_lo, 0.0, xdt)],
                                      axis=0)
                y = (_dot(jnp.concatenate(ws, axis=1), rhs)
                     + yoff[:, pr * 128:(pr + 1) * 128] * jnp.exp(csel))
                hnew = (jnp.exp(tsel) * hprev[:, h0 * 64:h0 * 64 + 128]
                        + _dot(bgt, xdt * jnp.exp(tsel - csel)))
                ys[d].append(y)
                hs[d].append(hnew)
    for d, y_ref in enumerate((yf_ref, yb_ref)):
        h_sc[d] = jnp.concatenate(hs[d], axis=1)
        y_ref[...] = jnp.concatenate(ys[d], axis=1).astype(y_ref.dtype)


def _scan_finish_kernel(rf_ref, rb_ref, g_ref, rnw_ref, sf_ref, sb_ref, x_ref, z_ref,
                        dsk_ref, snw_ref, yb_ref, yc_ref):
    tot = rf_ref[...].astype(F32) + rb_ref[...].astype(F32)
    ys = [_rms(tot[:, h * 128:(h + 1) * 128], R_V_DIM) for h in range(R_HEADS)]
    y = jnp.concatenate(ys, axis=1) * rnw_ref[...]
    yb_ref[...] = (y * _silu(g_ref[...].astype(F32))).astype(yb_ref.dtype)
    yt = (sf_ref[...].astype(F32) + sb_ref[...].astype(F32)
          + dsk_ref[...] * x_ref[...].astype(F32)) * _silu(z_ref[...].astype(F32))
    yc_ref[...] = (_rms(yt, yt.shape[-1]) * snw_ref[...]).astype(yc_ref.dtype)


def _scan_finish(rf, rb, p, rnw, sf, sb, u, dsk, snw, rows, tm):
    blk = lambda col: pl.BlockSpec((tm, 512), lambda i: (i, col))
    const = pl.BlockSpec((1, 512), lambda i: (0, 0))
    return pl.pallas_call(
        _scan_finish_kernel,
        grid=(rows // tm,),
        in_specs=[blk(0), blk(0), blk((OFF_R + 1024) // 512), const,
                  blk(0), blk(0), blk(0), blk(OFF_S // 512), const, const],
        out_specs=[blk(0), blk(0)],
        out_shape=[jax.ShapeDtypeStruct((rows, 512), BF)] * 2,
        compiler_params=_cparams(("parallel",)),
        name="scan_finish",
    )(rf, rb, p, rnw, sf, sb, u, p, dsk, snw)


def _mprep_kernel(p_ref, cos_ref, slo_ref, shi_ref, cqw_ref, ckvw_ref, wq_ref, wkv_ref,
                  qw_ref, kw_ref, q_ref, k_ref, v_ref):
    cos, slo, shi = cos_ref[...], slo_ref[...], shi_ref[...]
    scale = M_QK_DIM ** -0.5 * math.log2(math.e)
    cq = _rms(p_ref[:, 0:512].astype(F32), M_Q_RANK) * cqw_ref[...]
    q = _dot(cq, wq_ref[...])
    ckv = _rms(p_ref[:, 512:640].astype(F32), M_KV_RANK) * ckvw_ref[...]
    kv = _dot(ckv, wkv_ref[...])
    kr = p_ref[:, 640:768].astype(F32)
    kr_ss = jnp.sum(kr * kr, axis=-1, keepdims=True)
    for h in range(M_HEADS):
        c0 = q[:, h * 256:h * 256 + 128]
        c1 = q[:, h * 256 + 128:h * 256 + 256]
        r = lax.rsqrt((jnp.sum(c0 * c0, axis=-1, keepdims=True)
                       + jnp.sum(c1 * c1, axis=-1, keepdims=True)) * (1.0 / M_QK_DIM) + EPS)
        q_ref[:, h * 256:h * 256 + 128] = (c0 * r * qw_ref[:, 0:128] * scale).astype(BF)
        q_ref[:, h * 256 + 128:h * 256 + 256] = (
            _rope(c1 * r * qw_ref[:, 128:256], cos, slo, shi, 16) * scale).astype(BF)
        k0 = kv[:, h * 128:(h + 1) * 128]
        r = lax.rsqrt((jnp.sum(k0 * k0, axis=-1, keepdims=True) + kr_ss) * (1.0 / M_QK_DIM) + EPS)
        k_ref[:, h * 256:h * 256 + 128] = (k0 * r * kw_ref[:, 0:128]).astype(BF)
        k_ref[:, h * 256 + 128:h * 256 + 256] = (
            _rope(kr * r * kw_ref[:, 128:256], cos, slo, shi, 16)).astype(BF)
    tk = v_ref.shape[2]
    ones = jnp.ones((MV_ROWS - M_V, tk), BF)
    for r in range(v_ref.shape[0]):
        for h in range(M_HEADS):
            vt = kv[r * tk:(r + 1) * tk, 512 + h * M_V:512 + (h + 1) * M_V].T
            v_ref[r, h * MV_ROWS:h * MV_ROWS + M_V, :] = vt.astype(BF)
            v_ref[r, h * MV_ROWS + M_V:(h + 1) * MV_ROWS, :] = ones


def _mprep(p, tabs, cqw, ckvw, wq, wkv, qw, kw, dims, tm, tk):
    b, s, lc, nl, t = dims
    assert tm % tk == 0
    tab_spec = pl.BlockSpec((tm, 128), lambda i: (_table_index(i, tm, nl, s), 0))
    const = lambda shape: pl.BlockSpec(shape, lambda i: (0, 0))
    return pl.pallas_call(
        _mprep_kernel,
        grid=(t // tm,),
        in_specs=[pl.BlockSpec((tm, W_M), lambda i: (i, OFF_M // W_M)),
                  tab_spec, tab_spec, tab_spec,
                  const((1, 512)), const((1, 128)), const((512, 1024)), const((128, 1024)),
                  const((1, 256)), const((1, 256))],
        out_specs=[pl.BlockSpec((tm, 1024), lambda i: (i, 0)),
                   pl.BlockSpec((tm, 1024), lambda i: (i, 0)),
                   pl.BlockSpec((tm // tk, M_HEADS * MV_ROWS, tk), lambda i: (i, 0, 0))],
        out_shape=[jax.ShapeDtypeStruct((t, 1024), BF),
                   jax.ShapeDtypeStruct((t, 1024), BF),
                   jax.ShapeDtypeStruct((t // tk, M_HEADS * MV_ROWS, tk), BF)],
        compiler_params=_cparams(("parallel",)),
        name="m_prep",
    )(p, *tabs, cqw, ckvw, wq, wkv, qw, kw)


def _mla_kernel(q_ref, kx_ref, vx_ref, kl_ref, vl_ref, o_ref, acc_sc, s_sc, *, nq, tk):
    qi = pl.program_id(1)
    n_tiles = kl_ref.shape[0] // tk

    def scores(h, k_tile):
        return _dot_nt(k_tile[:, h * 256:(h + 1) * 256], q_ref[:, h * 256:(h + 1) * 256])

    def pv(h, vt_tile, p):
        return jnp.dot(vt_tile[h * MV_ROWS:(h + 1) * MV_ROWS, :], p.astype(BF),
                       preferred_element_type=F32)

    def k_tile(kt):
        return kl_ref.at[pl.ds(pl.multiple_of(kt * tk, tk), tk), :]

    s_ctx = [scores(h, kx_ref) for h in range(M_HEADS)]
    for h in range(M_HEADS):
        s_sc[h] = scores(h, k_tile(0))
    ms = []
    for h in range(M_HEADS):
        s = s_ctx[h]
        m0 = jnp.max(s, axis=0, keepdims=True)
        ms.append(m0)
        acc_sc[h] = pv(h, vx_ref, jnp.exp2(s - m0))

    def body(kt, ms):
        nxt = k_tile(jnp.minimum(kt + 1, n_tiles - 1))
        vt_tile = vl_ref.at[kt]
        ms_new = []
        s_cur = [s_sc[h] for h in range(M_HEADS)]
        s_nxt = [scores(h, nxt) for h in range(M_HEADS)]
        for h in range(M_HEADS):
            s = s_cur[h]
            m_new = jnp.maximum(ms[h], jnp.max(s, axis=0, keepdims=True))
            alpha = jnp.exp2(ms[h] - m_new)
            ms_new.append(m_new)
            acc_sc[h] = alpha * acc_sc[h] + pv(h, vt_tile, jnp.exp2(s - m_new))
        for h in range(M_HEADS):
            s_sc[h] = s_nxt[h]
        return tuple(ms_new)

    def finish():
        for h in range(M_HEADS):
            o = acc_sc[h, 0:M_V, :] * (1.0 / acc_sc[h, M_V:M_V + 1, :])
            o_ref[:, h * 128:(h + 1) * 128] = o.T.astype(o_ref.dtype)

    @pl.when(qi < nq)
    def _():
        lax.fori_loop(0, n_tiles, body, tuple(ms), unroll=4)
        finish()

    @pl.when(qi >= nq)
    def _():
        finish()


def _mla(qm, km, vmt, dims, need_ctx, tq, tk):
    b, s, lc, nl, t = dims
    nq = s // tq
    assert lc == tq and lc == tk and vmt.shape[2] == tk
    steps = nq + (1 if need_ctx else 0)
    rows = nl + (b * lc if need_ctx else 0)

    def qrow(bi, qi):
        return jnp.where(qi < nq, bi * nq + qi, nl // tq + bi)

    return pl.pallas_call(
        functools.partial(_mla_kernel, nq=nq, tk=tk),
        grid=(b, steps),
        in_specs=[pl.BlockSpec((tq, 1024), lambda bi, qi: (qrow(bi, qi), 0)),
                  pl.BlockSpec((lc, 1024), lambda bi, qi: (nl // lc + bi, 0)),
                  pl.BlockSpec((None, M_HEADS * MV_ROWS, tk), lambda bi, qi: (nl // tk + bi, 0, 0)),
                  pl.BlockSpec((s, 1024), lambda bi, qi: (bi, 0)),
                  pl.BlockSpec((s // tk, M_HEADS * MV_ROWS, tk), lambda bi, qi: (bi, 0, 0))],
        out_specs=pl.BlockSpec((tq, 512), lambda bi, qi: (qrow(bi, qi), 0)),
        out_shape=jax.ShapeDtypeStruct((rows, 512), BF),
        scratch_shapes=[pltpu.VMEM((M_HEADS, MV_ROWS, tq), F32),
                        pltpu.VMEM((M_HEADS, tk, tq), F32)],
        compiler_params=_cparams(("parallel", "arbitrary")),
        name="mla",
    )(qm, km, vmt, km, vmt)


def _merge_kernel(x_ref, mod_ref, ya_ref, yb_ref, yc_ref, yd_ref, g0_ref, g1_ref, g2_ref, g3_ref,
                  wb_ref, wo_ref, o_ref, acc_sc, *, tn):
    j = pl.program_id(1)
    acc = None
    for kk, (y_ref, g_ref) in enumerate(((ya_ref, g0_ref), (yb_ref, g1_ref),
                                         (yc_ref, g2_ref), (yd_ref, g3_ref))):
        gate = 1.0 / (1.0 + jnp.exp(-g_ref[...].astype(F32)))
        term = gate * jnp.dot(y_ref[...], wb_ref[kk], preferred_element_type=F32)
        acc = term if acc is None else acc + term
    acc_sc[:, pl.ds(pl.multiple_of(j * tn, tn), tn)] = acc.astype(BF)

    @pl.when(j == pl.num_programs(1) - 1)
    def _():
        o_ref[...] = x_ref[...] + mod_ref[2:3, :] * jnp.dot(
            acc_sc[...], wo_ref[...], preferred_element_type=F32)


def _merge(xs, mod, ys, p, wb, wo, layer, dims, rows, tm, tn):
    b, s, lc, nl, t = dims
    d = xs.shape[1]
    nj = d // tn
    goff = OFF_G // tn
    yspec = pl.BlockSpec((tm, BRANCH_W), lambda i, j: (i, 0))
    gspecs = [pl.BlockSpec((tm, tn), functools.partial(lambda i, j, kk: (i, goff + kk * nj + j), kk=kk))
              for kk in range(N_BRANCH)]
    return pl.pallas_call(
        functools.partial(_merge_kernel, tn=tn),
        grid=(rows // tm, nj),
        in_specs=[pl.BlockSpec((tm, d), lambda i, j: (i, 0)),
                  pl.BlockSpec((None, N_MOD, d), lambda i, j: (_mod_index(i, tm, nl, s, b), 0, 0)),
                  yspec, yspec, yspec, yspec, *gspecs,
                  pl.BlockSpec((None, N_BRANCH, BRANCH_W, tn), lambda i, j: (layer, 0, 0, j)),
                  pl.BlockSpec((None, d, d), lambda i, j: (layer, 0, 0))],
        out_specs=pl.BlockSpec((tm, d), lambda i, j: (i, 0)),
        out_shape=jax.ShapeDtypeStruct((rows, d), F32),
        scratch_shapes=[pltpu.VMEM((tm, d), BF)],
        compiler_params=_cparams(("parallel", "arbitrary")),
        name="merge",
    )(xs, mod, *ys, p, p, p, p, wb, wo)


def _ffn_kernel(x_ref, mod_ref, xn_ref, modn_ref, nw_ref, w1_ref, w2_ref, o_ref, h_sc):
    i = pl.program_id(0)
    f = pl.program_id(1)
    last = pl.num_programs(1) - 1
    slot = i % 2

    def normed(xr, mr):
        x = xr[...]
        y = _rms(x, x.shape[-1]) * nw_ref[...]
        return (y * (1.0 + mr[4:5, :]) + mr[3:4, :]).astype(BF)

    def mlp_part():
        u = jnp.maximum(jnp.dot(h_sc[slot], w1_ref[...], preferred_element_type=F32), 0.0)
        return jnp.dot((u * u).astype(BF), w2_ref[...], preferred_element_type=F32)

    @pl.when(jnp.logical_and(i == 0, f == 0))
    def _():
        h_sc[0] = normed(x_ref, mod_ref)

    @pl.when(f == 0)
    def _():
        o_ref[...] = mlp_part()

    @pl.when(jnp.logical_and(f > 0, f < last))
    def _():
        o_ref[...] += mlp_part()

    @pl.when(f == last)
    def _():
        part = mlp_part()
        h_sc[1 - slot] = normed(xn_ref, modn_ref)
        o_ref[...] = x_ref[...] + mod_ref[5:6, :] * (o_ref[...] + part)


def _ffn(xs, mod, nw, w1, w2, layer, dims, rows, tm, tf):
    b, s, lc, nl, t = dims
    d = xs.shape[1]
    dff = w1.shape[2]
    nblk = rows // tm
    assert dff // tf >= 2
    nxt = lambda i: jnp.minimum(i + 1, nblk - 1)
    return pl.pallas_call(
        _ffn_kernel,
        grid=(nblk, dff // tf),
        in_specs=[pl.BlockSpec((tm, d), lambda i, f: (i, 0)),
                  pl.BlockSpec((None, N_MOD, d), lambda i, f: (_mod_index(i, tm, nl, s, b), 0, 0)),
                  pl.BlockSpec((tm, d), lambda i, f: (nxt(i), 0)),
                  pl.BlockSpec((None, N_MOD, d),
                               lambda i, f: (_mod_index(nxt(i), tm, nl, s, b), 0, 0)),
                  pl.BlockSpec((1, d), lambda i, f: (0, 0)),
                  pl.BlockSpec((None, d, tf), lambda i, f: (layer, 0, f)),
                  pl.BlockSpec((None, tf, d), lambda i, f: (layer, f, 0))],
        out_specs=pl.BlockSpec((tm, d), lambda i, f: (i, 0)),
        out_shape=jax.ShapeDtypeStruct((rows, d), F32),
        scratch_shapes=[pltpu.VMEM((2, tm, d), BF)],
        compiler_params=_cparams(("arbitrary", "arbitrary")),
        name="ffn",
    )(xs, mod, xs, mod, nw, w1, w2)


def _rope_tables(pos_per_lane, freq_per_lane, lo_lane, pad_rows, live_lane=None):
    ang = pos_per_lane * freq_per_lane[None, :]
    cos, sin = jnp.cos(ang), jnp.sin(ang)
    if live_lane is not None:
        cos = jnp.where(live_lane[None, :], cos, 1.0)
        sin = jnp.where(live_lane[None, :], sin, 0.0)
    s_lo = jnp.where(lo_lane[None, :], -sin, 0.0)
    s_hi = jnp.where(lo_lane[None, :], 0.0, sin)
    w = cos.shape[1]
    ident = (jnp.ones((pad_rows, w), F32), jnp.zeros((pad_rows, w), F32), jnp.zeros((pad_rows, w), F32))
    return tuple(jnp.concatenate([tb.astype(F32), idn], axis=0)
                 for tb, idn in zip((cos, s_lo, s_hi), ident))


def _tables(s, pad_rows):
    t = jnp.arange(s)
    row = (t // GRID_W).astype(F32)[:, None]
    col = (t % GRID_W).astype(F32)[:, None]
    tpos = t.astype(F32)[:, None]
    lane = jnp.arange(128)
    fa = ROPE_BASE ** (-(lane % 32).astype(F32) / 32)
    pos_a = jnp.where((lane < 64)[None, :], row, col)
    tab_a = _rope_tables(pos_a, fa, (lane % 64) < 32, pad_rows)
    lane = jnp.arange(256)
    fb = ROPE_BASE ** (-(lane % 32).astype(F32) / 32)
    tab_r = _rope_tables(jnp.broadcast_to(tpos, (s, 256)), fb, (lane % 64) < 32, pad_rows)
    lane = jnp.arange(128)
    fm = ROPE_BASE ** (-(lane % 16).astype(F32) / 16)
    pos_m = jnp.where((lane < 32)[None, :], row, col)
    tab_m = _rope_tables(pos_m, fm, (lane % 32) < 16, pad_rows, live_lane=lane < 64)
    return tab_a, tab_r, tab_m


def _pad_cols(w, n):
    return jnp.pad(w, [(0, 0)] * (w.ndim - 1) + [(0, n - w.shape[-1])])


def kernel(x, c, ctx, c_ctx, w_ada, b_ada, norm1_w, norm2_w, w_in, a_q_norm, a_k_norm, a_sink,
           r_decay, r_norm, s_conv_w, s_conv_b, s_a_log, s_dt_bias, s_d, s_norm,
           m_cq_norm, m_ckv_norm, m_w_uq, m_w_ukv, m_q_norm, m_k_norm,
           w_branch, w_o, w_ff1, w_ff2):
    b, s, d = x.shape
    lc = ctx.shape[1]
    depth = w_ada.shape[0]
    nl, ncx = b * s, b * lc
    t = nl + ncx
    dims = (b, s, lc, nl, t)
    tm_big = max(m for m in (1024, 512, 256) if s % m == 0 and ncx % m == 0)
    tm = max(m for m in (512, 256) if s % m == 0 and ncx % m == 0)
    tc = lc
    tn_in, tn_merge, tf = 1024, 512, 1024
    assert lc == 256 and s % 256 == 0 and b + 1 <= 8

    w_in16 = w_in.astype(BF)
    w_mix = jnp.concatenate([_pad_cols(w_in16[..., 4112:4816], W_M),
                             _pad_cols(w_in16[..., 4096:4112], W_DT), w_in16[..., 0:4096]], axis=-1)
    w_gate = w_in16[..., 4816:]
    wq = _pad_cols(m_w_uq.reshape(depth, M_Q_RANK, M_HEADS, M_QK_DIM), 256)
    wq = wq.reshape(depth, M_Q_RANK, M_HEADS * 256).astype(BF)
    wkv = m_w_ukv.reshape(depth, M_KV_RANK, M_HEADS, M_NOPE + M_V)
    wkv = jnp.concatenate([wkv[..., :M_NOPE].reshape(depth, M_KV_RANK, M_HEADS * M_NOPE),
                           wkv[..., M_NOPE:].reshape(depth, M_KV_RANK, M_HEADS * M_V)],
                          axis=-1).astype(BF)
    wb = w_branch.astype(BF)
    wo = w_o.astype(BF)
    w1 = w_ff1.astype(BF)
    w2 = w_ff2.astype(BF)
    log_gamma = jnp.log1p(-jnp.exp(r_decay.astype(F32)))
    a_neg = _pad_cols(-jnp.exp(s_a_log.astype(F32)).reshape(depth, 1, 2 * S_HEADS), 128)
    dtb = _pad_cols(s_dt_bias.astype(F32).reshape(depth, 1, 2 * S_HEADS), 128)
    dsk = jnp.repeat(s_d.astype(F32), S_HEAD_DIM, axis=-1)[:, None, :]
    mqw = _pad_cols(m_q_norm, 256)[:, None, :]
    mkw = _pad_cols(m_k_norm, 256)[:, None, :]
    tab_a, tab_r, tab_m = _tables(s, tm)

    c8 = jnp.zeros((8, d), F32).at[:b].set(c).at[b].set(c_ctx)
    mod_all = _ada(c8, w_ada, b_ada).reshape(depth, 8, N_MOD, d)

    xs = jnp.concatenate([x.reshape(nl, d), ctx.reshape(ncx, d)], axis=0)
    for i in range(depth):
        need_ctx = i < depth - 1
        rows = t if need_ctx else nl
        mod = mod_all[i]
        p, pdt = _inproj(xs, mod, norm1_w[i][None, :], w_mix, w_gate, i, dims, tm_big, tn_in)
        qa, ka, va = _aprep(p, tab_a, a_q_norm[i][None, :], a_k_norm[i][None, :], dims, tm)
        ya = _aattn(a_sink[i].astype(F32), qa, ka, va, dims, need_ctx)
        u = _conv(p, s_conv_w[i], s_conv_b[i][None, :], dims, tc)
        rf, rb, sf, sb = _scans(log_gamma[i], p, tab_r, a_neg[i], dtb[i], u, pdt, dims)
        yb, yc = _scan_finish(rf, rb, p, r_norm[i].reshape(1, -1), sf, sb, u, dsk[i],
                              s_norm[i][None, :], rows, tm)
        qm, km, vm = _mprep(p, tab_m, m_cq_norm[i][None, :],
                            m_ckv_norm[i][None, :], wq[i], wkv[i], mqw[i], mkw[i], dims, tm, tc)
        yd = _mla(qm, km, vm, dims, need_ctx, tc, tc)
        xs = _merge(xs, mod, (ya, yb, yc, yd), p, wb, wo, i, dims, rows, tm, tn_merge)
        xs = _ffn(xs, mod, norm2_w[i][None, :], w1, w2, i, dims, rows, tm, tf)
    return xs.reshape(b, s, d)
```

```python
import functools
import math

import jax
import jax.numpy as jnp
from jax import lax
from jax.experimental import pallas as pl
from jax.experimental.pallas import tpu as pltpu

GRID_W = 64
EPS = 1e-6
ROPE_BASE = 10000.0
NEG_INF = -1e30

A_HEADS, A_KV_HEADS, A_HEAD_DIM, A_BLOCK = 4, 2, 128, 128
R_HEADS, R_QK_DIM, R_V_DIM, R_CHUNK = 4, 64, 128, 128
S_HEADS, S_HEAD_DIM, S_GROUPS, S_STATE, S_CONV, S_CHUNK = 8, 64, 2, 128, 5, 128
M_HEADS, M_Q_RANK, M_KV_RANK, M_NOPE, M_ROPE, M_V = 4, 512, 128, 128, 64, 128
M_QK_DIM = M_NOPE + M_ROPE
MV_ROWS = M_V + 16
N_BRANCH, N_MOD = 4, 6
BRANCH_W = 512

OFF_M, W_M = 0, 768
OFF_DT, W_DT = 768, 256
OFF_A, W_A = 1024, 1024
OFF_R, W_R = 2048, 1536
OFF_S, W_S = 3584, 1536
OFF_G = 5120

VMEM_LIMIT = 56 * 1024 * 1024

BF = jnp.bfloat16
F32 = jnp.float32


def _cparams(sem):
    return pltpu.CompilerParams(dimension_semantics=sem, vmem_limit_bytes=VMEM_LIMIT)


def _dot(a, b):
    return jnp.dot(a.astype(BF), b.astype(BF), preferred_element_type=F32)


def _dot_nt(a, b):
    return lax.dot_general(a.astype(BF), b.astype(BF), (((1,), (1,)), ((), ())),
                           preferred_element_type=F32)


def _silu(x):
    return x * (1.0 / (1.0 + jnp.exp(-x)))


def _rms(x, n):
    return x * lax.rsqrt(jnp.sum(x * x, axis=-1, keepdims=True) * (1.0 / n) + EPS)


def _rope(y, cos, s_lo, s_hi, shift):
    w = y.shape[-1]
    return y * cos + pltpu.roll(y, w - shift, 1) * s_lo + pltpu.roll(y, shift, 1) * s_hi


def _ada_kernel(c_ref, w_ref, b_ref, o_ref):
    o_ref[...] = _dot(_silu(c_ref[...]), w_ref[...]) + b_ref[...]


def _ada(c8, w_ada, b_ada):
    depth, d, n = w_ada.shape
    tn = 512
    return pl.pallas_call(
        _ada_kernel,
        grid=(depth, n // tn),
        in_specs=[pl.BlockSpec((8, d), lambda l, j: (0, 0)),
                  pl.BlockSpec((None, d, tn), lambda l, j: (l, 0, j)),
                  pl.BlockSpec((None, 1, tn), lambda l, j: (l, 0, j))],
        out_specs=pl.BlockSpec((None, 8, tn), lambda l, j: (l, 0, j)),
        out_shape=jax.ShapeDtypeStruct((depth, 8, n), F32),
        compiler_params=_cparams(("parallel", "parallel")),
        name="ada",
    )(c8, w_ada, b_ada.reshape(depth, 1, n))


def _inproj_kernel(x_ref, mod_ref, nw_ref, wm_ref, wg_ref, o_ref, dt_ref, h_sc, *, nm):
    i = pl.program_id(0)
    j = pl.program_id(1)
    last = pl.num_programs(1) - 1
    slot = i % 2

    def normed():
        x = x_ref[...]
        y = _rms(x, x.shape[-1]) * nw_ref[...]
        return (y * (1.0 + mod_ref[1:2, :]) + mod_ref[0:1, :]).astype(BF)

    def tile(w_ref):
        return jnp.dot(h_sc[slot], w_ref[...], preferred_element_type=F32)

    @pl.when(jnp.logical_and(i == 0, j == 0))
    def _():
        h_sc[0] = normed()

    @pl.when(j == 0)
    def _():
        r = tile(wm_ref)
        o_ref[...] = r.astype(o_ref.dtype)
        dt_ref[...] = r[:, OFF_DT:OFF_DT + W_DT]

    @pl.when(jnp.logical_and(j > 0, j < nm))
    def _():
        o_ref[...] = tile(wm_ref).astype(o_ref.dtype)

    @pl.when(jnp.logical_and(j >= nm, j < last))
    def _():
        o_ref[...] = tile(wg_ref).astype(o_ref.dtype)

    @pl.when(j == last)
    def _():
        o_ref[...] = tile(wg_ref).astype(o_ref.dtype)
        h_sc[1 - slot] = normed()


def _mod_index(i, tm, nl, s, b):
    return jnp.where(i < nl // tm, i // (s // tm), b)


def _inproj(xs, mod, nw, wm, wg, layer, dims, tm, tn):
    b, s, lc, nl, t = dims
    d = xs.shape[1]
    nm, ng = wm.shape[2] // tn, wg.shape[2] // tn
    nblk = t // tm
    assert OFF_DT + W_DT <= tn and wm.shape[2] == OFF_G and nm >= 2 and ng >= 2

    def ahead(i, j):
        return jnp.where(jnp.logical_and(i == 0, j == 0), 0, jnp.minimum(i + 1, nblk - 1))

    return pl.pallas_call(
        functools.partial(_inproj_kernel, nm=nm),
        grid=(nblk, nm + ng),
        in_specs=[pl.BlockSpec((tm, d), lambda i, j: (ahead(i, j), 0)),
                  pl.BlockSpec((None, N_MOD, d),
                               lambda i, j: (_mod_index(ahead(i, j), tm, nl, s, b), 0, 0)),
                  pl.BlockSpec((1, d), lambda i, j: (0, 0)),
                  pl.BlockSpec((None, d, tn), lambda i, j: (layer, 0, jnp.minimum(j, nm - 1))),
                  pl.BlockSpec((None, d, tn), lambda i, j: (layer, 0, jnp.maximum(j - nm, 0)))],
        out_specs=[pl.BlockSpec((tm, tn), lambda i, j: (i, j)),
                   pl.BlockSpec((tm, W_DT), lambda i, j: (i, 0))],
        out_shape=[jax.ShapeDtypeStruct((t, (nm + ng) * tn), BF),
                   jax.ShapeDtypeStruct((t, W_DT), F32)],
        scratch_shapes=[pltpu.VMEM((2, tm, d), BF)],
        compiler_params=_cparams(("arbitrary", "arbitrary")),
        name="inproj",
    )(xs, mod, nw, wm, wg)


def _aprep_kernel(p_ref, cos_ref, slo_ref, shi_ref, qw_ref, kw_ref, q_ref, k_ref, v_ref):
    cos, slo, shi = cos_ref[...], slo_ref[...], shi_ref[...]
    scale = A_HEAD_DIM ** -0.5
    for h in range(A_HEADS):
        y = _rms(p_ref[:, h * 128:(h + 1) * 128].astype(F32), A_HEAD_DIM) * qw_ref[...]
        q_ref[:, h * 128:(h + 1) * 128] = (_rope(y, cos, slo, shi, 32) * scale).astype(BF)
    for h in range(A_KV_HEADS):
        y = _rms(p_ref[:, 512 + h * 128:512 + (h + 1) * 128].astype(F32), A_HEAD_DIM) * kw_ref[...]
        k_ref[:, h * 128:(h + 1) * 128] = _rope(y, cos, slo, shi, 32).astype(BF)
    for r in range(v_ref.shape[0]):
        v_ref[r] = p_ref[r * A_BLOCK:(r + 1) * A_BLOCK, 768:1024].astype(F32).T.astype(BF)


def _table_index(i, tm, nl, s):
    return jnp.where(i < nl // tm, i % (s // tm), s // tm)


def _aprep(p, tabs, qw, kw, dims, tm):
    b, s, lc, nl, t = dims
    tab_spec = pl.BlockSpec((tm, 128), lambda i: (_table_index(i, tm, nl, s), 0))
    return pl.pallas_call(
        _aprep_kernel,
        grid=(t // tm,),
        in_specs=[pl.BlockSpec((tm, W_A), lambda i: (i, OFF_A // W_A)),
                  tab_spec, tab_spec, tab_spec,
                  pl.BlockSpec((1, 128), lambda i: (0, 0)),
                  pl.BlockSpec((1, 128), lambda i: (0, 0))],
        out_specs=[pl.BlockSpec((tm, 512), lambda i: (i, 0)),
                   pl.BlockSpec((tm, 256), lambda i: (i, 0)),
                   pl.BlockSpec((tm // A_BLOCK, 256, A_BLOCK), lambda i: (i, 0, 0))],
        out_shape=[jax.ShapeDtypeStruct((t, 512), BF),
                   jax.ShapeDtypeStruct((t, 256), BF),
                   jax.ShapeDtypeStruct((t // A_BLOCK, 256, A_BLOCK), BF)],
        compiler_params=_cparams(("parallel",)),
        name="a_prep",
    )(p, *tabs, qw, kw)


def _aattn_kernel(sink_ref, q_ref, kp_ref, kc_ref, kn_ref, vp_ref, vc_ref, vn_ref,
                  kx_ref, vx_ref, o_ref, *, nb):
    n = pl.program_id(1)
    blk = A_BLOCK
    lc = kx_ref.shape[0]
    g = A_HEADS // A_KV_HEADS
    nk = 3 * blk + lc
    kj = lax.broadcasted_iota(jnp.int32, (nk, g * blk), 0)
    qi = lax.broadcasted_iota(jnp.int32, (nk, g * blk), 1) % blk
    lo_seq = jnp.where(n == 0, blk, 0)
    hi_seq = jnp.where(n < nb, jnp.where(n == nb - 1, 2 * blk - 1, 3 * blk - 1), -1)
    mask = ((kj >= jnp.maximum(qi, lo_seq)) & (kj <= jnp.minimum(qi + 2 * blk, hi_seq))) | (kj >= 3 * blk)
    lane = lax.broadcasted_iota(jnp.int32, (1, g * blk), 1)
    ss = []
    for hk in range(A_KV_HEADS):
        sl = slice(hk * 128, (hk + 1) * 128)
        q2 = jnp.concatenate([q_ref[:, (hk * g + gi) * 128:(hk * g + gi + 1) * 128]
                              for gi in range(g)], axis=0)
        kk = jnp.concatenate([kp_ref[:, sl], kc_ref[:, sl], kn_ref[:, sl], kx_ref[:, sl]], axis=0)
        ss.append(_dot_nt(kk, q2))
    for hk in range(A_KV_HEADS):
        sl = slice(hk * 128, (hk + 1) * 128)
        vvt = jnp.concatenate([vp_ref[sl, :], vc_ref[sl, :], vn_ref[sl, :]]
                              + [vx_ref[r, sl, :] for r in range(lc // blk)], axis=1)
        s = jnp.where(mask, ss[hk], NEG_INF)
        sink = jnp.where(lane < blk, sink_ref[hk * g], sink_ref[hk * g + 1])
        m = jnp.maximum(jnp.max(s, axis=0, keepdims=True), sink)
        p = jnp.exp(s - m)
        inv = 1.0 / (jnp.sum(p, axis=0, keepdims=True) + jnp.exp(sink - m))
        o = (jnp.dot(vvt, p.astype(BF), preferred_element_type=F32) * inv).T
        for gi in range(g):
            o_ref[:, (hk * g + gi) * 128:(hk * g + gi + 1) * 128] = (
                o[gi * blk:(gi + 1) * blk]).astype(o_ref.dtype)


def _aattn(sink, qa, ka, va, dims, need_ctx):
    b, s, lc, nl, t = dims
    blk = A_BLOCK
    nb = s // blk
    ncb = lc // blk
    steps = nb + (ncb if need_ctx else 0)
    rows = nl + (b * lc if need_ctx else 0)

    def qrow(bi, n):
        return jnp.where(n < nb, bi * nb + n, nl // blk + bi * ncb + (n - nb))

    def krow(off):
        def f(bi, n):
            return (bi * nb + jnp.clip(n + off, 0, nb - 1), 0)
        return f

    kspec = [pl.BlockSpec((blk, 256), krow(o)) for o in (-1, 0, 1)]
    vspec = [pl.BlockSpec((None, 256, blk), lambda bi, n, o=o: (*krow(o)(bi, n), 0))
             for o in (-1, 0, 1)]
    xspec = pl.BlockSpec((lc, 256), lambda bi, n: (nl // lc + bi, 0))
    vxspec = pl.BlockSpec((ncb, 256, blk), lambda bi, n: (nl // lc + bi, 0, 0))
    return pl.pallas_call(
        functools.partial(_aattn_kernel, nb=nb),
        grid=(b, steps),
        in_specs=[pl.BlockSpec(memory_space=pltpu.SMEM),
                  pl.BlockSpec((blk, 512), lambda bi, n: (qrow(bi, n), 0)),
                  *kspec, *vspec, xspec, vxspec],
        out_specs=pl.BlockSpec((blk, 512), lambda bi, n: (qrow(bi, n), 0)),
        out_shape=jax.ShapeDtypeStruct((rows, 512), BF),
        compiler_params=_cparams(("parallel", "arbitrary")),
        name="a_attn",
    )(sink, qa, ka, ka, ka, va, va, va, ka, va)


def _scan_slot(t, ncc, nlc, bwd):
    if not bwd:
        return t
    return jnp.where(t < ncc, ncc - 1 - t, ncc + (nlc - 1 - (t - ncc)))


def _scan_row(bi, t, ncc, nlc, nl, chunk, bwd):
    slot = _scan_slot(t, ncc, nlc, bwd)
    return jnp.where(slot < ncc, nl // chunk + bi * ncc + slot, bi * nlc + (slot - ncc))


def _ret_init(lg_ref, st_sc, dm_sc, eq_sc, ek_sc):
    c = R_CHUNK
    hd = R_QK_DIM
    nh = R_HEADS

    def per_head(idx, vals):
        out = vals[nh - 1]
        for h in range(nh - 2, -1, -1):
            out = jnp.where(idx == h, vals[h], out)
        return out

    st_sc[...] = jnp.zeros_like(st_sc)
    ii = lax.broadcasted_iota(jnp.int32, (c, c), 0)
    jj = lax.broadcasted_iota(jnp.int32, (c, c), 1)
    col = lax.broadcasted_iota(jnp.int32, (c, 1), 0)
    row = lax.broadcasted_iota(jnp.int32, (1, c), 1)
    lane_head = lax.broadcasted_iota(jnp.int32, (1, nh * hd), 1) // hd
    sub_head = lax.broadcasted_iota(jnp.int32, (nh * hd, 1), 0) // hd
    for d in range(2):
        dd = ((jj - ii) if d else (ii - jj)).astype(F32)
        pcol = ((c - 1 - col) if d else col).astype(F32)
        prow = ((c - 1 - row) if d else row).astype(F32)
        lgs = [lg_ref[d, h] for h in range(nh)]
        eq_sc[d] = jnp.exp(per_head(lane_head, lgs) * (pcol + 1.0))
        ek_sc[d] = jnp.exp(per_head(sub_head, lgs) * (c - 1.0 - prow))
        for h in range(nh):
            dm_sc[d, h] = jnp.where(dd >= 0, jnp.exp(lgs[h] * jnp.maximum(dd, 0.0)), 0.0)


def _ret_step(lg_ref, qf_ref, kf_ref, vf_ref, qb_ref, kb_ref, vb_ref,
              cosf_ref, slof_ref, shif_ref, cosb_ref, slob_ref, shib_ref,
              of_ref, ob_ref, st_sc, dm_sc, eq_sc, ek_sc):
    c = R_CHUNK
    hd = R_QK_DIM
    nh = R_HEADS
    lane_head = lax.broadcasted_iota(jnp.int32, (1, 2 * hd), 1) // hd
    dirs = []
    for d, (q_ref, k_ref, v_ref, cos_ref, slo_ref, shi_ref) in enumerate((
            (qf_ref, kf_ref, vf_ref, cosf_ref, slof_ref, shif_ref),
            (qb_ref, kb_ref, vb_ref, cosb_ref, slob_ref, shib_ref))):
        cos, slo, shi = cos_ref[...], slo_ref[...], shi_ref[...]
        q = _rope(q_ref[...].astype(F32), cos, slo, shi, 32)
        k = _rope(k_ref[...].astype(F32), cos, slo, shi, 32) * (hd ** -0.5)
        kt = k.T
        dirs.append((q, q * eq_sc[d], kt, kt * ek_sc[d], v_ref[...], st_sc[d]))
    outs = ([], [])
    new_states = ([], [])
    for h in range(nh):
        pr = h // 2
        for d, (q, qd, kt, kdt, v, state) in enumerate(dirs):
            sel = lane_head == (h % 2)
            qh = jnp.where(sel, q[:, pr * 128:(pr + 1) * 128], 0.0)
            qdh = jnp.where(sel, qd[:, pr * 128:(pr + 1) * 128], 0.0)
            vh = v[:, h * 128:(h + 1) * 128]
            s = _dot(qh, kt[pr * 128:(pr + 1) * 128, :]) * dm_sc[d, h]
            inter = _dot(qdh, state[pr * 128:(pr + 1) * 128, :])
            outs[d].append(_dot(s, vh) + inter)
            chunk_decay = jnp.exp(lg_ref[d, h] * jnp.full((1, R_V_DIM), float(c), F32))
            new_states[d].append(chunk_decay * state[h * hd:(h + 1) * hd, :]
                                 + _dot(kdt[h * hd:(h + 1) * hd, :], vh))
    for d, o_ref in enumerate((of_ref, ob_ref)):
        st_sc[d] = jnp.concatenate(new_states[d], axis=0)
        o_ref[...] = jnp.concatenate(outs[d], axis=1).astype(o_ref.dtype)


N_RET_IN, N_SSD_IN = 13, 6


def _scan_kernel(*refs):
    ret_in = refs[:N_RET_IN]
    ssd_in = refs[N_RET_IN:N_RET_IN + N_SSD_IN]
    of_ref, ob_ref, yf_ref, yb_ref, st_sc, dm_sc, eq_sc, ek_sc, h_sc = refs[N_RET_IN + N_SSD_IN:]

    @pl.when(pl.program_id(1) == 0)
    def _():
        _ret_init(ret_in[0], st_sc, dm_sc, eq_sc, ek_sc)
        h_sc[...] = jnp.zeros_like(h_sc)

    _ret_step(*ret_in, of_ref, ob_ref, st_sc, dm_sc, eq_sc, ek_sc)
    _ssd_step(*ssd_in, yf_ref, yb_ref, h_sc)


def _scans(lg, p, tabs, a_flat, dtb_flat, u, pdt, dims):
    b, s, lc, nl, t = dims
    c = R_CHUNK
    assert S_CHUNK == c
    ncc, nlc = lc // c, s // c

    def row(bwd):
        return lambda bi, st: _scan_row(bi, st, ncc, nlc, nl, c, bwd)

    def spec(width, col_off, bwd):
        r = row(bwd)
        return pl.BlockSpec((c, width), lambda bi, st: (r(bi, st), col_off // width))

    def tab_spec(bwd):
        def trow(bi, st):
            slot = _scan_slot(st, ncc, nlc, bwd)
            return (jnp.where(slot < ncc, nlc, slot - ncc), 0)
        return pl.BlockSpec((c, 256), trow)

    def qkv(bwd):
        return [spec(256, OFF_R, bwd), spec(256, OFF_R + 256, bwd), spec(512, OFF_R + 512, bwd)]

    const = pl.BlockSpec((1, 128), lambda bi, st: (0, 0))
    ret_specs = [pl.BlockSpec(memory_space=pltpu.SMEM), *qkv(False), *qkv(True),
                 *[tab_spec(False)] * 3, *[tab_spec(True)] * 3]
    ssd_specs = [const, const, spec(1024, 0, False), spec(128, 0, False),
                 spec(1024, 0, True), spec(128, 0, True)]
    assert len(ret_specs) == N_RET_IN and len(ssd_specs) == N_SSD_IN
    return pl.pallas_call(
        _scan_kernel,
        grid=(b, ncc + nlc),
        in_specs=ret_specs + ssd_specs,
        out_specs=[spec(512, 0, False), spec(512, 0, True)] * 2,
        out_shape=[jax.ShapeDtypeStruct((t, 512), BF)] * 4,
        scratch_shapes=[pltpu.VMEM((2, R_HEADS * R_QK_DIM, R_V_DIM), F32),
                        pltpu.VMEM((2, R_HEADS, c, c), F32),
                        pltpu.VMEM((2, c, R_HEADS * R_QK_DIM), F32),
                        pltpu.VMEM((2, R_HEADS * R_QK_DIM, c), F32),
                        pltpu.VMEM((2, S_STATE, S_HEADS * S_HEAD_DIM), F32)],
        compiler_params=_cparams(("parallel", "arbitrary")),
        name="scans",
    )(lg, p, p, p, p, p, p, *tabs, *tabs, a_flat, dtb_flat, u, pdt, u, pdt)


def _conv_kernel(x_ref, xp_ref, xn_ref, w_ref, b_ref, o_ref, e_sc, *, tm, blocks_per_seg, n_lat_blocks):
    i = pl.program_id(0)
    is_lat = i < n_lat_blocks
    first = jnp.logical_or(jnp.logical_not(is_lat), i % blocks_per_seg == 0)
    last = jnp.logical_or(jnp.logical_not(is_lat), i % blocks_per_seg == blocks_per_seg - 1)
    hr = HALO_ROWS
    e_sc[0:hr, :] = jnp.where(first, 0.0, xp_ref[...].astype(F32))
    e_sc[hr:hr + tm, :] = x_ref[...].astype(F32)
    e_sc[hr + tm:2 * hr + tm, :] = jnp.where(last, 0.0, xn_ref[...].astype(F32))
    pad = (S_CONV - 1) // 2
    acc = b_ref[...] + w_ref[0:1, :] * e_sc[hr - pad:hr - pad + tm, :]
    for kk in range(1, S_CONV):
        acc = acc + w_ref[kk:kk + 1, :] * e_sc[hr - pad + kk:hr - pad + kk + tm, :]
    o_ref[...] = _silu(acc)


HALO_ROWS = 16


def _conv(p, w, bias, dims, tm):
    b, s, lc, nl, t = dims
    ch = w.shape[1]
    r8 = tm // HALO_ROWS
    nblk = t // tm
    xoff = (OFF_S + 512) // ch
    assert (OFF_S + 512) % ch == 0
    assert lc == tm
    return pl.pallas_call(
        functools.partial(_conv_kernel, tm=tm, blocks_per_seg=s // tm, n_lat_blocks=nl // tm),
        grid=(nblk,),
        in_specs=[pl.BlockSpec((tm, ch), lambda i: (i, xoff)),
                  pl.BlockSpec((HALO_ROWS, ch), lambda i: (jnp.maximum(i * r8 - 1, 0), xoff)),
                  pl.BlockSpec((HALO_ROWS, ch),
                               lambda i: (jnp.minimum((i + 1) * r8, nblk * r8 - 1), xoff)),
                  pl.BlockSpec((S_CONV, ch), lambda i: (0, 0)),
                  pl.BlockSpec((1, ch), lambda i: (0, 0))],
        out_specs=pl.BlockSpec((tm, ch), lambda i: (i, 0)),
        out_shape=jax.ShapeDtypeStruct((t, ch), F32),
        scratch_shapes=[pltpu.VMEM((tm + 2 * HALO_ROWS, ch), F32)],
        compiler_params=_cparams(("parallel",)),
        name="s_conv",
    )(p, p, p, w, bias)


def _ssd_step(a_ref, dtb_ref, uf_ref, dtf_ref, ub_ref, dtbw_ref, yf_ref, yb_ref, h_sc):
    qn = S_CHUNK
    ii = lax.broadcasted_iota(jnp.int32, (qn, qn), 0)
    jj = lax.broadcasted_iota(jnp.int32, (qn, qn), 1)
    tri = jnp.where(jj <= ii, 1.0, 0.0).astype(BF)
    hg = S_HEADS // S_GROUPS
    dirs = []
    for d, (u_ref, dt_ref) in enumerate(((uf_ref, dtf_ref), (ub_ref, dtbw_ref))):
        xr = dt_ref[...] + dtb_ref[...]
        dt = jnp.maximum(xr, 0.0) + jnp.log1p(jnp.exp(-jnp.abs(xr)))
        dta = dt * a_ref[...]
        hi = dta.astype(BF)
        lo = (dta - hi.astype(F32)).astype(BF)
        lo2 = (dta - hi.astype(F32) - lo.astype(F32)).astype(BF)
        pfx = (jnp.dot(tri, hi, preferred_element_type=F32)
               + jnp.dot(tri, lo, preferred_element_type=F32)
               + jnp.dot(tri, lo2, preferred_element_type=F32))
        total = pfx[qn - 1:qn, :]
        cum = (total - pfx + dta) if d else pfx
        mask = ((jj - ii) if d else (ii - jj)) >= 0
        dirs.append((u_ref, dt, cum, cum.T, total, mask, h_sc[d]))
    lane_lo = lax.broadcasted_iota(jnp.int32, (1, 128), 1) < S_HEAD_DIM
    ys = ([], [])
    hs = ([], [])
    for g in range(S_GROUPS):
        grp = []
        for d, (u_ref, dt, cum, cum_t, total, mask, hprev) in enumerate(dirs):
            bg = u_ref[:, 512 + g * 128:512 + (g + 1) * 128]
            cg = u_ref[:, 768 + g * 128:768 + (g + 1) * 128]
            cb = _dot_nt(cg, bg)
            yoff = _dot(cg, hprev[:, g * hg * 64:(g + 1) * hg * 64])
            grp.append((cb, bg.T, yoff))
        for pr in range(hg // 2):
            h0 = g * hg + 2 * pr
            for d, (u_ref, dt, cum, cum_t, total, mask, hprev) in enumerate(dirs):
                cb, bgt, yoff = grp[d]
                ws, cbs, dbs, tots = [], [], [], []
                for h in (h0, h0 + 1):
                    ln = d * S_HEADS + h
                    ccol_b = jnp.broadcast_to(cum[:, ln:ln + 1], (qn, 128))
                    ws.append(cb * jnp.exp(jnp.where(mask, ccol_b - cum_t[ln:ln + 1, :], NEG_INF)))
                    cbs.append(ccol_b)
                    dbs.append(jnp.broadcast_to(dt[:, ln:ln + 1], (qn, 128)))
                    tots.append(total[:, ln:ln + 1])
                csel = jnp.where(lane_lo, cbs[0], cbs[1])
                tsel = jnp.where(lane_lo, tots[0], tots[1])
                xdt = u_ref[:, h0 * 64:h0 * 64 + 128] * jnp.where(lane_lo, dbs[0], dbs[1])
                rhs = jnp.concatenate([jnp.where(lane_lo, xdt, 0.0), jnp.where(lane_lo, 0.0, xdt)],
                                      axis=0)
                y = (_dot(jnp.concatenate(ws, axis=1), rhs)
                     + yoff[:, pr * 128:(pr + 1) * 128] * jnp.exp(csel))
                hnew = (jnp.exp(tsel) * hprev[:, h0 * 64:h0 * 64 + 128]
                        + _dot(bgt, xdt * jnp.exp(tsel - csel)))
                ys[d].append(y)
                hs[d].append(hnew)
    for d, y_ref in enumerate((yf_ref, yb_ref)):
        h_sc[d] = jnp.concatenate(hs[d], axis=1)
        y_ref[...] = jnp.concatenate(ys[d], axis=1).astype(y_ref.dtype)


def _scan_finish_kernel(rf_ref, rb_ref, g_ref, rnw_ref, sf_ref, sb_ref, x_ref, z_ref,
                        dsk_ref, snw_ref, yb_ref, yc_ref):
    tot = rf_ref[...].astype(F32) + rb_ref[...].astype(F32)
    ys = [_rms(tot[:, h * 128:(h + 1) * 128], R_V_DIM) for h in range(R_HEADS)]
    y = jnp.concatenate(ys, axis=1) * rnw_ref[...]
    yb_ref[...] = (y * _silu(g_ref[...].astype(F32))).astype(yb_ref.dtype)
    yt = (sf_ref[...].astype(F32) + sb_ref[...].astype(F32)
          + dsk_ref[...] * x_ref[...].astype(F32)) * _silu(z_ref[...].astype(F32))
    yc_ref[...] = (_rms(yt, yt.shape[-1]) * snw_ref[...]).astype(yc_ref.dtype)


def _scan_finish(rf, rb, p, rnw, sf, sb, u, dsk, snw, rows, tm):
    blk = lambda col: pl.BlockSpec((tm, 512), lambda i: (i, col))
    const = pl.BlockSpec((1, 512), lambda i: (0, 0))
    return pl.pallas_call(
        _scan_finish_kernel,
        grid=(rows // tm,),
        in_specs=[blk(0), blk(0), blk((OFF_R + 1024) // 512), const,
                  blk(0), blk(0), blk(0), blk(OFF_S // 512), const, const],
        out_specs=[blk(0), blk(0)],
        out_shape=[jax.ShapeDtypeStruct((rows, 512), BF)] * 2,
        compiler_params=_cparams(("parallel",)),
        name="scan_finish",
    )(rf, rb, p, rnw, sf, sb, u, p, dsk, snw)


def _mprep_kernel(p_ref, cos_ref, slo_ref, shi_ref, cqw_ref, ckvw_ref, wq_ref, wkv_ref,
                  qw_ref, kw_ref, q_ref, k_ref, v_ref):
    cos, slo, shi = cos_ref[...], slo_ref[...], shi_ref[...]
    scale = M_QK_DIM ** -0.5 * math.log2(math.e)
    cq = _rms(p_ref[:, 0:512].astype(F32), M_Q_RANK) * cqw_ref[...]
    q = _dot(cq, wq_ref[...])
    ckv = _rms(p_ref[:, 512:640].astype(F32), M_KV_RANK) * ckvw_ref[...]
    kv = _dot(ckv, wkv_ref[...])
    kr = p_ref[:, 640:768].astype(F32)
    kr_ss = jnp.sum(kr * kr, axis=-1, keepdims=True)
    for h in range(M_HEADS):
        c0 = q[:, h * 256:h * 256 + 128]
        c1 = q[:, h * 256 + 128:h * 256 + 256]
        r = lax.rsqrt((jnp.sum(c0 * c0, axis=-1, keepdims=True)
                       + jnp.sum(c1 * c1, axis=-1, keepdims=True)) * (1.0 / M_QK_DIM) + EPS)
        q_ref[:, h * 256:h * 256 + 128] = (c0 * r * qw_ref[:, 0:128] * scale).astype(BF)
        q_ref[:, h * 256 + 128:h * 256 + 256] = (
            _rope(c1 * r * qw_ref[:, 128:256], cos, slo, shi, 16) * scale).astype(BF)
        k0 = kv[:, h * 128:(h + 1) * 128]
        r = lax.rsqrt((jnp.sum(k0 * k0, axis=-1, keepdims=True) + kr_ss) * (1.0 / M_QK_DIM) + EPS)
        k_ref[:, h * 256:h * 256 + 128] = (k0 * r * kw_ref[:, 0:128]).astype(BF)
        k_ref[:, h * 256 + 128:h * 256 + 256] = (
            _rope(kr * r * kw_ref[:, 128:256], cos, slo, shi, 16)).astype(BF)
    tk = v_ref.shape[2]
    ones = jnp.ones((MV_ROWS - M_V, tk), BF)
    for r in range(v_ref.shape[0]):
        for h in range(M_HEADS):
            vt = kv[r * tk:(r + 1) * tk, 512 + h * M_V:512 + (h + 1) * M_V].T
            v_ref[r, h * MV_ROWS:h * MV_ROWS + M_V, :] = vt.astype(BF)
            v_ref[r, h * MV_ROWS + M_V:(h + 1) * MV_ROWS, :] = ones


def _mprep(p, tabs, cqw, ckvw, wq, wkv, qw, kw, dims, tm, tk):
    b, s, lc, nl, t = dims
    assert tm % tk == 0
    tab_spec = pl.BlockSpec((tm, 128), lambda i: (_table_index(i, tm, nl, s), 0))
    const = lambda shape: pl.BlockSpec(shape, lambda i: (0, 0))
    return pl.pallas_call(
        _mprep_kernel,
        grid=(t // tm,),
        in_specs=[pl.BlockSpec((tm, W_M), lambda i: (i, OFF_M // W_M)),
                  tab_spec, tab_spec, tab_spec,
                  const((1, 512)), const((1, 128)), const((512, 1024)), const((128, 1024)),
                  const((1, 256)), const((1, 256))],
        out_specs=[pl.BlockSpec((tm, 1024), lambda i: (i, 0)),
                   pl.BlockSpec((tm, 1024), lambda i: (i, 0)),
                   pl.BlockSpec((tm // tk, M_HEADS * MV_ROWS, tk), lambda i: (i, 0, 0))],
        out_shape=[jax.ShapeDtypeStruct((t, 1024), BF),
                   jax.ShapeDtypeStruct((t, 1024), BF),
                   jax.ShapeDtypeStruct((t // tk, M_HEADS * MV_ROWS, tk), BF)],
        compiler_params=_cparams(("parallel",)),
        name="m_prep",
    )(p, *tabs, cqw, ckvw, wq, wkv, qw, kw)


def _mla_kernel(q_ref, kx_ref, vx_ref, kl_ref, vl_ref, o_ref, acc_sc, s_sc, *, nq, tk):
    qi = pl.program_id(1)
    n_tiles = kl_ref.shape[0] // tk

    def scores(h, k_tile):
        return _dot_nt(k_tile[:, h * 256:(h + 1) * 256], q_ref[:, h * 256:(h + 1) * 256])

    def pv(h, vt_tile, p):
        return jnp.dot(vt_tile[h * MV_ROWS:(h + 1) * MV_ROWS, :], p.astype(BF),
                       preferred_element_type=F32)

    def k_tile(kt):
        return kl_ref.at[pl.ds(pl.multiple_of(kt * tk, tk), tk), :]

    s_ctx = [scores(h, kx_ref) for h in range(M_HEADS)]
    for h in range(M_HEADS):
        s_sc[h] = scores(h, k_tile(0))
    ms = []
    for h in range(M_HEADS):
        s = s_ctx[h]
        m0 = jnp.max(s, axis=0, keepdims=True)
        ms.append(m0)
        acc_sc[h] = pv(h, vx_ref, jnp.exp2(s - m0))

    def body(kt, ms):
        nxt = k_tile(jnp.minimum(kt + 1, n_tiles - 1))
        vt_tile = vl_ref.at[kt]
        ms_new = []
        s_cur = [s_sc[h] for h in range(M_HEADS)]
        s_nxt = [scores(h, nxt) for h in range(M_HEADS)]
        for h in range(M_HEADS):
            s = s_cur[h]
            m_new = jnp.maximum(ms[h], jnp.max(s, axis=0, keepdims=True))
            alpha = jnp.exp2(ms[h] - m_new)
            ms_new.append(m_new)
            acc_sc[h] = alpha * acc_sc[h] + pv(h, vt_tile, jnp.exp2(s - m_new))
        for h in range(M_HEADS):
            s_sc[h] = s_nxt[h]
        return tuple(ms_new)

    def finish():
        for h in range(M_HEADS):
            o = acc_sc[h, 0:M_V, :] * (1.0 / acc_sc[h, M_V:M_V + 1, :])
            o_ref[:, h * 128:(h + 1) * 128] = o.T.astype(o_ref.dtype)

    @pl.when(qi < nq)
    def _():
        lax.fori_loop(0, n_tiles, body, tuple(ms), unroll=4)
        finish()

    @pl.when(qi >= nq)
    def _():
        finish()


def _mla(qm, km, vmt, dims, need_ctx, tq, tk):
    b, s, lc, nl, t = dims
    nq = s // tq
    assert lc == tq and lc == tk and vmt.shape[2] == tk
    steps = nq + (1 if need_ctx else 0)
    rows = nl + (b * lc if need_ctx else 0)

    def qrow(bi, qi):
        return jnp.where(qi < nq, bi * nq + qi, nl // tq + bi)

    return pl.pallas_call(
        functools.partial(_mla_kernel, nq=nq, tk=tk),
        grid=(b, steps),
        in_specs=[pl.BlockSpec((tq, 1024), lambda bi, qi: (qrow(bi, qi), 0)),
                  pl.BlockSpec((lc, 1024), lambda bi, qi: (nl // lc + bi, 0)),
                  pl.BlockSpec((None, M_HEADS * MV_ROWS, tk), lambda bi, qi: (nl // tk + bi, 0, 0)),
                  pl.BlockSpec((s, 1024), lambda bi, qi: (bi, 0)),
                  pl.BlockSpec((s // tk, M_HEADS * MV_ROWS, tk), lambda bi, qi: (bi, 0, 0))],
        out_specs=pl.BlockSpec((tq, 512), lambda bi, qi: (qrow(bi, qi), 0)),
        out_shape=jax.ShapeDtypeStruct((rows, 512), BF),
        scratch_shapes=[pltpu.VMEM((M_HEADS, MV_ROWS, tq), F32),
                        pltpu.VMEM((M_HEADS, tk, tq), F32)],
        compiler_params=_cparams(("parallel", "arbitrary")),
        name="mla",
    )(qm, km, vmt, km, vmt)


def _merge_kernel(x_ref, mod_ref, ya_ref, yb_ref, yc_ref, yd_ref, g0_ref, g1_ref, g2_ref, g3_ref,
                  wb_ref, wo_ref, o_ref, acc_sc, *, tn):
    j = pl.program_id(1)
    acc = None
    for kk, (y_ref, g_ref) in enumerate(((ya_ref, g0_ref), (yb_ref, g1_ref),
                                         (yc_ref, g2_ref), (yd_ref, g3_ref))):
        gate = 1.0 / (1.0 + jnp.exp(-g_ref[...].astype(F32)))
        term = gate * jnp.dot(y_ref[...], wb_ref[kk], preferred_element_type=F32)
        acc = term if acc is None else acc + term
    acc_sc[:, pl.ds(pl.multiple_of(j * tn, tn), tn)] = acc.astype(BF)

    @pl.when(j == pl.num_programs(1) - 1)
    def _():
        o_ref[...] = x_ref[...] + mod_ref[2:3, :] * jnp.dot(
            acc_sc[...], wo_ref[...], preferred_element_type=F32)


def _merge(xs, mod, ys, p, wb, wo, layer, dims, rows, tm, tn):
    b, s, lc, nl, t = dims
    d = xs.shape[1]
    nj = d // tn
    goff = OFF_G // tn
    yspec = pl.BlockSpec((tm, BRANCH_W), lambda i, j: (i, 0))
    gspecs = [pl.BlockSpec((tm, tn), functools.partial(lambda i, j, kk: (i, goff + kk * nj + j), kk=kk))
              for kk in range(N_BRANCH)]
    return pl.pallas_call(
        functools.partial(_merge_kernel, tn=tn),
        grid=(rows // tm, nj),
        in_specs=[pl.BlockSpec((tm, d), lambda i, j: (i, 0)),
                  pl.BlockSpec((None, N_MOD, d), lambda i, j: (_mod_index(i, tm, nl, s, b), 0, 0)),
                  yspec, yspec, yspec, yspec, *gspecs,
                  pl.BlockSpec((None, N_BRANCH, BRANCH_W, tn), lambda i, j: (layer, 0, 0, j)),
                  pl.BlockSpec((None, d, d), lambda i, j: (layer, 0, 0))],
        out_specs=pl.BlockSpec((tm, d), lambda i, j: (i, 0)),
        out_shape=jax.ShapeDtypeStruct((rows, d), F32),
        scratch_shapes=[pltpu.VMEM((tm, d), BF)],
        compiler_params=_cparams(("parallel", "arbitrary")),
        name="merge",
    )(xs, mod, *ys, p, p, p, p, wb, wo)


def _ffn_kernel(x_ref, mod_ref, xn_ref, modn_ref, nw_ref, w1_ref, w2_ref, o_ref, h_sc):
    i = pl.program_id(0)
    f = pl.program_id(1)
    last = pl.num_programs(1) - 1
    slot = i % 2

    def normed(xr, mr):
        x = xr[...]
        y = _rms(x, x.shape[-1]) * nw_ref[...]
        return (y * (1.0 + mr[4:5, :]) + mr[3:4, :]).astype(BF)

    def mlp_part():
        u = jnp.maximum(jnp.dot(h_sc[slot], w1_ref[...], preferred_element_type=F32), 0.0)
        return jnp.dot((u * u).astype(BF), w2_ref[...], preferred_element_type=F32)

    @pl.when(jnp.logical_and(i == 0, f == 0))
    def _():
        h_sc[0] = normed(x_ref, mod_ref)

    @pl.when(f == 0)
    def _():
        o_ref[...] = mlp_part()

    @pl.when(jnp.logical_and(f > 0, f < last))
    def _():
        o_ref[...] += mlp_part()

    @pl.when(f == last)
    def _():
        part = mlp_part()
        h_sc[1 - slot] = normed(xn_ref, modn_ref)
        o_ref[...] = x_ref[...] + mod_ref[5:6, :] * (o_ref[...] + part)


def _ffn(xs, mod, nw, w1, w2, layer, dims, rows, tm, tf):
    b, s, lc, nl, t = dims
    d = xs.shape[1]
    dff = w1.shape[2]
    nblk = rows // tm
    assert dff // tf >= 2
    nxt = lambda i: jnp.minimum(i + 1, nblk - 1)
    return pl.pallas_call(
        _ffn_kernel,
        grid=(nblk, dff // tf),
        in_specs=[pl.BlockSpec((tm, d), lambda i, f: (i, 0)),
                  pl.BlockSpec((None, N_MOD, d), lambda i, f: (_mod_index(i, tm, nl, s, b), 0, 0)),
                  pl.BlockSpec((tm, d), lambda i, f: (nxt(i), 0)),
                  pl.BlockSpec((None, N_MOD, d),
                               lambda i, f: (_mod_index(nxt(i), tm, nl, s, b), 0, 0)),
                  pl.BlockSpec((1, d), lambda i, f: (0, 0)),
                  pl.BlockSpec((None, d, tf), lambda i, f: (layer, 0, f)),
                  pl.BlockSpec((None, tf, d), lambda i, f: (layer, f, 0))],
        out_specs=pl.BlockSpec((tm, d), lambda i, f: (i, 0)),
        out_shape=jax.ShapeDtypeStruct((rows, d), F32),
        scratch_shapes=[pltpu.VMEM((2, tm, d), BF)],
        compiler_params=_cparams(("arbitrary", "arbitrary")),
        name="ffn",
    )(xs, mod, xs, mod, nw, w1, w2)


def _rope_tables(pos_per_lane, freq_per_lane, lo_lane, pad_rows, live_lane=None):
    ang = pos_per_lane * freq_per_lane[None, :]
    cos, sin = jnp.cos(ang), jnp.sin(ang)
    if live_lane is not None:
        cos = jnp.where(live_lane[None, :], cos, 1.0)
        sin = jnp.where(live_lane[None, :], sin, 0.0)
    s_lo = jnp.where(lo_lane[None, :], -sin, 0.0)
    s_hi = jnp.where(lo_lane[None, :], 0.0, sin)
    w = cos.shape[1]
    ident = (jnp.ones((pad_rows, w), F32), jnp.zeros((pad_rows, w), F32), jnp.zeros((pad_rows, w), F32))
    return tuple(jnp.concatenate([tb.astype(F32), idn], axis=0)
                 for tb, idn in zip((cos, s_lo, s_hi), ident))


def _tables(s, pad_rows):
    t = jnp.arange(s)
    row = (t // GRID_W).astype(F32)[:, None]
    col = (t % GRID_W).astype(F32)[:, None]
    tpos = t.astype(F32)[:, None]
    lane = jnp.arange(128)
    fa = ROPE_BASE ** (-(lane % 32).astype(F32) / 32)
    pos_a = jnp.where((lane < 64)[None, :], row, col)
    tab_a = _rope_tables(pos_a, fa, (lane % 64) < 32, pad_rows)
    lane = jnp.arange(256)
    fb = ROPE_BASE ** (-(lane % 32).astype(F32) / 32)
    tab_r = _rope_tables(jnp.broadcast_to(tpos, (s, 256)), fb, (lane % 64) < 32, pad_rows)
    lane = jnp.arange(128)
    fm = ROPE_BASE ** (-(lane % 16).astype(F32) / 16)
    pos_m = jnp.where((lane < 32)[None, :], row, col)
    tab_m = _rope_tables(pos_m, fm, (lane % 32) < 16, pad_rows, live_lane=lane < 64)
    return tab_a, tab_r, tab_m


def _pad_cols(w, n):
    return jnp.pad(w, [(0, 0)] * (w.ndim - 1) + [(0, n - w.shape[-1])])


def kernel(x, c, ctx, c_ctx, w_ada, b_ada, norm1_w, norm2_w, w_in, a_q_norm, a_k_norm, a_sink,
           r_decay, r_norm, s_conv_w, s_conv_b, s_a_log, s_dt_bias, s_d, s_norm,
           m_cq_norm, m_ckv_norm, m_w_uq, m_w_ukv, m_q_norm, m_k_norm,
           w_branch, w_o, w_ff1, w_ff2):
    b, s, d = x.shape
    lc = ctx.shape[1]
    depth = w_ada.shape[0]
    nl, ncx = b * s, b * lc
    t = nl + ncx
    dims = (b, s, lc, nl, t)
    tm_big = max(m for m in (1024, 512, 256) if s % m == 0 and ncx % m == 0)
    tm = max(m for m in (512, 256) if s % m == 0 and ncx % m == 0)
    tc = lc
    tn_in, tn_merge, tf = 1024, 512, 1024
    assert lc == 256 and s % 256 == 0 and b + 1 <= 8

    w_in16 = w_in.astype(BF)
    w_mix = jnp.concatenate([_pad_cols(w_in16[..., 4112:4816], W_M),
                             _pad_cols(w_in16[..., 4096:4112], W_DT), w_in16[..., 0:4096]], axis=-1)
    w_gate = w_in16[..., 4816:]
    wq = _pad_cols(m_w_uq.reshape(depth, M_Q_RANK, M_HEADS, M_QK_DIM), 256)
    wq = wq.reshape(depth, M_Q_RANK, M_HEADS * 256).astype(BF)
    wkv = m_w_ukv.reshape(depth, M_KV_RANK, M_HEADS, M_NOPE + M_V)
    wkv = jnp.concatenate([wkv[..., :M_NOPE].reshape(depth, M_KV_RANK, M_HEADS * M_NOPE),
                           wkv[..., M_NOPE:].reshape(depth, M_KV_RANK, M_HEADS * M_V)],
                          axis=-1).astype(BF)
    wb = w_branch.astype(BF)
    wo = w_o.astype(BF)
    w1 = w_ff1.astype(BF)
    w2 = w_ff2.astype(BF)
    log_gamma = jnp.log1p(-jnp.exp(r_decay.astype(F32)))
    a_neg = _pad_cols(-jnp.exp(s_a_log.astype(F32)).reshape(depth, 1, 2 * S_HEADS), 128)
    dtb = _pad_cols(s_dt_bias.astype(F32).reshape(depth, 1, 2 * S_HEADS), 128)
    dsk = jnp.repeat(s_d.astype(F32), S_HEAD_DIM, axis=-1)[:, None, :]
    mqw = _pad_cols(m_q_norm, 256)[:, None, :]
    mkw = _pad_cols(m_k_norm, 256)[:, None, :]
    tab_a, tab_r, tab_m = _tables(s, tm)

    c8 = jnp.zeros((8, d), F32).at[:b].set(c).at[b].set(c_ctx)
    mod_all = _ada(c8, w_ada, b_ada).reshape(depth, 8, N_MOD, d)

    xs = jnp.concatenate([x.reshape(nl, d), ctx.reshape(ncx, d)], axis=0)
    for i in range(depth):
        need_ctx = i < depth - 1
        rows = t if need_ctx else nl
        mod = mod_all[i]
        p, pdt = _inproj(xs, mod, norm1_w[i][None, :], w_mix, w_gate, i, dims, tm_big, tn_in)
        qa, ka, va = _aprep(p, tab_a, a_q_norm[i][None, :], a_k_norm[i][None, :], dims, tm)
        ya = _aattn(a_sink[i].astype(F32), qa, ka, va, dims, need_ctx)
        u = _conv(p, s_conv_w[i], s_conv_b[i][None, :], dims, tc)
        rf, rb, sf, sb = _scans(log_gamma[i], p, tab_r, a_neg[i], dtb[i], u, pdt, dims)
        yb, yc = _scan_finish(rf, rb, p, r_norm[i].reshape(1, -1), sf, sb, u, dsk[i],
                              s_norm[i][None, :], rows, tm)
        qm, km, vm = _mprep(p, tab_m, m_cq_norm[i][None, :],
                            m_ckv_norm[i][None, :], wq[i], wkv[i], mqw[i], mkw[i], dims, tm, tc)
        yd = _mla(qm, km, vm, dims, need_ctx, tc, tc)
        xs = _merge(xs, mod, (ya, yb, yc, yd), p, wb, wo, i, dims, rows, tm, tn_merge)
        xs = _ffn(xs, mod, norm2_w[i][None, :], w1, w2, i, dims, rows, tm, tf)
    return xs.reshape(b, s, d)
```

```python
import functools
import math

import jax
import jax.numpy as jnp
from jax import lax
from jax.experimental import pallas as pl
from jax.experimental.pallas import tpu as pltpu

GRID_W = 64
EPS = 1e-6
ROPE_BASE = 10000.0
NEG_INF = -1e30

A_HEADS, A_KV_HEADS, A_HEAD_DIM, A_BLOCK = 4, 2, 128, 128
R_HEADS, R_QK_DIM, R_V_DIM, R_CHUNK = 4, 64, 128, 128
S_HEADS, S_HEAD_DIM, S_GROUPS, S_STATE, S_CONV, S_CHUNK = 8, 64, 2, 128, 5, 128
M_HEADS, M_Q_RANK, M_KV_RANK, M_NOPE, M_ROPE, M_V = 4, 512, 128, 128, 64, 128
M_QK_DIM = M_NOPE + M_ROPE
MV_ROWS = M_V + 16
N_BRANCH, N_MOD = 4, 6
BRANCH_W = 512

OFF_M, W_M = 0, 768
OFF_DT, W_DT = 768, 256
OFF_A, W_A = 1024, 1024
OFF_R, W_R = 2048, 1536
OFF_S, W_S = 3584, 1536
OFF_G = 5120

VMEM_LIMIT = 56 * 1024 * 1024

BF = jnp.bfloat16
F32 = jnp.float32


def _cparams(sem):
    return pltpu.CompilerParams(dimension_semantics=sem, vmem_limit_bytes=VMEM_LIMIT)


def _dot(a, b):
    return jnp.dot(a.astype(BF), b.astype(BF), preferred_element_type=F32)


def _dot_nt(a, b):
    return lax.dot_general(a.astype(BF), b.astype(BF), (((1,), (1,)), ((), ())),
                           preferred_element_type=F32)


def _sigmoid(x):
    return 0.5 * jnp.tanh(0.5 * x) + 0.5


def _silu(x):
    return x * _sigmoid(x)


def _rms(x, n):
    return x * lax.rsqrt(jnp.sum(x * x, axis=-1, keepdims=True) * (1.0 / n) + EPS)


def _rope(y, cos, s_lo, s_hi, shift):
    w = y.shape[-1]
    return y * cos + pltpu.roll(y, w - shift, 1) * s_lo + pltpu.roll(y, shift, 1) * s_hi


def _ada_kernel(c_ref, w_ref, b_ref, o_ref):
    o_ref[...] = _dot(_silu(c_ref[...]), w_ref[...]) + b_ref[...]


def _ada(c8, w_ada, b_ada):
    depth, d, n = w_ada.shape
    tn = 512
    return pl.pallas_call(
        _ada_kernel,
        grid=(depth, n // tn),
        in_specs=[pl.BlockSpec((8, d), lambda l, j: (0, 0)),
                  pl.BlockSpec((None, d, tn), lambda l, j: (l, 0, j)),
                  pl.BlockSpec((None, 1, tn), lambda l, j: (l, 0, j))],
        out_specs=pl.BlockSpec((None, 8, tn), lambda l, j: (l, 0, j)),
        out_shape=jax.ShapeDtypeStruct((depth, 8, n), F32),
        compiler_params=_cparams(("parallel", "parallel")),
        name="ada",
    )(c8, w_ada, b_ada.reshape(depth, 1, n))


def _inproj_kernel(x_ref, mod_ref, nw_ref, wm_ref, wg_ref, o_ref, dt_ref, h_sc, *, nm):
    i = pl.program_id(0)
    j = pl.program_id(1)
    last = pl.num_programs(1) - 1
    slot = i % 2

    def normed():
        x = x_ref[...]
        y = _rms(x, x.shape[-1]) * nw_ref[...]
        return (y * (1.0 + mod_ref[1:2, :]) + mod_ref[0:1, :]).astype(BF)

    def tile(w_ref):
        return jnp.dot(h_sc[slot], w_ref[...], preferred_element_type=F32)

    @pl.when(jnp.logical_and(i == 0, j == 0))
    def _():
        h_sc[0] = normed()

    @pl.when(j == 0)
    def _():
        r = tile(wm_ref)
        o_ref[...] = r.astype(o_ref.dtype)
        dt_ref[...] = r[:, OFF_DT:OFF_DT + W_DT]

    @pl.when(jnp.logical_and(j > 0, j < nm))
    def _():
        o_ref[...] = tile(wm_ref).astype(o_ref.dtype)

    @pl.when(jnp.logical_and(j >= nm, j < last))
    def _():
        o_ref[...] = tile(wg_ref).astype(o_ref.dtype)

    @pl.when(j == last)
    def _():
        o_ref[...] = tile(wg_ref).astype(o_ref.dtype)
        h_sc[1 - slot] = normed()


def _mod_index(i, tm, nl, s, b):
    return jnp.where(i < nl // tm, i // (s // tm), b)


def _inproj(xs, mod, nw, wm, wg, layer, dims, tm, tn):
    b, s, lc, nl, t = dims
    d = xs.shape[1]
    nm, ng = wm.shape[2] // tn, wg.shape[2] // tn
    nblk = t // tm
    assert OFF_DT + W_DT <= tn and wm.shape[2] == OFF_G and nm >= 2 and ng >= 2

    def ahead(i, j):
        return jnp.where(jnp.logical_and(i == 0, j == 0), 0, jnp.minimum(i + 1, nblk - 1))

    return pl.pallas_call(
        functools.partial(_inproj_kernel, nm=nm),
        grid=(nblk, nm + ng),
        in_specs=[pl.BlockSpec((tm, d), lambda i, j: (ahead(i, j), 0)),
                  pl.BlockSpec((None, N_MOD, d),
                               lambda i, j: (_mod_index(ahead(i, j), tm, nl, s, b), 0, 0)),
                  pl.BlockSpec((1, d), lambda i, j: (0, 0)),
                  pl.BlockSpec((None, d, tn), lambda i, j: (layer, 0, jnp.minimum(j, nm - 1))),
                  pl.BlockSpec((None, d, tn), lambda i, j: (layer, 0, jnp.maximum(j - nm, 0)))],
        out_specs=[pl.BlockSpec((tm, tn), lambda i, j: (i, j)),
                   pl.BlockSpec((tm, W_DT), lambda i, j: (i, 0))],
        out_shape=[jax.ShapeDtypeStruct((t, (nm + ng) * tn), BF),
                   jax.ShapeDtypeStruct((t, W_DT), F32)],
        scratch_shapes=[pltpu.VMEM((2, tm, d), BF)],
        compiler_params=_cparams(("arbitrary", "arbitrary")),
        name="inproj",
    )(xs, mod, nw, wm, wg)


def _aprep_kernel(p_ref, cos_ref, slo_ref, shi_ref, qw_ref, kw_ref, q_ref, k_ref, v_ref):
    cos, slo, shi = cos_ref[...], slo_ref[...], shi_ref[...]
    scale = A_HEAD_DIM ** -0.5
    for h in range(A_HEADS):
        y = _rms(p_ref[:, h * 128:(h + 1) * 128].astype(F32), A_HEAD_DIM) * qw_ref[...]
        q_ref[:, h * 128:(h + 1) * 128] = (_rope(y, cos, slo, shi, 32) * scale).astype(BF)
    for h in range(A_KV_HEADS):
        y = _rms(p_ref[:, 512 + h * 128:512 + (h + 1) * 128].astype(F32), A_HEAD_DIM) * kw_ref[...]
        k_ref[:, h * 128:(h + 1) * 128] = _rope(y, cos, slo, shi, 32).astype(BF)
    for r in range(v_ref.shape[0]):
        v_ref[r] = p_ref[r * A_BLOCK:(r + 1) * A_BLOCK, 768:1024].astype(F32).T.astype(BF)


def _table_index(i, tm, nl, s):
    return jnp.where(i < nl // tm, i % (s // tm), s // tm)


def _aprep(p, tabs, qw, kw, dims, tm):
    b, s, lc, nl, t = dims
    tab_spec = pl.BlockSpec((tm, 128), lambda i: (_table_index(i, tm, nl, s), 0))
    return pl.pallas_call(
        _aprep_kernel,
        grid=(t // tm,),
        in_specs=[pl.BlockSpec((tm, W_A), lambda i: (i, OFF_A // W_A)),
                  tab_spec, tab_spec, tab_spec,
                  pl.BlockSpec((1, 128), lambda i: (0, 0)),
                  pl.BlockSpec((1, 128), lambda i: (0, 0))],
        out_specs=[pl.BlockSpec((tm, 512), lambda i: (i, 0)),
                   pl.BlockSpec((tm, 256), lambda i: (i, 0)),
                   pl.BlockSpec((tm // A_BLOCK, 256, A_BLOCK), lambda i: (i, 0, 0))],
        out_shape=[jax.ShapeDtypeStruct((t, 512), BF),
                   jax.ShapeDtypeStruct((t, 256), BF),
                   jax.ShapeDtypeStruct((t // A_BLOCK, 256, A_BLOCK), BF)],
        compiler_params=_cparams(("parallel",)),
        name="a_prep",
    )(p, *tabs, qw, kw)


def _aattn_kernel(sink_ref, q_ref, kp_ref, kc_ref, kn_ref, vp_ref, vc_ref, vn_ref,
                  kx_ref, vx_ref, o_ref, *, nb):
    n = pl.program_id(1)
    blk = A_BLOCK
    lc = kx_ref.shape[0]
    g = A_HEADS // A_KV_HEADS
    nk = 3 * blk + lc
    kj = lax.broadcasted_iota(jnp.int32, (nk, g * blk), 0)
    qi = lax.broadcasted_iota(jnp.int32, (nk, g * blk), 1) % blk
    lo_seq = jnp.where(n == 0, blk, 0)
    hi_seq = jnp.where(n < nb, jnp.where(n == nb - 1, 2 * blk - 1, 3 * blk - 1), -1)
    mask = ((kj >= jnp.maximum(qi, lo_seq)) & (kj <= jnp.minimum(qi + 2 * blk, hi_seq))) | (kj >= 3 * blk)
    lane = lax.broadcasted_iota(jnp.int32, (1, g * blk), 1)
    ss = []
    for hk in range(A_KV_HEADS):
        sl = slice(hk * 128, (hk + 1) * 128)
        q2 = jnp.concatenate([q_ref[:, (hk * g + gi) * 128:(hk * g + gi + 1) * 128]
                              for gi in range(g)], axis=0)
        kk = jnp.concatenate([kp_ref[:, sl], kc_ref[:, sl], kn_ref[:, sl], kx_ref[:, sl]], axis=0)
        ss.append(_dot_nt(kk, q2))
    for hk in range(A_KV_HEADS):
        sl = slice(hk * 128, (hk + 1) * 128)
        vvt = jnp.concatenate([vp_ref[sl, :], vc_ref[sl, :], vn_ref[sl, :]]
                              + [vx_ref[r, sl, :] for r in range(lc // blk)], axis=1)
        s = jnp.where(mask, ss[hk], NEG_INF)
        sink = jnp.where(lane < blk, sink_ref[hk * g], sink_ref[hk * g + 1])
        m = jnp.maximum(jnp.max(s, axis=0, keepdims=True), sink)
        p = jnp.exp(s - m)
        inv = 1.0 / (jnp.sum(p, axis=0, keepdims=True) + jnp.exp(sink - m))
        o = (jnp.dot(vvt, p.astype(BF), preferred_element_type=F32) * inv).T
        for gi in range(g):
            o_ref[:, (hk * g + gi) * 128:(hk * g + gi + 1) * 128] = (
                o[gi * blk:(gi + 1) * blk]).astype(o_ref.dtype)


def _aattn(sink, qa, ka, va, dims, need_ctx):
    b, s, lc, nl, t = dims
    blk = A_BLOCK
    nb = s // blk
    ncb = lc // blk
    steps = nb + (ncb if need_ctx else 0)
    rows = nl + (b * lc if need_ctx else 0)

    def qrow(bi, n):
        return jnp.where(n < nb, bi * nb + n, nl // blk + bi * ncb + (n - nb))

    def krow(off):
        def f(bi, n):
            return (bi * nb + jnp.clip(n + off, 0, nb - 1), 0)
        return f

    kspec = [pl.BlockSpec((blk, 256), krow(o)) for o in (-1, 0, 1)]
    vspec = [pl.BlockSpec((None, 256, blk), lambda bi, n, o=o: (*krow(o)(bi, n), 0))
             for o in (-1, 0, 1)]
    xspec = pl.BlockSpec((lc, 256), lambda bi, n: (nl // lc + bi, 0))
    vxspec = pl.BlockSpec((ncb, 256, blk), lambda bi, n: (nl // lc + bi, 0, 0))
    return pl.pallas_call(
        functools.partial(_aattn_kernel, nb=nb),
        grid=(b, steps),
        in_specs=[pl.BlockSpec(memory_space=pltpu.SMEM),
                  pl.BlockSpec((blk, 512), lambda bi, n: (qrow(bi, n), 0)),
                  *kspec, *vspec, xspec, vxspec],
        out_specs=pl.BlockSpec((blk, 512), lambda bi, n: (qrow(bi, n), 0)),
        out_shape=jax.ShapeDtypeStruct((rows, 512), BF),
        compiler_params=_cparams(("parallel", "arbitrary")),
        name="a_attn",
    )(sink, qa, ka, ka, ka, va, va, va, ka, va)


def _scan_slot(t, ncc, nlc, bwd):
    if not bwd:
        return t
    return jnp.where(t < ncc, ncc - 1 - t, ncc + (nlc - 1 - (t - ncc)))


def _scan_row(bi, t, ncc, nlc, nl, chunk, bwd):
    slot = _scan_slot(t, ncc, nlc, bwd)
    return jnp.where(slot < ncc, nl // chunk + bi * ncc + slot, bi * nlc + (slot - ncc))


def _ret_init(lg_ref, st_sc, dm_sc, eq_sc, ek_sc):
    c = R_CHUNK
    hd = R_QK_DIM
    nh = R_HEADS

    def per_head(idx, vals):
        out = vals[nh - 1]
        for h in range(nh - 2, -1, -1):
            out = jnp.where(idx == h, vals[h], out)
        return out

    st_sc[...] = jnp.zeros_like(st_sc)
    ii = lax.broadcasted_iota(jnp.int32, (c, c), 0)
    jj = lax.broadcasted_iota(jnp.int32, (c, c), 1)
    col = lax.broadcasted_iota(jnp.int32, (c, 1), 0)
    row = lax.broadcasted_iota(jnp.int32, (1, c), 1)
    lane_head = lax.broadcasted_iota(jnp.int32, (1, nh * hd), 1) // hd
    sub_head = lax.broadcasted_iota(jnp.int32, (nh * hd, 1), 0) // hd
    for d in range(2):
        dd = ((jj - ii) if d else (ii - jj)).astype(F32)
        pcol = ((c - 1 - col) if d else col).astype(F32)
        prow = ((c - 1 - row) if d else row).astype(F32)
        lgs = [lg_ref[d, h] for h in range(nh)]
        eq_sc[d] = jnp.exp(per_head(lane_head, lgs) * (pcol + 1.0))
        ek_sc[d] = jnp.exp(per_head(sub_head, lgs) * (c - 1.0 - prow))
        for h in range(nh):
            dm_sc[d, h] = jnp.where(dd >= 0, jnp.exp(lgs[h] * jnp.maximum(dd, 0.0)), 0.0)


def _ret_step(lg_ref, qf_ref, kf_ref, vf_ref, qb_ref, kb_ref, vb_ref,
              cosf_ref, slof_ref, shif_ref, cosb_ref, slob_ref, shib_ref,
              of_ref, ob_ref, st_sc, dm_sc, eq_sc, ek_sc):
    c = R_CHUNK
    hd = R_QK_DIM
    nh = R_HEADS
    lane_head = lax.broadcasted_iota(jnp.int32, (1, 2 * hd), 1) // hd
    dirs = []
    for d, (q_ref, k_ref, v_ref, cos_ref, slo_ref, shi_ref) in enumerate((
            (qf_ref, kf_ref, vf_ref, cosf_ref, slof_ref, shif_ref),
            (qb_ref, kb_ref, vb_ref, cosb_ref, slob_ref, shib_ref))):
        cos, slo, shi = cos_ref[...], slo_ref[...], shi_ref[...]
        q = _rope(q_ref[...].astype(F32), cos, slo, shi, 32)
        k = _rope(k_ref[...].astype(F32), cos, slo, shi, 32) * (hd ** -0.5)
        kt = k.T
        dirs.append((q, q * eq_sc[d], kt, kt * ek_sc[d], v_ref[...], st_sc[d]))
    outs = ([], [])
    new_states = ([], [])
    for h in range(nh):
        pr = h // 2
        for d, (q, qd, kt, kdt, v, state) in enumerate(dirs):
            sel = lane_head == (h % 2)
            qh = jnp.where(sel, q[:, pr * 128:(pr + 1) * 128], 0.0)
            qdh = jnp.where(sel, qd[:, pr * 128:(pr + 1) * 128], 0.0)
            vh = v[:, h * 128:(h + 1) * 128]
            s = _dot(qh, kt[pr * 128:(pr + 1) * 128, :]) * dm_sc[d, h]
            inter = _dot(qdh, state[pr * 128:(pr + 1) * 128, :])
            outs[d].append(_dot(s, vh) + inter)
            chunk_decay = jnp.exp(lg_ref[d, h] * jnp.full((1, R_V_DIM), float(c), F32))
            new_states[d].append(chunk_decay * state[h * hd:(h + 1) * hd, :]
                                 + _dot(kdt[h * hd:(h + 1) * hd, :], vh))
    for d, o_ref in enumerate((of_ref, ob_ref)):
        st_sc[d] = jnp.concatenate(new_states[d], axis=0)
        o_ref[...] = jnp.concatenate(outs[d], axis=1).astype(o_ref.dtype)


N_RET_IN, N_SSD_IN = 13, 6


def _scan_kernel(*refs):
    ret_in = refs[:N_RET_IN]
    ssd_in = refs[N_RET_IN:N_RET_IN + N_SSD_IN]
    of_ref, ob_ref, yf_ref, yb_ref, st_sc, dm_sc, eq_sc, ek_sc, h_sc = refs[N_RET_IN + N_SSD_IN:]

    @pl.when(pl.program_id(1) == 0)
    def _():
        _ret_init(ret_in[0], st_sc, dm_sc, eq_sc, ek_sc)
        h_sc[...] = jnp.zeros_like(h_sc)

    _ret_step(*ret_in, of_ref, ob_ref, st_sc, dm_sc, eq_sc, ek_sc)
    _ssd_step(*ssd_in, yf_ref, yb_ref, h_sc)


def _scans(lg, p, tabs, a_flat, dtb_flat, u, pdt, dims):
    b, s, lc, nl, t = dims
    c = R_CHUNK
    assert S_CHUNK == c
    ncc, nlc = lc // c, s // c

    def row(bwd):
        return lambda bi, st: _scan_row(bi, st, ncc, nlc, nl, c, bwd)

    def spec(width, col_off, bwd):
        r = row(bwd)
        return pl.BlockSpec((c, width), lambda bi, st: (r(bi, st), col_off // width))

    def tab_spec(bwd):
        def trow(bi, st):
            slot = _scan_slot(st, ncc, nlc, bwd)
            return (jnp.where(slot < ncc, nlc, slot - ncc), 0)
        return pl.BlockSpec((c, 256), trow)

    def qkv(bwd):
        return [spec(256, OFF_R, bwd), spec(256, OFF_R + 256, bwd), spec(512, OFF_R + 512, bwd)]

    const = pl.BlockSpec((1, 128), lambda bi, st: (0, 0))
    ret_specs = [pl.BlockSpec(memory_space=pltpu.SMEM), *qkv(False), *qkv(True),
                 *[tab_spec(False)] * 3, *[tab_spec(True)] * 3]
    ssd_specs = [const, const, spec(1024, 0, False), spec(128, 0, False),
                 spec(1024, 0, True), spec(128, 0, True)]
    assert len(ret_specs) == N_RET_IN and len(ssd_specs) == N_SSD_IN
    return pl.pallas_call(
        _scan_kernel,
        grid=(b, ncc + nlc),
        in_specs=ret_specs + ssd_specs,
        out_specs=[spec(512, 0, False), spec(512, 0, True)] * 2,
        out_shape=[jax.ShapeDtypeStruct((t, 512), BF)] * 4,
        scratch_shapes=[pltpu.VMEM((2, R_HEADS * R_QK_DIM, R_V_DIM), F32),
                        pltpu.VMEM((2, R_HEADS, c, c), F32),
                        pltpu.VMEM((2, c, R_HEADS * R_QK_DIM), F32),
                        pltpu.VMEM((2, R_HEADS * R_QK_DIM, c), F32),
                        pltpu.VMEM((2, S_STATE, S_HEADS * S_HEAD_DIM), F32)],
        compiler_params=_cparams(("parallel", "arbitrary")),
        name="scans",
    )(lg, p, p, p, p, p, p, *tabs, *tabs, a_flat, dtb_flat, u, pdt, u, pdt)


def _conv_kernel(x_ref, xp_ref, xn_ref, w_ref, b_ref, o_ref, e_sc, *, tm, blocks_per_seg, n_lat_blocks):
    i = pl.program_id(0)
    is_lat = i < n_lat_blocks
    first = jnp.logical_or(jnp.logical_not(is_lat), i % blocks_per_seg == 0)
    last = jnp.logical_or(jnp.logical_not(is_lat), i % blocks_per_seg == blocks_per_seg - 1)
    hr = HALO_ROWS
    e_sc[0:hr, :] = jnp.where(first, 0.0, xp_ref[...].astype(F32))
    e_sc[hr:hr + tm, :] = x_ref[...].astype(F32)
    e_sc[hr + tm:2 * hr + tm, :] = jnp.where(last, 0.0, xn_ref[...].astype(F32))
    pad = (S_CONV - 1) // 2
    acc = b_ref[...] + w_ref[0:1, :] * e_sc[hr - pad:hr - pad + tm, :]
    for kk in range(1, S_CONV):
        acc = acc + w_ref[kk:kk + 1, :] * e_sc[hr - pad + kk:hr - pad + kk + tm, :]
    o_ref[...] = _silu(acc)


HALO_ROWS = 16


def _conv(p, w, bias, dims, tm):
    b, s, lc, nl, t = dims
    ch = w.shape[1]
    r8 = tm // HALO_ROWS
    nblk = t // tm
    xoff = (OFF_S + 512) // ch
    assert (OFF_S + 512) % ch == 0
    assert lc == tm
    return pl.pallas_call(
        functools.partial(_conv_kernel, tm=tm, blocks_per_seg=s // tm, n_lat_blocks=nl // tm),
        grid=(nblk,),
        in_specs=[pl.BlockSpec((tm, ch), lambda i: (i, xoff)),
                  pl.BlockSpec((HALO_ROWS, ch), lambda i: (jnp.maximum(i * r8 - 1, 0), xoff)),
                  pl.BlockSpec((HALO_ROWS, ch),
                               lambda i: (jnp.minimum((i + 1) * r8, nblk * r8 - 1), xoff)),
                  pl.BlockSpec((S_CONV, ch), lambda i: (0, 0)),
                  pl.BlockSpec((1, ch), lambda i: (0, 0))],
        out_specs=pl.BlockSpec((tm, ch), lambda i: (i, 0)),
        out_shape=jax.ShapeDtypeStruct((t, ch), F32),
        scratch_shapes=[pltpu.VMEM((tm + 2 * HALO_ROWS, ch), F32)],
        compiler_params=_cparams(("parallel",)),
        name="s_conv",
    )(p, p, p, w, bias)


def _ssd_step(a_ref, dtb_ref, uf_ref, dtf_ref, ub_ref, dtbw_ref, yf_ref, yb_ref, h_sc):
    qn = S_CHUNK
    ii = lax.broadcasted_iota(jnp.int32, (qn, qn), 0)
    jj = lax.broadcasted_iota(jnp.int32, (qn, qn), 1)
    tri = jnp.where(jj <= ii, 1.0, 0.0).astype(BF)
    hg = S_HEADS // S_GROUPS
    dirs = []
    for d, (u_ref, dt_ref) in enumerate(((uf_ref, dtf_ref), (ub_ref, dtbw_ref))):
        xr = dt_ref[...] + dtb_ref[...]
        dt = jnp.maximum(xr, 0.0) + jnp.log1p(jnp.exp(-jnp.abs(xr)))
        dta = dt * a_ref[...]
        hi = dta.astype(BF)
        lo = (dta - hi.astype(F32)).astype(BF)
        lo2 = (dta - hi.astype(F32) - lo.astype(F32)).astype(BF)
        pfx = (jnp.dot(tri, hi, preferred_element_type=F32)
               + jnp.dot(tri, lo, preferred_element_type=F32)
               + jnp.dot(tri, lo2, preferred_element_type=F32))
        total = pfx[qn - 1:qn, :]
        cum = (total - pfx + dta) if d else pfx
        mask = ((jj - ii) if d else (ii - jj)) >= 0
        dirs.append((u_ref, dt, cum, cum.T, total, mask, h_sc[d]))
    lane_lo = lax.broadcasted_iota(jnp.int32, (1, 128), 1) < S_HEAD_DIM
    ys = ([], [])
    hs = ([], [])
    for g in range(S_GROUPS):
        grp = []
        for d, (u_ref, dt, cum, cum_t, total, mask, hprev) in enumerate(dirs):
            bg = u_ref[:, 512 + g * 128:512 + (g + 1) * 128]
            cg = u_ref[:, 768 + g * 128:768 + (g + 1) * 128]
            cb = _dot_nt(cg, bg)
            yoff = _dot(cg, hprev[:, g * hg * 64:(g + 1) * hg * 64])
            grp.append((cb, bg.T, yoff))
        for pr in range(hg // 2):
            h0 = g * hg + 2 * pr
            for d, (u_ref, dt, cum, cum_t, total, mask, hprev) in enumerate(dirs):
                cb, bgt, yoff = grp[d]
                ws, cbs, dbs, tots = [], [], [], []
                for h in (h0, h0 + 1):
                    ln = d * S_HEADS + h
                    ccol_b = jnp.broadcast_to(cum[:, ln:ln + 1], (qn, 128))
                    ws.append(cb * jnp.exp(jnp.where(mask, ccol_b - cum_t[ln:ln + 1, :], NEG_INF)))
                    cbs.append(ccol_b)
                    dbs.append(jnp.broadcast_to(dt[:, ln:ln + 1], (qn, 128)))
                    tots.append(total[:, ln:ln + 1])
                csel = jnp.where(lane_lo, cbs[0], cbs[1])
                tsel = jnp.where(lane_lo, tots[0], tots[1])
                xdt = u_ref[:, h0 * 64:h0 * 64 + 128] * jnp.where(lane_lo, dbs[0], dbs[1])
                rhs = jnp.concatenate([jnp.where(lane_lo, xdt, 0.0), jnp.where(lane_lo, 0.0, xdt)],
                                      axis=0)
                y = (_dot(jnp.concatenate(ws, axis=1), rhs)
                     + yoff[:, pr * 128:(pr + 1) * 128] * jnp.exp(csel))
                hnew = (jnp.exp(tsel) * hprev[:, h0 * 64:h0 * 64 + 128]
                        + _dot(bgt, xdt * jnp.exp(tsel - csel)))
                ys[d].append(y)
                hs[d].append(hnew)
    for d, y_ref in enumerate((yf_ref, yb_ref)):
        h_sc[d] = jnp.concatenate(hs[d], axis=1)
        y_ref[...] = jnp.concatenate(ys[d], axis=1).astype(y_ref.dtype)


def _scan_finish_kernel(rf_ref, rb_ref, g_ref, rnw_ref, sf_ref, sb_ref, x_ref, z_ref,
                        dsk_ref, snw_ref, yb_ref, yc_ref):
    tot = rf_ref[...].astype(F32) + rb_ref[...].astype(F32)
    ys = [_rms(tot[:, h * 128:(h + 1) * 128], R_V_DIM) for h in range(R_HEADS)]
    y = jnp.concatenate(ys, axis=1) * rnw_ref[...]
    yb_ref[...] = (y * _silu(g_ref[...].astype(F32))).astype(yb_ref.dtype)
    yt = (sf_ref[...].astype(F32) + sb_ref[...].astype(F32)
          + dsk_ref[...] * x_ref[...].astype(F32)) * _silu(z_ref[...].astype(F32))
    yc_ref[...] = (_rms(yt, yt.shape[-1]) * snw_ref[...]).astype(yc_ref.dtype)


def _scan_finish(rf, rb, p, rnw, sf, sb, u, dsk, snw, rows, tm):
    blk = lambda col: pl.BlockSpec((tm, 512), lambda i: (i, col))
    const = pl.BlockSpec((1, 512), lambda i: (0, 0))
    return pl.pallas_call(
        _scan_finish_kernel,
        grid=(rows // tm,),
        in_specs=[blk(0), blk(0), blk((OFF_R + 1024) // 512), const,
                  blk(0), blk(0), blk(0), blk(OFF_S // 512), const, const],
        out_specs=[blk(0), blk(0)],
        out_shape=[jax.ShapeDtypeStruct((rows, 512), BF)] * 2,
        compiler_params=_cparams(("parallel",)),
        name="scan_finish",
    )(rf, rb, p, rnw, sf, sb, u, p, dsk, snw)


def _mprep_kernel(p_ref, cos_ref, slo_ref, shi_ref, cqw_ref, ckvw_ref, wq_ref, wkv_ref,
                  qw_ref, kw_ref, q_ref, k_ref, v_ref):
    cos, slo, shi = cos_ref[...], slo_ref[...], shi_ref[...]
    scale = M_QK_DIM ** -0.5 * math.log2(math.e)
    cq = _rms(p_ref[:, 0:512].astype(F32), M_Q_RANK) * cqw_ref[...]
    q = _dot(cq, wq_ref[...])
    ckv = _rms(p_ref[:, 512:640].astype(F32), M_KV_RANK) * ckvw_ref[...]
    kv = _dot(ckv, wkv_ref[...])
    kr = p_ref[:, 640:768].astype(F32)
    kr_ss = jnp.sum(kr * kr, axis=-1, keepdims=True)
    for h in range(M_HEADS):
        c0 = q[:, h * 256:h * 256 + 128]
        c1 = q[:, h * 256 + 128:h * 256 + 256]
        r = lax.rsqrt((jnp.sum(c0 * c0, axis=-1, keepdims=True)
                       + jnp.sum(c1 * c1, axis=-1, keepdims=True)) * (1.0 / M_QK_DIM) + EPS)
        q_ref[:, h * 256:h * 256 + 128] = (c0 * r * qw_ref[:, 0:128] * scale).astype(BF)
        q_ref[:, h * 256 + 128:h * 256 + 256] = (
            _rope(c1 * r * qw_ref[:, 128:256], cos, slo, shi, 16) * scale).astype(BF)
        k0 = kv[:, h * 128:(h + 1) * 128]
        r = lax.rsqrt((jnp.sum(k0 * k0, axis=-1, keepdims=True) + kr_ss) * (1.0 / M_QK_DIM) + EPS)
        k_ref[:, h * 256:h * 256 + 128] = (k0 * r * kw_ref[:, 0:128]).astype(BF)
        k_ref[:, h * 256 + 128:h * 256 + 256] = (
            _rope(kr * r * kw_ref[:, 128:256], cos, slo, shi, 16)).astype(BF)
    tk = v_ref.shape[2]
    ones = jnp.ones((MV_ROWS - M_V, tk), BF)
    for r in range(v_ref.shape[0]):
        for h in range(M_HEADS):
            vt = kv[r * tk:(r + 1) * tk, 512 + h * M_V:512 + (h + 1) * M_V].T
            v_ref[r, h * MV_ROWS:h * MV_ROWS + M_V, :] = vt.astype(BF)
            v_ref[r, h * MV_ROWS + M_V:(h + 1) * MV_ROWS, :] = ones


def _mprep(p, tabs, cqw, ckvw, wq, wkv, qw, kw, dims, tm, tk):
    b, s, lc, nl, t = dims
    assert tm % tk == 0
    tab_spec = pl.BlockSpec((tm, 128), lambda i: (_table_index(i, tm, nl, s), 0))
    const = lambda shape: pl.BlockSpec(shape, lambda i: (0, 0))
    return pl.pallas_call(
        _mprep_kernel,
        grid=(t // tm,),
        in_specs=[pl.BlockSpec((tm, W_M), lambda i: (i, OFF_M // W_M)),
                  tab_spec, tab_spec, tab_spec,
                  const((1, 512)), const((1, 128)), const((512, 1024)), const((128, 1024)),
                  const((1, 256)), const((1, 256))],
        out_specs=[pl.BlockSpec((tm, 1024), lambda i: (i, 0)),
                   pl.BlockSpec((tm, 1024), lambda i: (i, 0)),
                   pl.BlockSpec((tm // tk, M_HEADS * MV_ROWS, tk), lambda i: (i, 0, 0))],
        out_shape=[jax.ShapeDtypeStruct((t, 1024), BF),
                   jax.ShapeDtypeStruct((t, 1024), BF),
                   jax.ShapeDtypeStruct((t // tk, M_HEADS * MV_ROWS, tk), BF)],
        compiler_params=_cparams(("parallel",)),
        name="m_prep",
    )(p, *tabs, cqw, ckvw, wq, wkv, qw, kw)


def _mla_kernel(q_ref, kx_ref, vx_ref, kl_ref, vl_ref, o_ref, acc_sc, s_sc, *, nq, tk):
    qi = pl.program_id(1)
    n_tiles = kl_ref.shape[0] // tk

    def scores(h, k_tile):
        return _dot_nt(k_tile[:, h * 256:(h + 1) * 256], q_ref[:, h * 256:(h + 1) * 256])

    def pv(h, vt_tile, p):
        return jnp.dot(vt_tile[h * MV_ROWS:(h + 1) * MV_ROWS, :], p.astype(BF),
                       preferred_element_type=F32)

    def k_tile(kt):
        return kl_ref.at[pl.ds(pl.multiple_of(kt * tk, tk), tk), :]

    s_ctx = [scores(h, kx_ref) for h in range(M_HEADS)]
    for h in range(M_HEADS):
        s_sc[h] = scores(h, k_tile(0))
    ms = []
    for h in range(M_HEADS):
        s = s_ctx[h]
        m0 = jnp.max(s, axis=0, keepdims=True)
        ms.append(m0)
        acc_sc[h] = pv(h, vx_ref, jnp.exp2(s - m0))

    def body(kt, ms):
        nxt = k_tile(jnp.minimum(kt + 1, n_tiles - 1))
        vt_tile = vl_ref.at[kt]
        ms_new = []
        s_cur = [s_sc[h] for h in range(M_HEADS)]
        s_nxt = [scores(h, nxt) for h in range(M_HEADS)]
        for h in range(M_HEADS):
            s = s_cur[h]
            m_new = jnp.maximum(ms[h], jnp.max(s, axis=0, keepdims=True))
            alpha = jnp.exp2(ms[h] - m_new)
            ms_new.append(m_new)
            acc_sc[h] = alpha * acc_sc[h] + pv(h, vt_tile, jnp.exp2(s - m_new))
        for h in range(M_HEADS):
            s_sc[h] = s_nxt[h]
        return tuple(ms_new)

    def finish():
        for h in range(M_HEADS):
            o = acc_sc[h, 0:M_V, :] * (1.0 / acc_sc[h, M_V:M_V + 1, :])
            o_ref[:, h * 128:(h + 1) * 128] = o.T.astype(o_ref.dtype)

    @pl.when(qi < nq)
    def _():
        lax.fori_loop(0, n_tiles, body, tuple(ms), unroll=4)
        finish()

    @pl.when(qi >= nq)
    def _():
        finish()


def _mla(qm, km, vmt, dims, need_ctx, tq, tk):
    b, s, lc, nl, t = dims
    nq = s // tq
    assert lc == tq and lc == tk and vmt.shape[2] == tk
    steps = nq + (1 if need_ctx else 0)
    rows = nl + (b * lc if need_ctx else 0)

    def qrow(bi, qi):
        return jnp.where(qi < nq, bi * nq + qi, nl // tq + bi)

    return pl.pallas_call(
        functools.partial(_mla_kernel, nq=nq, tk=tk),
        grid=(b, steps),
        in_specs=[pl.BlockSpec((tq, 1024), lambda bi, qi: (qrow(bi, qi), 0)),
                  pl.BlockSpec((lc, 1024), lambda bi, qi: (nl // lc + bi, 0)),
                  pl.BlockSpec((None, M_HEADS * MV_ROWS, tk), lambda bi, qi: (nl // tk + bi, 0, 0)),
                  pl.BlockSpec((s, 1024), lambda bi, qi: (bi, 0)),
                  pl.BlockSpec((s // tk, M_HEADS * MV_ROWS, tk), lambda bi, qi: (bi, 0, 0))],
        out_specs=pl.BlockSpec((tq, 512), lambda bi, qi: (qrow(bi, qi), 0)),
        out_shape=jax.ShapeDtypeStruct((rows, 512), BF),
        scratch_shapes=[pltpu.VMEM((M_HEADS, MV_ROWS, tq), F32),
                        pltpu.VMEM((M_HEADS, tk, tq), F32)],
        compiler_params=_cparams(("parallel", "arbitrary")),
        name="mla",
    )(qm, km, vmt, km, vmt)


def _merge_kernel(x_ref, mod_ref, ya_ref, yb_ref, yc_ref, yd_ref, g0_ref, g1_ref, g2_ref, g3_ref,
                  wb_ref, wo_ref, o_ref, acc_sc, *, tn):
    j = pl.program_id(1)
    acc = None
    for kk, (y_ref, g_ref) in enumerate(((ya_ref, g0_ref), (yb_ref, g1_ref),
                                         (yc_ref, g2_ref), (yd_ref, g3_ref))):
        gate = _sigmoid(g_ref[...].astype(F32))
        term = gate * jnp.dot(y_ref[...], wb_ref[kk], preferred_element_type=F32)
        acc = term if acc is None else acc + term
    acc_sc[:, pl.ds(pl.multiple_of(j * tn, tn), tn)] = acc.astype(BF)

    @pl.when(j == pl.num_programs(1) - 1)
    def _():
        o_ref[...] = x_ref[...] + mod_ref[2:3, :] * jnp.dot(
            acc_sc[...], wo_ref[...], preferred_element_type=F32)


def _merge(xs, mod, ys, p, wb, wo, layer, dims, rows, tm, tn):
    b, s, lc, nl, t = dims
    d = xs.shape[1]
    nj = d // tn
    goff = OFF_G // tn
    yspec = pl.BlockSpec((tm, BRANCH_W), lambda i, j: (i, 0))
    gspecs = [pl.BlockSpec((tm, tn), functools.partial(lambda i, j, kk: (i, goff + kk * nj + j), kk=kk))
              for kk in range(N_BRANCH)]
    return pl.pallas_call(
        functools.partial(_merge_kernel, tn=tn),
        grid=(rows // tm, nj),
        in_specs=[pl.BlockSpec((tm, d), lambda i, j: (i, 0)),
                  pl.BlockSpec((None, N_MOD, d), lambda i, j: (_mod_index(i, tm, nl, s, b), 0, 0)),
                  yspec, yspec, yspec, yspec, *gspecs,
                  pl.BlockSpec((None, N_BRANCH, BRANCH_W, tn), lambda i, j: (layer, 0, 0, j)),
                  pl.BlockSpec((None, d, d), lambda i, j: (layer, 0, 0))],
        out_specs=pl.BlockSpec((tm, d), lambda i, j: (i, 0)),
        out_shape=jax.ShapeDtypeStruct((rows, d), F32),
        scratch_shapes=[pltpu.VMEM((tm, d), BF)],
        compiler_params=_cparams(("parallel", "arbitrary")),
        name="merge",
    )(xs, mod, *ys, p, p, p, p, wb, wo)


def _ffn_kernel(x_ref, mod_ref, xn_ref, modn_ref, nw_ref, w1_ref, w2_ref, o_ref, h_sc):
    i = pl.program_id(0)
    f = pl.program_id(1)
    last = pl.num_programs(1) - 1
    slot = i % 2

    def normed(xr, mr):
        x = xr[...]
        y = _rms(x, x.shape[-1]) * nw_ref[...]
        return (y * (1.0 + mr[4:5, :]) + mr[3:4, :]).astype(BF)

    def mlp_part():
        u = jnp.maximum(jnp.dot(h_sc[slot], w1_ref[...], preferred_element_type=F32), 0.0)
        return jnp.dot((u * u).astype(BF), w2_ref[...], preferred_element_type=F32)

    @pl.when(jnp.logical_and(i == 0, f == 0))
    def _():
        h_sc[0] = normed(x_ref, mod_ref)

    @pl.when(f == 0)
    def _():
        o_ref[...] = mlp_part()

    @pl.when(jnp.logical_and(f > 0, f < last))
    def _():
        o_ref[...] += mlp_part()

    @pl.when(f == last)
    def _():
        part = mlp_part()
        h_sc[1 - slot] = normed(xn_ref, modn_ref)
        o_ref[...] = x_ref[...] + mod_ref[5:6, :] * (o_ref[...] + part)


def _ffn(xs, mod, nw, w1, w2, layer, dims, rows, tm, tf):
    b, s, lc, nl, t = dims
    d = xs.shape[1]
    dff = w1.shape[2]
    nblk = rows // tm
    assert dff // tf >= 2
    nxt = lambda i: jnp.minimum(i + 1, nblk - 1)
    return pl.pallas_call(
        _ffn_kernel,
        grid=(nblk, dff // tf),
        in_specs=[pl.BlockSpec((tm, d), lambda i, f: (i, 0)),
                  pl.BlockSpec((None, N_MOD, d), lambda i, f: (_mod_index(i, tm, nl, s, b), 0, 0)),
                  pl.BlockSpec((tm, d), lambda i, f: (nxt(i), 0)),
                  pl.BlockSpec((None, N_MOD, d),
                               lambda i, f: (_mod_index(nxt(i), tm, nl, s, b), 0, 0)),
                  pl.BlockSpec((1, d), lambda i, f: (0, 0)),
                  pl.BlockSpec((None, d, tf), lambda i, f: (layer, 0, f)),
                  pl.BlockSpec((None, tf, d), lambda i, f: (layer, f, 0))],
        out_specs=pl.BlockSpec((tm, d), lambda i, f: (i, 0)),
        out_shape=jax.ShapeDtypeStruct((rows, d), F32),
        scratch_shapes=[pltpu.VMEM((2, tm, d), BF)],
        compiler_params=_cparams(("arbitrary", "arbitrary")),
        name="ffn",
    )(xs, mod, xs, mod, nw, w1, w2)


def _rope_tables(pos_per_lane, freq_per_lane, lo_lane, pad_rows, live_lane=None):
    ang = pos_per_lane * freq_per_lane[None, :]
    cos, sin = jnp.cos(ang), jnp.sin(ang)
    if live_lane is not None:
        cos = jnp.where(live_lane[None, :], cos, 1.0)
        sin = jnp.where(live_lane[None, :], sin, 0.0)
    s_lo = jnp.where(lo_lane[None, :], -sin, 0.0)
    s_hi = jnp.where(lo_lane[None, :], 0.0, sin)
    w = cos.shape[1]
    ident = (jnp.ones((pad_rows, w), F32), jnp.zeros((pad_rows, w), F32), jnp.zeros((pad_rows, w), F32))
    return tuple(jnp.concatenate([tb.astype(F32), idn], axis=0)
                 for tb, idn in zip((cos, s_lo, s_hi), ident))


def _tables(s, pad_rows):
    t = jnp.arange(s)
    row = (t // GRID_W).astype(F32)[:, None]
    col = (t % GRID_W).astype(F32)[:, None]
    tpos = t.astype(F32)[:, None]
    lane = jnp.arange(128)
    fa = ROPE_BASE ** (-(lane % 32).astype(F32) / 32)
    pos_a = jnp.where((lane < 64)[None, :], row, col)
    tab_a = _rope_tables(pos_a, fa, (lane % 64) < 32, pad_rows)
    lane = jnp.arange(256)
    fb = ROPE_BASE ** (-(lane % 32).astype(F32) / 32)
    tab_r = _rope_tables(jnp.broadcast_to(tpos, (s, 256)), fb, (lane % 64) < 32, pad_rows)
    lane = jnp.arange(128)
    fm = ROPE_BASE ** (-(lane % 16).astype(F32) / 16)
    pos_m = jnp.where((lane < 32)[None, :], row, col)
    tab_m = _rope_tables(pos_m, fm, (lane % 32) < 16, pad_rows, live_lane=lane < 64)
    return tab_a, tab_r, tab_m


def _pad_cols(w, n):
    return jnp.pad(w, [(0, 0)] * (w.ndim - 1) + [(0, n - w.shape[-1])])


def kernel(x, c, ctx, c_ctx, w_ada, b_ada, norm1_w, norm2_w, w_in, a_q_norm, a_k_norm, a_sink,
           r_decay, r_norm, s_conv_w, s_conv_b, s_a_log, s_dt_bias, s_d, s_norm,
           m_cq_norm, m_ckv_norm, m_w_uq, m_w_ukv, m_q_norm, m_k_norm,
           w_branch, w_o, w_ff1, w_ff2):
    b, s, d = x.shape
    lc = ctx.shape[1]
    depth = w_ada.shape[0]
    nl, ncx = b * s, b * lc
    t = nl + ncx
    dims = (b, s, lc, nl, t)
    tm_big = max(m for m in (1024, 512, 256) if s % m == 0 and ncx % m == 0)
    tm = max(m for m in (512, 256) if s % m == 0 and ncx % m == 0)
    tc = lc
    tn_in, tn_merge, tf = 1024, 512, 1024
    assert lc == 256 and s % 256 == 0 and b + 1 <= 8

    w_in16 = w_in.astype(BF)
    w_mix = jnp.concatenate([_pad_cols(w_in16[..., 4112:4816], W_M),
                             _pad_cols(w_in16[..., 4096:4112], W_DT), w_in16[..., 0:4096]], axis=-1)
    w_gate = w_in16[..., 4816:]
    wq = _pad_cols(m_w_uq.reshape(depth, M_Q_RANK, M_HEADS, M_QK_DIM), 256)
    wq = wq.reshape(depth, M_Q_RANK, M_HEADS * 256).astype(BF)
    wkv = m_w_ukv.reshape(depth, M_KV_RANK, M_HEADS, M_NOPE + M_V)
    wkv = jnp.concatenate([wkv[..., :M_NOPE].reshape(depth, M_KV_RANK, M_HEADS * M_NOPE),
                           wkv[..., M_NOPE:].reshape(depth, M_KV_RANK, M_HEADS * M_V)],
                          axis=-1).astype(BF)
    wb = w_branch.astype(BF)
    wo = w_o.astype(BF)
    w1 = w_ff1.astype(BF)
    w2 = w_ff2.astype(BF)
    log_gamma = jnp.log1p(-jnp.exp(r_decay.astype(F32)))
    a_neg = _pad_cols(-jnp.exp(s_a_log.astype(F32)).reshape(depth, 1, 2 * S_HEADS), 128)
    dtb = _pad_cols(s_dt_bias.astype(F32).reshape(depth, 1, 2 * S_HEADS), 128)
    dsk = jnp.repeat(s_d.astype(F32), S_HEAD_DIM, axis=-1)[:, None, :]
    mqw = _pad_cols(m_q_norm, 256)[:, None, :]
    mkw = _pad_cols(m_k_norm, 256)[:, None, :]
    tab_a, tab_r, tab_m = _tables(s, tm)

    c8 = jnp.zeros((8, d), F32).at[:b].set(c).at[b].set(c_ctx)
    mod_all = _ada(c8, w_ada, b_ada).reshape(depth, 8, N_MOD, d)

    xs = jnp.concatenate([x.reshape(nl, d), ctx.reshape(ncx, d)], axis=0)
    for i in range(depth):
        need_ctx = i < depth - 1
        rows = t if need_ctx else nl
        mod = mod_all[i]
        p, pdt = _inproj(xs, mod, norm1_w[i][None, :], w_mix, w_gate, i, dims, tm_big, tn_in)
        qa, ka, va = _aprep(p, tab_a, a_q_norm[i][None, :], a_k_norm[i][None, :], dims, tm)
        ya = _aattn(a_sink[i].astype(F32), qa, ka, va, dims, need_ctx)
        u = _conv(p, s_conv_w[i], s_conv_b[i][None, :], dims, tc)
        rf, rb, sf, sb = _scans(log_gamma[i], p, tab_r, a_neg[i], dtb[i], u, pdt, dims)
        yb, yc = _scan_finish(rf, rb, p, r_norm[i].reshape(1, -1), sf, sb, u, dsk[i],
                              s_norm[i][None, :], rows, tm)
        qm, km, vm = _mprep(p, tab_m, m_cq_norm[i][None, :],
                            m_ckv_norm[i][None, :], wq[i], wkv[i], mqw[i], mkw[i], dims, tm, tc)
        yd = _mla(qm, km, vm, dims, need_ctx, tc, tc)
        xs = _merge(xs, mod, (ya, yb, yc, yd), p, wb, wo, i, dims, rows, tm, tn_merge)
        xs = _ffn(xs, mod, norm2_w[i][None, :], w1, w2, i, dims, rows, tm, tf)
    return xs.reshape(b, s, d)
```
